```python
import math
import jax, jax.numpy as jnp
from jax import lax
import numpy as np

D_MODEL = 1024
BATCH = 8
SEQ = 2048
DEPTH = 1
DEC_BATCH = 128
DEC_SEQ = 1
PAST_LEN = 16384
PAGE_SIZE = 128

H_A = 4
DK_A = 128
DV_A = 128
CONV_W = 4
H_B = 4
DK_B = 128
DV_B = 256
ROPE_BASE = 10000.0
CHUNK = 64
N_KEYS = 128
N_EXPERTS = N_KEYS * N_KEYS
PEER_HEADS = 8
PEER_DQ = 256
PEER_TOPK = 16
PEER_BLOCK = 128
EPS = 1e-6

QA = H_A * DK_A
VA = H_A * DV_A
QB = H_B * DK_B
VB = H_B * DV_B
CONV_CH = 2 * QA + VA
IN_SIZES = (CONV_CH, VA, H_A, H_A, QB, QB, VB, VB, D_MODEL, D_MODEL)
IN_DIM = CONV_CH + VA + 2 * H_A + 2 * QB + 2 * VB + 2 * D_MODEL

kernel_name = "hybrid_deltanet_retention_peer_step"


def rmsnorm(x, g):
    xf = x.astype(jnp.float32)
    y = xf * lax.rsqrt(jnp.mean(xf * xf, axis=-1, keepdims=True) + EPS)
    return (y * g.astype(jnp.float32)).astype(x.dtype)


def l2norm(x):
    xf = x.astype(jnp.float32)
    return xf * lax.rsqrt(jnp.sum(xf * xf, axis=-1, keepdims=True) + EPS)


def split_cols(t, sizes):
    out, o = [], 0
    for s in sizes:
        out.append(t[..., o:o + s])
        o += s
    return out


def chunk_len(L):
    return CHUNK if L % CHUNK == 0 else L


def to_chunks(t, C):
    B, L = t.shape[0], t.shape[1]
    return t.reshape((B, L // C, C) + t.shape[2:]).swapaxes(2, 3)


def from_chunks(o):
    B, N, H, C, D = o.shape
    return o.swapaxes(2, 3).reshape(B, N * C, H, D)


def causal_conv(x, buf, w):
    L = x.shape[1]
    xp = jnp.concatenate([buf.astype(x.dtype), x], axis=1)
    out = xp[:, 0:L] * w[0]
    for i in range(1, CONV_W):
        out = out + xp[:, i:i + L] * w[i]
    return jax.nn.silu(out), xp[:, -(CONV_W - 1):]


def rope(x, pos):
    half = x.shape[-1] // 2
    inv = ROPE_BASE ** (-jnp.arange(half, dtype=jnp.float32) / half)
    ang = pos[:, None] * inv[None, :]
    cos = jnp.cos(ang)[None, :, None, :]
    sin = jnp.sin(ang)[None, :, None, :]
    x1, x2 = x[..., :half], x[..., half:]
    return jnp.concatenate([x1 * cos - x2 * sin, x2 * cos + x1 * sin], axis=-1)


def gated_delta(q, k, v, beta, logd, S0):
    f32 = jnp.float32
    C = chunk_len(q.shape[1])
    q, k, v = (to_chunks(t.astype(f32), C) for t in (q, k, v))
    beta, logd = to_chunks(beta, C), to_chunks(logd, C)
    G = jnp.cumsum(logd, axis=-1)
    tril = jnp.tril(jnp.ones((C, C), bool))
    strict = jnp.tril(jnp.ones((C, C), bool), -1)
    decay_in = jnp.exp(jnp.where(tril, G[..., :, None] - G[..., None, :], -jnp.inf))
    kb = k * beta[..., None]
    Lmat = jnp.where(strict, jnp.einsum('bnhid,bnhjd->bnhij', kb, k) * decay_in, 0.0)
    eye = jnp.eye(C, dtype=f32)
    T = lax.linalg.triangular_solve(Lmat + eye, jnp.broadcast_to(eye, Lmat.shape),
                                    left_side=True, lower=True, unit_diagonal=True)
    u = jnp.einsum('bnhij,bnhje->bnhie', T, v * beta[..., None])
    w = jnp.einsum('bnhij,bnhjd->bnhid', T, kb * jnp.exp(G)[..., None])
    qk = jnp.einsum('bnhid,bnhjd->bnhij', q, k) * decay_in
    qg = q * jnp.exp(G)[..., None]
    kd = k * jnp.exp(G[..., -1:] - G)[..., None]
    gC = jnp.exp(G[..., -1])

    def step(S, xs):
        u_c, w_c, qk_c, qg_c, kd_c, g_c = xs
        v_new = u_c - jnp.einsum('bhcd,bhde->bhce', w_c, S)
        o = jnp.einsum('bhcd,bhde->bhce', qg_c, S) + jnp.einsum('bhij,bhje->bhie', qk_c, v_new)
        S = S * g_c[..., None, None] + jnp.einsum('bhcd,bhce->bhde', kd_c, v_new)
        return S, o

    xs = tuple(jnp.moveaxis(t, 1, 0) for t in (u, w, qk, qg, kd, gC))
    S, o = lax.scan(step, S0.astype(f32), xs)
    return from_chunks(jnp.moveaxis(o, 0, 1)), S


def retention(q, k, v, S0):
    f32 = jnp.float32
    C = chunk_len(q.shape[1])
    q, k, v = (to_chunks(t.astype(f32), C) for t in (q, k, v))
    logg = jnp.log1p(-(2.0 ** (-5.0 - jnp.arange(H_B, dtype=f32))))
    idx = jnp.arange(C, dtype=f32)
    tril = jnp.tril(jnp.ones((C, C), bool))
    D = jnp.exp(jnp.where(tril[None], (idx[:, None] - idx[None, :])[None] * logg[:, None, None], -jnp.inf))
    q_dec = jnp.exp((idx[None, :] + 1.0) * logg[:, None])
    k_dec = jnp.exp((C - 1.0 - idx[None, :]) * logg[:, None])
    gC = jnp.exp(C * logg)
    inner = jnp.einsum('bnhij,bnhje->bnhie', jnp.einsum('bnhid,bnhjd->bnhij', q, k) * D, v)
    qd = q * q_dec[:, :, None]
    kd = k * k_dec[:, :, None]

    def step(S, xs):
        inner_c, qd_c, kd_c, v_c = xs
        o = inner_c + jnp.einsum('bhcd,bhde->bhce', qd_c, S)
        S = S * gC[:, None, None] + jnp.einsum('bhcd,bhce->bhde', kd_c, v_c)
        return S, o

    xs = tuple(jnp.moveaxis(t, 1, 0) for t in (inner, qd, kd, v))
    S, o = lax.scan(step, S0.astype(f32), xs)
    return from_chunks(jnp.moveaxis(o, 0, 1)), S


def token_mixers(xn, conv_buf, s_delta, s_ret, pos, w_in, conv_w, a_log, dt_bias,
                 gn_a, gn_b, w_br_a, w_br_b, w_out):
    f32 = jnp.float32
    B, L, _ = xn.shape
    proj = jnp.einsum('bld,de->ble', xn, w_in)
    qkv_a, gate_a, b_a, a_a, q_b, k_b, v_b, gate_b, sel_a, sel_b = split_cols(proj, IN_SIZES)
    qkv_a, conv_new = causal_conv(qkv_a, conv_buf, conv_w)
    q_a, k_a, v_a = split_cols(qkv_a, (QA, QA, VA))
    q_a = l2norm(q_a.reshape(B, L, H_A, DK_A)) * (DK_A ** -0.5)
    k_a = l2norm(k_a.reshape(B, L, H_A, DK_A))
    v_a = v_a.reshape(B, L, H_A, DV_A)
    beta = jax.nn.sigmoid(b_a.astype(f32))
    logd = -jnp.exp(a_log.astype(f32)) * jax.nn.softplus(a_a.astype(f32) + dt_bias.astype(f32))
    o_a, s_delta_new = gated_delta(q_a, k_a, v_a, beta, logd, s_delta)
    o_a = rmsnorm(o_a, gn_a) * jax.nn.silu(gate_a.reshape(B, L, H_A, DV_A).astype(f32))
    o_a = o_a.reshape(B, L, VA).astype(xn.dtype)
    q_b = rope(q_b.reshape(B, L, H_B, DK_B).astype(f32), pos)
    k_b = rope(k_b.reshape(B, L, H_B, DK_B).astype(f32), pos) * (DK_B ** -0.5)
    v_b = v_b.reshape(B, L, H_B, DV_B)
    o_b, s_ret_new = retention(q_b, k_b, v_b, s_ret)
    o_b = rmsnorm(o_b, gn_b) * jax.nn.silu(gate_b.reshape(B, L, H_B, DV_B).astype(f32))
    o_b = o_b.reshape(B, L, VB).astype(xn.dtype)
    merged = (jax.nn.sigmoid(sel_a) * jnp.einsum('blv,vd->bld', o_a, w_br_a)
              + jax.nn.sigmoid(sel_b) * jnp.einsum('blv,vd->bld', o_b, w_br_b))
    return jnp.einsum('bld,de->ble', merged, w_out), conv_new, s_delta_new, s_ret_new


def peer(h, w_q, keys, u_tab, v_tab):
    T = h.shape[0]
    nblk = -(-T // PEER_BLOCK)
    hp = jnp.pad(h, ((0, nblk * PEER_BLOCK - T), (0, 0))).reshape(nblk, PEER_BLOCK, D_MODEL)
    n_cand = PEER_TOPK * PEER_TOPK

    def block(xb):
        q = jnp.einsum('td,de->te', xb, w_q).reshape(PEER_BLOCK, PEER_HEADS, 2, PEER_DQ // 2)
        s = jnp.einsum('thpd,hpkd->thpk', q, keys).astype(jnp.float32)
        s_top, i_top = lax.top_k(s, PEER_TOPK)
        cand = (s_top[:, :, 0, :, None] + s_top[:, :, 1, None, :]).reshape(PEER_BLOCK, PEER_HEADS, n_cand)
        cand_idx = (i_top[:, :, 0, :, None] * N_KEYS + i_top[:, :, 1, None, :]).reshape(PEER_BLOCK, PEER_HEADS, n_cand)
        best, sel = lax.top_k(cand, PEER_TOPK)
        eidx = jnp.take_along_axis(cand_idx, sel, axis=-1)
        gate = jax.nn.softmax(best, axis=-1)
        act = jax.nn.gelu(jnp.einsum('td,thkd->thk', xb, u_tab[eidx]).astype(jnp.float32), approximate=False)
        coef = (gate * act).astype(xb.dtype)
        return jnp.einsum('thk,thkd->td', coef, v_tab[eidx])

    return lax.map(block, hp).reshape(nblk * PEER_BLOCK, D_MODEL)[:T]


def trunk(x, conv_buf, s_delta, s_ret, pos, layer_w, norm_final):
    (norm_mix, w_in, conv_w, a_log, dt_bias, gn_a, gn_b, w_br_a, w_br_b, w_out,
     norm_ffn, peer_wq, peer_keys, peer_u, peer_v) = layer_w
    B, L, _ = x.shape
    h = x
    convs, deltas, rets = [], [], []
    for l in range(DEPTH):
        m, c_new, sd_new, sr_new = token_mixers(
            rmsnorm(h, norm_mix[l]), conv_buf[l], s_delta[l], s_ret[l], pos,
            w_in[l], conv_w[l], a_log[l], dt_bias[l], gn_a[l], gn_b[l], w_br_a[l], w_br_b[l], w_out[l])
        h = h + m
        f = peer(rmsnorm(h, norm_ffn[l]).reshape(B * L, D_MODEL), peer_wq[l], peer_keys[l], peer_u[l], peer_v[l])
        h = h + f.reshape(B, L, D_MODEL)
        convs.append(c_new)
        deltas.append(sd_new)
        rets.append(sr_new)
    return rmsnorm(h, norm_final), jnp.stack(convs), jnp.stack(deltas), jnp.stack(rets)


def setup_inputs(seed: int = 0) -> dict:
    key = jax.random.key(seed)
    ks = jax.random.split(key, 24)

    def nrm(k, shape, scale):
        return jax.random.normal(k, shape, jnp.float32) * scale

    dt = jnp.exp(jax.random.uniform(ks[9], (DEPTH, H_A), jnp.float32, math.log(1e-3), math.log(1e-1)))
    return {
        "x_prompt": nrm(ks[0], (BATCH, SEQ, D_MODEL), 1.0),
        "x_sample": nrm(ks[1], (DEC_BATCH, DEC_SEQ, D_MODEL), 1.0),
        "state_conv_a": nrm(ks[2], (DEPTH, DEC_BATCH, CONV_W - 1, CONV_CH), 1.0),
        "state_delta": nrm(ks[3], (DEPTH, DEC_BATCH, H_A, DK_A, DV_A), DK_A ** -0.5),
        "state_ret": nrm(ks[4], (DEPTH, DEC_BATCH, H_B, DK_B, DV_B), 0.5),
        "norm_mix": 1.0 + nrm(ks[5], (DEPTH, D_MODEL), 0.02),
        "w_in": nrm(ks[6], (DEPTH, D_MODEL, IN_DIM), D_MODEL ** -0.5),
        "conv_w": nrm(ks[7], (DEPTH, CONV_W, CONV_CH), CONV_W ** -0.5),
        "a_log": jnp.log(jax.random.uniform(ks[8], (DEPTH, H_A), jnp.float32, 1.0, 16.0)),
        "dt_bias": dt + jnp.log(-jnp.expm1(-dt)),
        "gn_a": 1.0 + nrm(ks[10], (DEPTH, DV_A), 0.02),
        "gn_b": 1.0 + nrm(ks[11], (DEPTH, DV_B), 0.02),
        "w_br_a": nrm(ks[12], (DEPTH, VA, D_MODEL), VA ** -0.5),
        "w_br_b": nrm(ks[13], (DEPTH, VB, D_MODEL), VB ** -0.5),
        "w_out": nrm(ks[14], (DEPTH, D_MODEL, D_MODEL), D_MODEL ** -0.5),
        "norm_ffn": 1.0 + nrm(ks[15], (DEPTH, D_MODEL), 0.02),
        "peer_wq": nrm(ks[16], (DEPTH, D_MODEL, PEER_HEADS * PEER_DQ), D_MODEL ** -0.5),
        "peer_keys": nrm(ks[17], (DEPTH, PEER_HEADS, 2, N_KEYS, PEER_DQ // 2), (PEER_DQ // 2) ** -0.5),
        "peer_u": nrm(ks[18], (DEPTH, N_EXPERTS, D_MODEL), D_MODEL ** -0.5),
        "peer_v": nrm(ks[19], (DEPTH, N_EXPERTS, D_MODEL), PEER_HEADS ** -0.5),
        "norm_final": 1.0 + nrm(ks[20], (D_MODEL,), 0.02),
    }


def reference(x_prompt, x_sample, state_conv_a, state_delta, state_ret,
              norm_mix, w_in, conv_w, a_log, dt_bias, gn_a, gn_b, w_br_a, w_br_b, w_out,
              norm_ffn, peer_wq, peer_keys, peer_u, peer_v, norm_final):
    layer_w = (norm_mix, w_in, conv_w, a_log, dt_bias, gn_a, gn_b, w_br_a, w_br_b, w_out,
               norm_ffn, peer_wq, peer_keys, peer_u, peer_v)
    B, L = x_prompt.shape[0], x_prompt.shape[1]
    Bs, Ls = x_sample.shape[0], x_sample.shape[1]
    zero_conv = jnp.zeros((DEPTH, B, CONV_W - 1, CONV_CH), x_prompt.dtype)
    zero_delta = jnp.zeros((DEPTH, B, H_A, DK_A, DV_A), jnp.float32)
    zero_ret = jnp.zeros((DEPTH, B, H_B, DK_B, DV_B), jnp.float32)
    pos_p = jnp.arange(L, dtype=jnp.float32)
    y_prompt, conv_p, delta_p, ret_p = trunk(x_prompt, zero_conv, zero_delta, zero_ret, pos_p, layer_w, norm_final)
    pos_s = PAST_LEN + jnp.arange(Ls, dtype=jnp.float32)
    y_sample, conv_s, delta_s, ret_s = trunk(x_sample, state_conv_a, state_delta, state_ret, pos_s, layer_w, norm_final)
    return (y_prompt, y_sample,
            conv_p.astype(x_prompt.dtype), delta_p.astype(x_prompt.dtype), ret_p.astype(x_prompt.dtype),
            conv_s.astype(state_conv_a.dtype), delta_s.astype(state_delta.dtype), ret_s.astype(state_ret.dtype))
```

```python
import functools
import math

import jax
import jax.numpy as jnp
from jax import lax
from jax.experimental import pallas as pl
from jax.experimental.pallas import tpu as pltpu

F32 = jnp.float32
BF16 = jnp.bfloat16

EPS = 1e-6
D_MODEL = 1024
H_A, DK_A, DV_A, CONV_W = 4, 128, 128, 4
H_B, DK_B, DV_B = 4, 128, 256
ROPE_BASE = 10000.0
PAST_LEN = 16384
N_KEYS = 128
PEER_HEADS = 8
PEER_TOPK = 16
QA, VA, QB, VB = H_A * DK_A, H_A * DV_A, H_B * DK_B, H_B * DV_B
CONV_CH = 2 * QA + VA

COL_GATE_A, COL_QB, COL_KB, COL_VB = 1536, 2048, 2560, 3072
COL_GATE_B, COL_SEL_A, COL_SEL_B, N_MAIN = 4096, 5120, 6144, 7168
LANES = 128
SUBLANES = 8
CHUNK = 128
VMEM_LIMIT = 56 * 1024 * 1024


def _cparams(*sem):
    return pltpu.CompilerParams(dimension_semantics=sem, vmem_limit_bytes=VMEM_LIMIT)


def _dot(a, b):
    return jnp.dot(a.astype(BF16), b.astype(BF16), preferred_element_type=F32)


def _dot_nt(a, b):
    return lax.dot_general(a.astype(BF16), b.astype(BF16), (((1,), (1,)), ((), ())),
                           preferred_element_type=F32)


def _split2(a):
    hi = a.astype(BF16)
    return hi, (a - hi.astype(F32)).astype(BF16)


def _dot3(a, b):
    ah, al = _split2(a)
    bh, bl = _split2(b)
    d = functools.partial(jnp.dot, preferred_element_type=F32)
    return d(ah, bh) + d(al, bh) + d(ah, bl)


def _sigmoid(x):
    return 1.0 / (1.0 + jnp.exp(-x))


def _silu(x):
    return x * _sigmoid(x)


def _softplus(x):
    return jnp.maximum(x, 0.0) + jnp.log1p(jnp.exp(-jnp.abs(x)))


def _gelu(x):
    return 0.5 * x * (1.0 + lax.erf(x * (1.0 / math.sqrt(2.0))))


def _rope_kernel(cos_ref, sin_ref, *, pos0):
    shape = cos_ref.shape
    half = shape[1] // 2
    lane = lax.broadcasted_iota(jnp.int32, shape, 1)
    row = lax.broadcasted_iota(jnp.int32, shape, 0)
    j = jnp.where(lane >= half, lane - half, lane).astype(F32)
    inv = jnp.exp(j * (-math.log(ROPE_BASE) / half))
    ang = (row + pos0).astype(F32) * inv
    s = jnp.sin(ang)
    cos_ref[...] = jnp.cos(ang)
    sin_ref[...] = jnp.where(lane >= half, s, -s)


def _rope_tables(rows, pos0):
    sds = jax.ShapeDtypeStruct((rows, DK_B), F32)
    return pl.pallas_call(functools.partial(_rope_kernel, pos0=pos0), out_shape=(sds, sds),
                          name="rope_tables")()


def _rope(x, cos2, sin2):
    return x * cos2 + pltpu.roll(x, DK_B // 2, axis=1) * sin2


def _inproj_kernel(x_ref, g_ref, w_ref, wbh_ref, wbl_ref, proj_ref, ba_ref, xn_ref):
    @pl.when(pl.program_id(1) == 0)
    def _():
        x = x_ref[...]
        y = x * lax.rsqrt(jnp.mean(x * x, axis=-1, keepdims=True) + EPS) * g_ref[...]
        yh, yl = _split2(y)
        xn_ref[...] = yh
        d = functools.partial(jnp.dot, preferred_element_type=F32)
        ba_ref[...] = d(yh, wbh_ref[...]) + d(yl, wbh_ref[...]) + d(yh, wbl_ref[...])

    proj_ref[...] = jnp.dot(xn_ref[...], w_ref[...], preferred_element_type=F32)


def _inproj(x2, g, w_main, wb_hi, wb_lo):
    t = x2.shape[0]
    tm = min(t, 1024)
    tn = 1024
    return pl.pallas_call(
        _inproj_kernel,
        grid=(t // tm, N_MAIN // tn),
        in_specs=[pl.BlockSpec((tm, D_MODEL), lambda i, j: (i, 0)),
                  pl.BlockSpec((1, D_MODEL), lambda i, j: (0, 0)),
                  pl.BlockSpec((D_MODEL, tn), lambda i, j: (0, j)),
                  pl.BlockSpec((D_MODEL, LANES), lambda i, j: (0, 0)),
                  pl.BlockSpec((D_MODEL, LANES), lambda i, j: (0, 0))],
        out_specs=(pl.BlockSpec((tm, tn), lambda i, j: (i, j)),
                   pl.BlockSpec((tm, LANES), lambda i, j: (i, 0))),
        out_shape=(jax.ShapeDtypeStruct((t, N_MAIN), F32), jax.ShapeDtypeStruct((t, LANES), F32)),
        scratch_shapes=[pltpu.VMEM((tm, D_MODEL), BF16)],
        compiler_params=_cparams("parallel", "arbitrary"),
        name="rms_inproj",
    )(x2, g, w_main, wb_hi, wb_lo)


def _lane_pick(x, idx):
    lane = lax.broadcasted_iota(jnp.int32, x.shape, 1)
    return jnp.sum(jnp.where(lane == idx, x, 0.0), axis=1, keepdims=True)


def _decay_terms(ba, alog_row, dtb_row):
    beta = _sigmoid(ba)
    logd = -jnp.exp(alog_row) * _softplus(ba + dtb_row)
    return beta, logd


def _unit_lower_inverse(lm, masks):
    n = lm.shape[0]
    row = lax.broadcasted_iota(jnp.int32, (n, n), 0)
    col = lax.broadcasted_iota(jnp.int32, (n, n), 1)
    t = jnp.where(row == col, 1.0, 0.0) - lm * masks[0]
    for m in masks[1:]:
        t = t - _dot3(_dot3(t, lm * m), t)
    return t


def _doubling_masks(n):
    row = lax.broadcasted_iota(jnp.int32, (n, n), 0)
    col = lax.broadcasted_iota(jnp.int32, (n, n), 1)
    masks = []
    lvl = 0
    while (1 << lvl) < n:
        same = (row >> (lvl + 1)) == (col >> (lvl + 1))
        lower = ((row >> lvl) & 1) == 1
        left = ((col >> lvl) & 1) == 0
        masks.append(jnp.where(same & lower & left, 1.0, 0.0))
        lvl += 1
    return masks


def _delta_kernel(q_ref, k_ref, v_ref, ba_ref, wq_ref, wk_ref, wv_ref, alog_ref, dtb_ref,
                  o_ref, s_ref, pad_ref, qs_ref, ks_ref, vs_ref, bt_ref, ld_ref):
    h = pl.program_id(1)
    n = q_ref.shape[0]
    c = CHUNK

    def conv_silu(x_ref, w_ref):
        pad_ref[0:8, :] = jnp.zeros((8, LANES), F32)
        pad_ref[8:8 + n, :] = x_ref[...]
        base = 8 - (CONV_W - 1)
        acc = pad_ref[base:base + n, :] * w_ref[0:1, :]
        for i in range(1, CONV_W):
            acc = acc + pad_ref[base + i:base + i + n, :] * w_ref[i:i + 1, :]
        return _silu(acc)

    def l2n(x):
        return x * lax.rsqrt(jnp.sum(x * x, axis=-1, keepdims=True) + EPS)

    qs_ref[...] = l2n(conv_silu(q_ref, wq_ref)) * (DK_A ** -0.5)
    ks_ref[...] = l2n(conv_silu(k_ref, wk_ref))
    vs_ref[...] = conv_silu(v_ref, wv_ref)
    beta, logd = _decay_terms(ba_ref[...], alog_ref[...], dtb_ref[...])
    bt_ref[...] = beta
    ld_ref[...] = logd

    row = lax.broadcasted_iota(jnp.int32, (c, c), 0)
    col = lax.broadcasted_iota(jnp.int32, (c, c), 1)
    tril = row >= col
    strict = row > col
    tril_f = jnp.where(tril, 1.0, 0.0).astype(BF16)
    masks = _doubling_masks(c)

    def body(ci, s):
        rows = pl.ds(pl.multiple_of(ci * c, c), c)
        qc, kc, vc = qs_ref[rows, :], ks_ref[rows, :], vs_ref[rows, :]
        bcol = _lane_pick(bt_ref[rows, :], h)
        ldc = ld_ref[rows, :]
        l1 = ldc.astype(BF16)
        r1 = ldc - l1.astype(F32)
        l2 = r1.astype(BF16)
        l3 = (r1 - l2.astype(F32)).astype(BF16)
        d = functools.partial(jnp.dot, preferred_element_type=F32)
        gcol = _lane_pick(d(tril_f, l1) + d(tril_f, l2) + d(tril_f, l3), H_A + h)
        gmat = jnp.broadcast_to(gcol, (c, c))
        diff = gmat - gmat.T
        decay = jnp.where(tril, jnp.exp(jnp.where(tril, diff, 0.0)), 0.0)
        kb = kc * bcol
        lm = jnp.where(strict, _dot_nt(kb, kc) * decay, 0.0)
        t = _unit_lower_inverse(lm, masks)
        eg = jnp.exp(gcol)
        u = _dot(t, vc * bcol)
        w = _dot(t, kb * eg)
        qk = _dot_nt(qc, kc) * decay
        glast = gcol[c - 1:c, :]
        kd = kc * jnp.exp(glast - gcol)
        v_new = u - _dot(w, s)
        o_ref[rows, :] = _dot(qc * eg, s) + _dot(qk, v_new)
        return s * jnp.exp(glast) + _dot(kd.T, v_new)

    s_ref[...] = lax.fori_loop(0, n // c, body, jnp.zeros((DK_A, DV_A), F32))


def _delta_prompt(proj3, ba3, conv_w, alog_row, dtb_row):
    b, n, _ = proj3.shape
    hb = DK_A // LANES

    def col(off):
        return lambda i, h: (i, 0, off // LANES + h * hb)

    def wcol(off):
        return lambda i, h: (0, off // LANES + h * hb)

    seq = lambda off: pl.BlockSpec((None, n, LANES), col(off))
    wsp = lambda off: pl.BlockSpec((CONV_W, LANES), wcol(off))
    row = pl.BlockSpec((1, LANES), lambda i, h: (0, 0))
    return pl.pallas_call(
        _delta_kernel,
        grid=(b, H_A),
        in_specs=[seq(0), seq(QA), seq(2 * QA),
                  pl.BlockSpec((None, n, LANES), lambda i, h: (i, 0, 0)),
                  wsp(0), wsp(QA), wsp(2 * QA), row, row],
        out_specs=(pl.BlockSpec((None, n, DV_A), lambda i, h: (i, 0, h)),
                   pl.BlockSpec((None, None, DK_A, DV_A), lambda i, h: (i, h, 0, 0))),
        out_shape=(jax.ShapeDtypeStruct((b, n, VA), F32),
                   jax.ShapeDtypeStruct((b, H_A, DK_A, DV_A), F32)),
        scratch_shapes=[pltpu.VMEM((n + 8, LANES), F32)] + [pltpu.VMEM((n, LANES), F32)] * 5,
        compiler_params=_cparams("parallel", "parallel"),
        name="delta_prompt",
    )(proj3, proj3, proj3, ba3, conv_w, conv_w, conv_w, alog_row, dtb_row)


def _ret_kernel(q_ref, k_ref, v_ref, cos_ref, sin_ref, logg_ref, o_ref, s_ref, qs_ref, ks_ref):
    n = q_ref.shape[0]
    c = CHUNK
    cos2, sin2 = cos_ref[...], sin_ref[...]
    qs_ref[...] = _rope(q_ref[...], cos2, sin2)
    ks_ref[...] = _rope(k_ref[...], cos2, sin2) * (DK_B ** -0.5)

    logg = logg_ref[...]
    row = lax.broadcasted_iota(jnp.int32, (c, c), 0)
    col = lax.broadcasted_iota(jnp.int32, (c, c), 1)
    tril = row >= col
    dmat = jnp.where(tril, jnp.exp(jnp.where(tril, (row - col).astype(F32) * logg, 0.0)), 0.0)
    idx = lax.broadcasted_iota(jnp.int32, (c, LANES), 0).astype(F32)
    q_dec = jnp.exp((idx + 1.0) * logg)
    k_dec = jnp.exp((c - 1.0 - idx) * logg)
    g_c = jnp.exp(c * logg[:, 0:1])

    def body(ci, s):
        rows = pl.ds(pl.multiple_of(ci * c, c), c)
        qc, kc, vc = qs_ref[rows, :], ks_ref[rows, :], v_ref[rows, :]
        inner = _dot(_dot_nt(qc, kc) * dmat, vc)
        o_ref[rows, :] = inner + _dot(qc * q_dec, s)
        return s * g_c + _dot((kc * k_dec).T, vc)

    s_ref[...] = lax.fori_loop(0, n // c, body, jnp.zeros((DK_B, DV_B), F32))


def _ret_prompt(proj3, cos2, sin2, logg_tab):
    b, n, _ = proj3.shape
    return pl.pallas_call(
        _ret_kernel,
        grid=(b, H_B),
        in_specs=[pl.BlockSpec((None, n, DK_B), lambda i, h: (i, 0, COL_QB // DK_B + h)),
                  pl.BlockSpec((None, n, DK_B), lambda i, h: (i, 0, COL_KB // DK_B + h)),
                  pl.BlockSpec((None, n, DV_B), lambda i, h: (i, 0, COL_VB // DV_B + h)),
                  pl.BlockSpec((n, DK_B), lambda i, h: (0, 0)),
                  pl.BlockSpec((n, DK_B), lambda i, h: (0, 0)),
                  pl.BlockSpec((None, 1, LANES), lambda i, h: (h, 0, 0))],
        out_specs=(pl.BlockSpec((None, n, DV_B), lambda i, h: (i, 0, h)),
                   pl.BlockSpec((None, None, DK_B, DV_B), lambda i, h: (i, h, 0, 0))),
        out_shape=(jax.ShapeDtypeStruct((b, n, VB), F32),
                   jax.ShapeDtypeStruct((b, H_B, DK_B, DV_B), F32)),
        scratch_shapes=[pltpu.VMEM((n, DK_B), F32)] * 2,
        compiler_params=_cparams("parallel", "parallel"),
        name="retention_prompt",
    )(proj3, proj3, proj3, cos2, sin2, logg_tab)


def _sample_prep_kernel(proj_ref, ba_ref, cb_ref, cw_ref, alog_ref, dtb_ref, cos_ref, sin_ref,
                        cnew_ref, va_ref, qat_ref, kat_ref, qbt_ref, kbt_ref, bg_ref):
    x = proj_ref[:, 0:CONV_CH]
    acc = cb_ref[0] * cw_ref[0:1, :]
    for i in range(1, CONV_W - 1):
        acc = acc + cb_ref[i] * cw_ref[i:i + 1, :]
    acc = acc + x * cw_ref[CONV_W - 1:CONV_W, :]
    qkv = _silu(acc)
    for i in range(CONV_W - 2):
        cnew_ref[i] = cb_ref[i + 1]
    cnew_ref[CONV_W - 2] = x

    def l2n(v):
        return v * lax.rsqrt(jnp.sum(v * v, axis=-1, keepdims=True) + EPS)

    cos2, sin2 = cos_ref[0:1, :], sin_ref[0:1, :]
    for h in range(H_A):
        hs = slice(h * DK_A, (h + 1) * DK_A)
        qat_ref[hs, :] = (l2n(qkv[:, h * DK_A:(h + 1) * DK_A]) * (DK_A ** -0.5)).T
        kat_ref[hs, :] = l2n(qkv[:, QA + h * DK_A:QA + (h + 1) * DK_A]).T
    va_ref[...] = qkv[:, 2 * QA:]
    for h in range(H_B):
        hs = slice(h * DK_B, (h + 1) * DK_B)
        qbt_ref[hs, :] = _rope(proj_ref[:, COL_QB + h * DK_B:COL_QB + (h + 1) * DK_B], cos2, sin2).T
        kbt_ref[hs, :] = (_rope(proj_ref[:, COL_KB + h * DK_B:COL_KB + (h + 1) * DK_B], cos2, sin2)
                          * (DK_B ** -0.5)).T
    beta, logd = _decay_terms(ba_ref[...], alog_ref[...], dtb_ref[...])
    lane = lax.broadcasted_iota(jnp.int32, beta.shape, 1)
    bg_ref[...] = jnp.where(lane < H_A, beta, jnp.exp(logd))


def _sample_prep(proj, ba, cb3, conv_w, alog_row, dtb_row, cos2, sin2):
    n = proj.shape[0]
    sd = lambda *s: jax.ShapeDtypeStruct(s, F32)
    return pl.pallas_call(
        _sample_prep_kernel,
        out_shape=(sd(CONV_W - 1, n, CONV_CH), sd(n, VA), sd(QA, n), sd(QA, n), sd(QB, n), sd(QB, n),
                   sd(n, LANES)),
        compiler_params=pltpu.CompilerParams(vmem_limit_bytes=VMEM_LIMIT),
        name="sample_prep",
    )(proj, ba, cb3, conv_w, alog_row, dtb_row, cos2, sin2)


SAMPLE_ROWS = 8


def _sample_state_kernel(sd_ref, sr_ref, va_ref, vb_ref, bg_ref, qat_ref, kat_ref, qbt_ref, kbt_ref,
                         logg_ref, sdn_ref, srn_ref, oa_ref, ob_ref):
    base = pl.program_id(0) * SAMPLE_ROWS
    nseq = qat_ref.shape[1]
    lane = lax.broadcasted_iota(jnp.int32, (DK_A, nseq), 1)

    def column(t_ref, h, seq):
        blk = t_ref[h * DK_A:(h + 1) * DK_A, :]
        return jnp.sum(jnp.where(lane == seq, blk, 0.0), axis=1, keepdims=True)

    for j in range(SAMPLE_ROWS):
        seq = base + j
        bg_row = bg_ref[j:j + 1, :]
        for h in range(H_A):
            kcol, qcol = column(kat_ref, h, seq), column(qat_ref, h, seq)
            beta = _lane_pick(bg_row, h)
            eg = _lane_pick(bg_row, H_A + h)
            s0 = sd_ref[j, h]
            v = va_ref[j:j + 1, h * DV_A:(h + 1) * DV_A]
            ks = jnp.sum(kcol * s0, axis=0, keepdims=True)
            v_new = beta * v - (beta * eg) * ks
            s1 = s0 * eg + kcol * v_new
            sdn_ref[j, h] = s1
            oa_ref[j:j + 1, h * DV_A:(h + 1) * DV_A] = jnp.sum(qcol * s1, axis=0, keepdims=True)
        for h in range(H_B):
            kcol, qcol = column(kbt_ref, h, seq), column(qbt_ref, h, seq)
            gamma = jnp.exp(logg_ref[h][:, 0:1])
            v = vb_ref[j:j + 1, h * DV_B:(h + 1) * DV_B]
            s1 = sr_ref[j, h] * gamma + kcol * v
            srn_ref[j, h] = s1
            ob_ref[j:j + 1, h * DV_B:(h + 1) * DV_B] = jnp.sum(qcol * s1, axis=0, keepdims=True)


def _sample_state(sd, sr, va, proj, bg, qat, kat, qbt, kbt, logg_tab):
    n = sd.shape[0]
    r = SAMPLE_ROWS
    full = lambda a: pl.BlockSpec(a.shape, lambda i: (0,) * a.ndim)
    return pl.pallas_call(
        _sample_state_kernel,
        grid=(n // r,),
        in_specs=[pl.BlockSpec((r, H_A, DK_A, DV_A), lambda i: (i, 0, 0, 0)),
                  pl.BlockSpec((r, H_B, DK_B, DV_B), lambda i: (i, 0, 0, 0)),
                  pl.BlockSpec((r, VA), lambda i: (i, 0)),
                  pl.BlockSpec((r, VB), lambda i: (i, COL_VB // VB)),
                  pl.BlockSpec((r, LANES), lambda i: (i, 0)),
                  full(qat), full(kat), full(qbt), full(kbt), full(logg_tab)],
        out_specs=(pl.BlockSpec((r, H_A, DK_A, DV_A), lambda i: (i, 0, 0, 0)),
                   pl.BlockSpec((r, H_B, DK_B, DV_B), lambda i: (i, 0, 0, 0)),
                   pl.BlockSpec((r, VA), lambda i: (i, 0)),
                   pl.BlockSpec((r, VB), lambda i: (i, 0))),
        out_shape=(jax.ShapeDtypeStruct(sd.shape, F32), jax.ShapeDtypeStruct(sr.shape, F32),
                   jax.ShapeDtypeStruct((n, VA), F32), jax.ShapeDtypeStruct((n, VB), F32)),
        compiler_params=_cparams("parallel"),
        name="sample_state",
    )(sd, sr, va, proj, bg, qat, kat, qbt, kbt, logg_tab)


def _postmix_kernel(oa_ref, ob_ref, ga_ref, gb_ref, sa_ref, sb_ref, x_ref, gna_ref, gnb_ref,
                    wa_ref, wb_ref, wo_ref, nf_ref, wq_ref, keys_ref, h_ref, hnt_ref, st_ref):
    def gated(o_ref, g_ref, gn_ref, heads, dv):
        parts = []
        for h in range(heads):
            o = o_ref[:, h * dv:(h + 1) * dv]
            y = o * lax.rsqrt(jnp.mean(o * o, axis=-1, keepdims=True) + EPS) * gn_ref[...]
            parts.append(y * _silu(g_ref[:, h * dv:(h + 1) * dv]))
        return jnp.concatenate(parts, axis=1)

    br_a = _dot(gated(oa_ref, ga_ref, gna_ref, H_A, DV_A), wa_ref[...])
    br_b = _dot(gated(ob_ref, gb_ref, gnb_ref, H_B, DV_B), wb_ref[...])
    merged = _sigmoid(sa_ref[...]) * br_a + _sigmoid(sb_ref[...]) * br_b
    hres = x_ref[...] + _dot(merged, wo_ref[...])
    h_ref[...] = hres
    hn = hres * lax.rsqrt(jnp.mean(hres * hres, axis=-1, keepdims=True) + EPS) * nf_ref[...]
    hnt_ref[...] = hn.T.astype(BF16)
    q = _dot(hn, wq_ref[...])
    for hp in range(2 * PEER_HEADS):
        st_ref[hp] = _dot_nt(keys_ref[hp], q[:, hp * N_KEYS:(hp + 1) * N_KEYS])


def _postmix(oa, ob, proj, x2, gn_a, gn_b, w_a, w_b, w_o, norm_ffn, w_q, keys):
    t = x2.shape[0]
    tm = min(t, 256)
    full = lambda a: pl.BlockSpec(a.shape, lambda i: (0,) * a.ndim)
    pcol = lambda width, off: pl.BlockSpec((tm, width), lambda i: (i, off // width))
    return pl.pallas_call(
        _postmix_kernel,
        grid=(t // tm,),
        in_specs=[pl.BlockSpec((tm, VA), lambda i: (i, 0)), pl.BlockSpec((tm, VB), lambda i: (i, 0)),
                  pcol(VA, COL_GATE_A), pcol(VB, COL_GATE_B), pcol(D_MODEL, COL_SEL_A),
                  pcol(D_MODEL, COL_SEL_B), pl.BlockSpec((tm, D_MODEL), lambda i: (i, 0)),
                  full(gn_a), full(gn_b), full(w_a), full(w_b), full(w_o), full(norm_ffn), full(w_q),
                  full(keys)],
        out_specs=(pl.BlockSpec((tm, D_MODEL), lambda i: (i, 0)),
                   pl.BlockSpec((D_MODEL, tm), lambda i: (0, i)),
                   pl.BlockSpec((2 * PEER_HEADS, N_KEYS, tm), lambda i: (0, 0, i))),
        out_shape=(jax.ShapeDtypeStruct((t, D_MODEL), F32), jax.ShapeDtypeStruct((D_MODEL, t), BF16),
                   jax.ShapeDtypeStruct((2 * PEER_HEADS, N_KEYS, t), F32)),
        compiler_params=_cparams("parallel"),
        name="postmix",
    )(oa, ob, proj, proj, proj, proj, x2, gn_a, gn_b, w_a, w_b, w_o, norm_ffn, w_q, keys)


A_PER_CHUNK = 16
E_CHUNK = A_PER_CHUNK * N_KEYS


def _top_values(x, count):
    n = x.shape[0]
    iota = lax.broadcasted_iota(jnp.int32, x.shape, 0)
    vals = []
    for _ in range(count):
        m = jnp.max(x, axis=0, keepdims=True)
        first = jnp.min(jnp.where(x == m, iota, n), axis=0, keepdims=True)
        x = jnp.where(iota == first, -jnp.inf, x)
        vals.append(m)
    return vals


def _peer_kernel(hnt_ref, st_ref, u_ref, vt_ref, ft_ref, p1_ref, p2_ref, tau_ref, act_ref, coef_ref,
                 acc_ref):
    ci = pl.program_id(1)
    tb = hnt_ref.shape[1]

    @pl.when(ci == 0)
    def _():
        def head(h, carry):
            s1 = st_ref[2 * h]
            s2 = st_ref[2 * h + 1]
            a = _top_values(s1, PEER_TOPK)
            b = _top_values(s2, PEER_TOPK)
            bmat = jnp.concatenate(b, axis=0)
            cand = jnp.concatenate([a[r] + bmat for r in range(PEER_TOPK)], axis=0)
            best = _top_values(cand, PEER_TOPK)
            z = jnp.exp(best[0] - best[0])
            for r in range(1, PEER_TOPK):
                z = z + jnp.exp(best[r] - best[0])
            p1_ref[h] = jnp.exp(s1 - a[0]) / z
            p2_ref[h] = jnp.exp(s2 - b[0])
            tau_ref[h] = best[PEER_TOPK - 1]
            return carry

        lax.fori_loop(0, PEER_HEADS, head, 0)
        acc_ref[...] = jnp.zeros_like(acc_ref)

    act_ref[...] = jnp.dot(u_ref[...], hnt_ref[...], preferred_element_type=F32)

    def rows_of_a8(ag, carry):
        a0 = pl.multiple_of(ci * A_PER_CHUNK + ag * SUBLANES, SUBLANES)
        for tg in range(tb // LANES):
            lanes = slice(tg * LANES, (tg + 1) * LANES)
            s1t = [st_ref[2 * h, pl.ds(a0, SUBLANES), lanes] for h in range(PEER_HEADS)]
            p1t = [p1_ref[h, pl.ds(a0, SUBLANES), lanes] for h in range(PEER_HEADS)]
            for r in range(SUBLANES):
                rows = pl.ds(pl.multiple_of((ag * SUBLANES + r) * N_KEYS, N_KEYS), N_KEYS)
                gate = jnp.zeros((N_KEYS, LANES), F32)
                for h in range(PEER_HEADS):
                    keep = (s1t[h][r:r + 1, :] + st_ref[2 * h + 1, :, lanes]) >= tau_ref[h, :, lanes]
                    gate = gate + jnp.where(keep, p2_ref[h, :, lanes], 0.0) * p1t[h][r:r + 1, :]
                coef_ref[rows, lanes] = (gate * _gelu(act_ref[rows, lanes])).astype(BF16)
        return carry

    lax.fori_loop(0, A_PER_CHUNK // SUBLANES, rows_of_a8, 0)
    acc_ref[...] += jnp.dot(vt_ref[...], coef_ref[...], preferred_element_type=F32)

    @pl.when(ci == pl.num_programs(1) - 1)
    def _():
        ft_ref[...] = acc_ref[...]


def _peer(hnt, st, u_bf, vt_bf):
    t = hnt.shape[1]
    tb = min(t, 256)
    n_exp = u_bf.shape[0]
    return pl.pallas_call(
        _peer_kernel,
        grid=(t // tb, n_exp // E_CHUNK),
        in_specs=[pl.BlockSpec((D_MODEL, tb), lambda i, c: (0, i)),
                  pl.BlockSpec((2 * PEER_HEADS, N_KEYS, tb), lambda i, c: (0, 0, i)),
                  pl.BlockSpec((E_CHUNK, D_MODEL), lambda i, c: (c, 0)),
                  pl.BlockSpec((D_MODEL, E_CHUNK), lambda i, c: (0, c))],
        out_specs=pl.BlockSpec((D_MODEL, tb), lambda i, c: (0, i)),
        out_shape=jax.ShapeDtypeStruct((D_MODEL, t), F32),
        scratch_shapes=[pltpu.VMEM((PEER_HEADS, N_KEYS, tb), F32),
                        pltpu.VMEM((PEER_HEADS, N_KEYS, tb), F32),
                        pltpu.VMEM((PEER_HEADS, 1, tb), F32),
                        pltpu.VMEM((E_CHUNK, tb), F32),
                        pltpu.VMEM((E_CHUNK, tb), BF16),
                        pltpu.VMEM((D_MODEL, tb), F32)],
        compiler_params=_cparams("parallel", "arbitrary"),
        name="peer_dense",
    )(hnt, st, u_bf, vt_bf)


def _final_kernel(h_ref, ft_ref, g_ref, y_ref):
    hres = h_ref[...] + ft_ref[...].T
    y_ref[...] = hres * lax.rsqrt(jnp.mean(hres * hres, axis=-1, keepdims=True) + EPS) * g_ref[...]


def _final(h, ft, g):
    t = h.shape[0]
    tm = min(t, 512)
    return pl.pallas_call(
        _final_kernel,
        grid=(t // tm,),
        in_specs=[pl.BlockSpec((tm, D_MODEL), lambda i: (i, 0)),
                  pl.BlockSpec((D_MODEL, tm), lambda i: (0, i)),
                  pl.BlockSpec((1, D_MODEL), lambda i: (0, 0))],
        out_specs=pl.BlockSpec((tm, D_MODEL), lambda i: (i, 0)),
        out_shape=jax.ShapeDtypeStruct((t, D_MODEL), F32),
        compiler_params=_cparams("parallel"),
        name="final_norm",
    )(h, ft, g)


def _pack_weights(norm_mix, w_in, conv_w, a_log, dt_bias, gn_a, gn_b, w_br_a, w_br_b, w_out,
                  norm_ffn, peer_wq, peer_keys, peer_u, peer_v, norm_final):
    w = w_in[0]
    n_small = 2 * H_A
    c0 = CONV_CH + VA
    w_main = jnp.concatenate([w[:, :c0], w[:, c0 + n_small:]], axis=1).astype(BF16)
    w_ba = jnp.pad(w[:, c0:c0 + n_small], ((0, 0), (0, LANES - n_small)))
    wb_hi = w_ba.astype(BF16)
    wb_lo = (w_ba - wb_hi.astype(F32)).astype(BF16)
    pad_row = lambda v: jnp.pad(v.reshape(1, H_A), ((0, 0), (H_A, LANES - 2 * H_A)))
    logg = jnp.log1p(-(2.0 ** (-5.0 - jnp.arange(H_B, dtype=F32))))
    return dict(
        norm_mix=norm_mix[0].reshape(1, D_MODEL), w_main=w_main, wb_hi=wb_hi, wb_lo=wb_lo,
        conv_w=conv_w[0], alog_row=pad_row(a_log[0]), dtb_row=pad_row(dt_bias[0]),
        logg_tab=jnp.broadcast_to(logg[:, None, None], (H_B, 1, LANES)),
        gn_a=gn_a[0].reshape(1, DV_A), gn_b=gn_b[0].reshape(1, DV_B),
        w_a=w_br_a[0].astype(BF16), w_b=w_br_b[0].astype(BF16), w_o=w_out[0].astype(BF16),
        norm_ffn=norm_ffn[0].reshape(1, D_MODEL), w_q=peer_wq[0].astype(BF16),
        keys=peer_keys[0].reshape(2 * PEER_HEADS, N_KEYS, N_KEYS).astype(BF16),
        u_bf=peer_u[0].astype(BF16), vt_bf=peer_v[0].T.astype(BF16),
        norm_final=norm_final.reshape(1, D_MODEL))


def _channel_mix(oa, ob, proj, x2, p):
    h, hnt, st = _postmix(oa, ob, proj, x2, p["gn_a"], p["gn_b"], p["w_a"], p["w_b"], p["w_o"],
                          p["norm_ffn"], p["w_q"], p["keys"])
    ft = _peer(hnt, st, p["u_bf"], p["vt_bf"])
    return _final(h, ft, p["norm_final"])


def _prompt_group(x, p):
    b, n, _ = x.shape
    x2 = x.reshape(b * n, D_MODEL)
    proj, ba = _inproj(x2, p["norm_mix"], p["w_main"], p["wb_hi"], p["wb_lo"])
    proj3 = proj.reshape(b, n, N_MAIN)
    oa, s_delta = _delta_prompt(proj3, ba.reshape(b, n, LANES), p["conv_w"], p["alog_row"], p["dtb_row"])
    cos2, sin2 = _rope_tables(n, 0)
    ob, s_ret = _ret_prompt(proj3, cos2, sin2, p["logg_tab"])
    y = _channel_mix(oa.reshape(b * n, VA), ob.reshape(b * n, VB), proj, x2, p)
    conv_new = proj3[:, n - (CONV_W - 1):, :CONV_CH]
    return y.reshape(b, n, D_MODEL), conv_new[None], s_delta[None], s_ret[None]


def _sample_group(x, conv_buf, s_delta, s_ret, p):
    n = x.shape[0]
    x2 = x.reshape(n, D_MODEL)
    proj, ba = _inproj(x2, p["norm_mix"], p["w_main"], p["wb_hi"], p["wb_lo"])
    cos2, sin2 = _rope_tables(8, PAST_LEN)
    cb3 = jnp.transpose(conv_buf, (1, 0, 2))
    cnew, va, qat, kat, qbt, kbt, bg = _sample_prep(proj, ba, cb3, p["conv_w"], p["alog_row"],
                                                    p["dtb_row"], cos2, sin2)
    sd_new, sr_new, oa, ob = _sample_state(s_delta, s_ret, va, proj, bg, qat, kat, qbt, kbt,
                                           p["logg_tab"])
    y = _channel_mix(oa, ob, proj, x2, p)
    return (y.reshape(n, 1, D_MODEL), jnp.transpose(cnew, (1, 0, 2))[None], sd_new[None], sr_new[None])


def kernel(x_prompt, x_sample, state_conv_a, state_delta, state_ret, norm_mix, w_in, conv_w, a_log, dt_bias, gn_a, gn_b, w_br_a, w_br_b, w_out, norm_ffn, peer_wq, peer_keys, peer_u, peer_v, norm_final):
    assert w_in.shape[0] == 1 and x_sample.shape[1] == 1
    p = _pack_weights(norm_mix, w_in, conv_w, a_log, dt_bias, gn_a, gn_b, w_br_a, w_br_b, w_out,
                      norm_ffn, peer_wq, peer_keys, peer_u, peer_v, norm_final)
    y_p, conv_p, delta_p, ret_p = _prompt_group(x_prompt, p)
    y_s, conv_s, delta_s, ret_s = _sample_group(x_sample, state_conv_a[0], state_delta[0],
                                                state_ret[0], p)
    return (y_p, y_s, conv_p, delta_p, ret_p, conv_s, delta_s, ret_s)
```

```python
import functools
import math

import jax
import jax.numpy as jnp
from jax import lax
from jax.experimental import pallas as pl
from jax.experimental.pallas import tpu as pltpu

F32 = jnp.float32
BF16 = jnp.bfloat16

EPS = 1e-6
D_MODEL = 1024
H_A, DK_A, DV_A, CONV_W = 4, 128, 128, 4
H_B, DK_B, DV_B = 4, 128, 256
ROPE_BASE = 10000.0
PAST_LEN = 16384
N_KEYS = 128
PEER_HEADS = 8
PEER_TOPK = 16
QA, VA, QB, VB = H_A * DK_A, H_A * DV_A, H_B * DK_B, H_B * DV_B
CONV_CH = 2 * QA + VA

COL_GATE_A, COL_QB, COL_KB, COL_VB = 1536, 2048, 2560, 3072
COL_GATE_B, COL_SEL_A, COL_SEL_B, N_MAIN = 4096, 5120, 6144, 7168
LANES = 128
SUBLANES = 8
CHUNK = 128
VMEM_LIMIT = 56 * 1024 * 1024


def _cparams(*sem):
    return pltpu.CompilerParams(dimension_semantics=sem, vmem_limit_bytes=VMEM_LIMIT)


def _dot(a, b):
    return jnp.dot(a.astype(BF16), b.astype(BF16), preferred_element_type=F32)


def _dot_nt(a, b):
    return lax.dot_general(a.astype(BF16), b.astype(BF16), (((1,), (1,)), ((), ())),
                           preferred_element_type=F32)


def _split2(a):
    hi = a.astype(BF16)
    return hi, (a - hi.astype(F32)).astype(BF16)


def _dot3(a, b):
    ah, al = _split2(a)
    bh, bl = _split2(b)
    d = functools.partial(jnp.dot, preferred_element_type=F32)
    return d(ah, bh) + d(al, bh) + d(ah, bl)


def _sigmoid(x):
    return 1.0 / (1.0 + jnp.exp(-x))


def _silu(x):
    return x * _sigmoid(x)


def _softplus(x):
    return jnp.maximum(x, 0.0) + jnp.log1p(jnp.exp(-jnp.abs(x)))


def _gelu(x):
    return 0.5 * x * (1.0 + lax.erf(x * (1.0 / math.sqrt(2.0))))


def _rope_kernel(cos_ref, sin_ref, *, pos0):
    shape = cos_ref.shape
    half = shape[1] // 2
    lane = lax.broadcasted_iota(jnp.int32, shape, 1)
    row = lax.broadcasted_iota(jnp.int32, shape, 0)
    j = jnp.where(lane >= half, lane - half, lane).astype(F32)
    inv = jnp.exp(j * (-math.log(ROPE_BASE) / half))
    ang = (row + pos0).astype(F32) * inv
    s = jnp.sin(ang)
    cos_ref[...] = jnp.cos(ang)
    sin_ref[...] = jnp.where(lane >= half, s, -s)


def _rope_tables(rows, pos0):
    sds = jax.ShapeDtypeStruct((rows, DK_B), F32)
    return pl.pallas_call(functools.partial(_rope_kernel, pos0=pos0), out_shape=(sds, sds),
                          name="rope_tables")()


def _rope(x, cos2, sin2):
    return x * cos2 + pltpu.roll(x, DK_B // 2, axis=1) * sin2


def _inproj_kernel(x_ref, g_ref, w_ref, wbh_ref, wbl_ref, proj_ref, ba_ref, xn_ref):
    @pl.when(pl.program_id(1) == 0)
    def _():
        x = x_ref[...]
        y = x * lax.rsqrt(jnp.mean(x * x, axis=-1, keepdims=True) + EPS) * g_ref[...]
        yh, yl = _split2(y)
        xn_ref[...] = yh
        d = functools.partial(jnp.dot, preferred_element_type=F32)
        ba_ref[...] = d(yh, wbh_ref[...]) + d(yl, wbh_ref[...]) + d(yh, wbl_ref[...])

    proj_ref[...] = jnp.dot(xn_ref[...], w_ref[...], preferred_element_type=F32)


def _inproj(x2, g, w_main, wb_hi, wb_lo):
    t = x2.shape[0]
    tm = min(t, 1024)
    tn = 1024
    return pl.pallas_call(
        _inproj_kernel,
        grid=(t // tm, N_MAIN // tn),
        in_specs=[pl.BlockSpec((tm, D_MODEL), lambda i, j: (i, 0)),
                  pl.BlockSpec((1, D_MODEL), lambda i, j: (0, 0)),
                  pl.BlockSpec((D_MODEL, tn), lambda i, j: (0, j)),
                  pl.BlockSpec((D_MODEL, LANES), lambda i, j: (0, 0)),
                  pl.BlockSpec((D_MODEL, LANES), lambda i, j: (0, 0))],
        out_specs=(pl.BlockSpec((tm, tn), lambda i, j: (i, j)),
                   pl.BlockSpec((tm, LANES), lambda i, j: (i, 0))),
        out_shape=(jax.ShapeDtypeStruct((t, N_MAIN), F32), jax.ShapeDtypeStruct((t, LANES), F32)),
        scratch_shapes=[pltpu.VMEM((tm, D_MODEL), BF16)],
        compiler_params=_cparams("parallel", "arbitrary"),
        name="rms_inproj",
    )(x2, g, w_main, wb_hi, wb_lo)


def _lane_pick(x, idx):
    lane = lax.broadcasted_iota(jnp.int32, x.shape, 1)
    return jnp.sum(jnp.where(lane == idx, x, 0.0), axis=1, keepdims=True)


def _decay_terms(ba, alog_row, dtb_row):
    beta = _sigmoid(ba)
    logd = -jnp.exp(alog_row) * _softplus(ba + dtb_row)
    return beta, logd


def _unit_lower_inverse(lm, masks):
    n = lm.shape[0]
    row = lax.broadcasted_iota(jnp.int32, (n, n), 0)
    col = lax.broadcasted_iota(jnp.int32, (n, n), 1)
    t = jnp.where(row == col, 1.0, 0.0) - lm * masks[0]
    for m in masks[1:]:
        t = t - _dot3(_dot3(t, lm * m), t)
    return t


def _doubling_masks(n):
    row = lax.broadcasted_iota(jnp.int32, (n, n), 0)
    col = lax.broadcasted_iota(jnp.int32, (n, n), 1)
    masks = []
    lvl = 0
    while (1 << lvl) < n:
        same = (row >> (lvl + 1)) == (col >> (lvl + 1))
        lower = ((row >> lvl) & 1) == 1
        left = ((col >> lvl) & 1) == 0
        masks.append(jnp.where(same & lower & left, 1.0, 0.0))
        lvl += 1
    return masks


def _delta_kernel(q_ref, k_ref, v_ref, ba_ref, wq_ref, wk_ref, wv_ref, alog_ref, dtb_ref,
                  o_ref, s_ref, pad_ref, qs_ref, ks_ref, vs_ref, bt_ref, ld_ref):
    h = pl.program_id(1)
    n = q_ref.shape[0]
    c = CHUNK

    def conv_silu(x_ref, w_ref):
        pad_ref[0:8, :] = jnp.zeros((8, LANES), F32)
        pad_ref[8:8 + n, :] = x_ref[...]
        base = 8 - (CONV_W - 1)
        acc = pad_ref[base:base + n, :] * w_ref[0:1, :]
        for i in range(1, CONV_W):
            acc = acc + pad_ref[base + i:base + i + n, :] * w_ref[i:i + 1, :]
        return _silu(acc)

    def l2n(x):
        return x * lax.rsqrt(jnp.sum(x * x, axis=-1, keepdims=True) + EPS)

    qs_ref[...] = l2n(conv_silu(q_ref, wq_ref)) * (DK_A ** -0.5)
    ks_ref[...] = l2n(conv_silu(k_ref, wk_ref))
    vs_ref[...] = conv_silu(v_ref, wv_ref)
    beta, logd = _decay_terms(ba_ref[...], alog_ref[...], dtb_ref[...])
    bt_ref[...] = beta
    ld_ref[...] = logd

    row = lax.broadcasted_iota(jnp.int32, (c, c), 0)
    col = lax.broadcasted_iota(jnp.int32, (c, c), 1)
    tril = row >= col
    strict = row > col
    tril_f = jnp.where(tril, 1.0, 0.0).astype(BF16)
    masks = _doubling_masks(c)

    def body(ci, s):
        rows = pl.ds(pl.multiple_of(ci * c, c), c)
        qc, kc, vc = qs_ref[rows, :], ks_ref[rows, :], vs_ref[rows, :]
        bcol = _lane_pick(bt_ref[rows, :], h)
        ldc = ld_ref[rows, :]
        l1 = ldc.astype(BF16)
        r1 = ldc - l1.astype(F32)
        l2 = r1.astype(BF16)
        l3 = (r1 - l2.astype(F32)).astype(BF16)
        d = functools.partial(jnp.dot, preferred_element_type=F32)
        gcol = _lane_pick(d(tril_f, l1) + d(tril_f, l2) + d(tril_f, l3), H_A + h)
        gmat = jnp.broadcast_to(gcol, (c, c))
        diff = gmat - gmat.T
        decay = jnp.where(tril, jnp.exp(jnp.where(tril, diff, 0.0)), 0.0)
        kb = kc * bcol
        lm = jnp.where(strict, _dot_nt(kb, kc) * decay, 0.0)
        t = _unit_lower_inverse(lm, masks)
        eg = jnp.exp(gcol)
        u = _dot(t, vc * bcol)
        w = _dot(t, kb * eg)
        qk = _dot_nt(qc, kc) * decay
        glast = gcol[c - 1:c, :]
        kd = kc * jnp.exp(glast - gcol)
        v_new = u - _dot(w, s)
        o_ref[rows, :] = _dot(qc * eg, s) + _dot(qk, v_new)
        return s * jnp.exp(glast) + _dot(kd.T, v_new)

    s_ref[...] = lax.fori_loop(0, n // c, body, jnp.zeros((DK_A, DV_A), F32))


def _delta_prompt(proj3, ba3, conv_w, alog_row, dtb_row):
    b, n, _ = proj3.shape
    hb = DK_A // LANES

    def col(off):
        return lambda i, h: (i, 0, off // LANES + h * hb)

    def wcol(off):
        return lambda i, h: (0, off // LANES + h * hb)

    seq = lambda off: pl.BlockSpec((None, n, LANES), col(off))
    wsp = lambda off: pl.BlockSpec((CONV_W, LANES), wcol(off))
    row = pl.BlockSpec((1, LANES), lambda i, h: (0, 0))
    return pl.pallas_call(
        _delta_kernel,
        grid=(b, H_A),
        in_specs=[seq(0), seq(QA), seq(2 * QA),
                  pl.BlockSpec((None, n, LANES), lambda i, h: (i, 0, 0)),
                  wsp(0), wsp(QA), wsp(2 * QA), row, row],
        out_specs=(pl.BlockSpec((None, n, DV_A), lambda i, h: (i, 0, h)),
                   pl.BlockSpec((None, None, DK_A, DV_A), lambda i, h: (i, h, 0, 0))),
        out_shape=(jax.ShapeDtypeStruct((b, n, VA), F32),
                   jax.ShapeDtypeStruct((b, H_A, DK_A, DV_A), F32)),
        scratch_shapes=[pltpu.VMEM((n + 8, LANES), F32)] + [pltpu.VMEM((n, LANES), F32)] * 5,
        compiler_params=_cparams("parallel", "parallel"),
        name="delta_prompt",
    )(proj3, proj3, proj3, ba3, conv_w, conv_w, conv_w, alog_row, dtb_row)


def _ret_kernel(q_ref, k_ref, v_ref, cos_ref, sin_ref, logg_ref, o_ref, s_ref, qs_ref, ks_ref):
    n = q_ref.shape[0]
    c = CHUNK
    cos2, sin2 = cos_ref[...], sin_ref[...]
    qs_ref[...] = _rope(q_ref[...], cos2, sin2)
    ks_ref[...] = _rope(k_ref[...], cos2, sin2) * (DK_B ** -0.5)

    logg = logg_ref[...]
    row = lax.broadcasted_iota(jnp.int32, (c, c), 0)
    col = lax.broadcasted_iota(jnp.int32, (c, c), 1)
    tril = row >= col
    dmat = jnp.where(tril, jnp.exp(jnp.where(tril, (row - col).astype(F32) * logg, 0.0)), 0.0)
    idx = lax.broadcasted_iota(jnp.int32, (c, LANES), 0).astype(F32)
    q_dec = jnp.exp((idx + 1.0) * logg)
    k_dec = jnp.exp((c - 1.0 - idx) * logg)
    g_c = jnp.exp(c * logg[:, 0:1])

    def body(ci, s):
        rows = pl.ds(pl.multiple_of(ci * c, c), c)
        qc, kc, vc = qs_ref[rows, :], ks_ref[rows, :], v_ref[rows, :]
        inner = _dot(_dot_nt(qc, kc) * dmat, vc)
        o_ref[rows, :] = inner + _dot(qc * q_dec, s)
        return s * g_c + _dot((kc * k_dec).T, vc)

    s_ref[...] = lax.fori_loop(0, n // c, body, jnp.zeros((DK_B, DV_B), F32))


def _ret_prompt(proj3, cos2, sin2, logg_tab):
    b, n, _ = proj3.shape
    return pl.pallas_call(
        _ret_kernel,
        grid=(b, H_B),
        in_specs=[pl.BlockSpec((None, n, DK_B), lambda i, h: (i, 0, COL_QB // DK_B + h)),
                  pl.BlockSpec((None, n, DK_B), lambda i, h: (i, 0, COL_KB // DK_B + h)),
                  pl.BlockSpec((None, n, DV_B), lambda i, h: (i, 0, COL_VB // DV_B + h)),
                  pl.BlockSpec((n, DK_B), lambda i, h: (0, 0)),
                  pl.BlockSpec((n, DK_B), lambda i, h: (0, 0)),
                  pl.BlockSpec((None, 1, LANES), lambda i, h: (h, 0, 0))],
        out_specs=(pl.BlockSpec((None, n, DV_B), lambda i, h: (i, 0, h)),
                   pl.BlockSpec((None, None, DK_B, DV_B), lambda i, h: (i, h, 0, 0))),
        out_shape=(jax.ShapeDtypeStruct((b, n, VB), F32),
                   jax.ShapeDtypeStruct((b, H_B, DK_B, DV_B), F32)),
        scratch_shapes=[pltpu.VMEM((n, DK_B), F32)] * 2,
        compiler_params=_cparams("parallel", "parallel"),
        name="retention_prompt",
    )(proj3, proj3, proj3, cos2, sin2, logg_tab)


def _sample_prep_kernel(proj_ref, ba_ref, cb_ref, cw_ref, alog_ref, dtb_ref, cos_ref, sin_ref,
                        cnew_ref, va_ref, qat_ref, kat_ref, qbt_ref, kbt_ref, bg_ref):
    x = proj_ref[:, 0:CONV_CH]
    acc = cb_ref[0] * cw_ref[0:1, :]
    for i in range(1, CONV_W - 1):
        acc = acc + cb_ref[i] * cw_ref[i:i + 1, :]
    acc = acc + x * cw_ref[CONV_W - 1:CONV_W, :]
    qkv = _silu(acc)
    for i in range(CONV_W - 2):
        cnew_ref[i] = cb_ref[i + 1]
    cnew_ref[CONV_W - 2] = x

    def l2n(v):
        return v * lax.rsqrt(jnp.sum(v * v, axis=-1, keepdims=True) + EPS)

    cos2, sin2 = cos_ref[0:1, :], sin_ref[0:1, :]
    for h in range(H_A):
        hs = slice(h * DK_A, (h + 1) * DK_A)
        qat_ref[hs, :] = (l2n(qkv[:, h * DK_A:(h + 1) * DK_A]) * (DK_A ** -0.5)).T
        kat_ref[hs, :] = l2n(qkv[:, QA + h * DK_A:QA + (h + 1) * DK_A]).T
    va_ref[...] = qkv[:, 2 * QA:]
    for h in range(H_B):
        hs = slice(h * DK_B, (h + 1) * DK_B)
        qbt_ref[hs, :] = _rope(proj_ref[:, COL_QB + h * DK_B:COL_QB + (h + 1) * DK_B], cos2, sin2).T
        kbt_ref[hs, :] = (_rope(proj_ref[:, COL_KB + h * DK_B:COL_KB + (h + 1) * DK_B], cos2, sin2)
                          * (DK_B ** -0.5)).T
    beta, logd = _decay_terms(ba_ref[...], alog_ref[...], dtb_ref[...])
    lane = lax.broadcasted_iota(jnp.int32, beta.shape, 1)
    bg_ref[...] = jnp.where(lane < H_A, beta, jnp.exp(logd))


def _sample_prep(proj, ba, cb3, conv_w, alog_row, dtb_row, cos2, sin2):
    n = proj.shape[0]
    sd = lambda *s: jax.ShapeDtypeStruct(s, F32)
    return pl.pallas_call(
        _sample_prep_kernel,
        out_shape=(sd(CONV_W - 1, n, CONV_CH), sd(n, VA), sd(QA, n), sd(QA, n), sd(QB, n), sd(QB, n),
                   sd(n, LANES)),
        compiler_params=pltpu.CompilerParams(vmem_limit_bytes=VMEM_LIMIT),
        name="sample_prep",
    )(proj, ba, cb3, conv_w, alog_row, dtb_row, cos2, sin2)


SAMPLE_ROWS = 8


def _sample_state_kernel(sd_ref, sr_ref, va_ref, vb_ref, bg_ref, qat_ref, kat_ref, qbt_ref, kbt_ref,
                         logg_ref, sdn_ref, srn_ref, oa_ref, ob_ref):
    base = pl.program_id(0) * SAMPLE_ROWS
    nseq = qat_ref.shape[1]
    lane = lax.broadcasted_iota(jnp.int32, (DK_A, nseq), 1)

    def column(t_ref, h, seq):
        blk = t_ref[h * DK_A:(h + 1) * DK_A, :]
        return jnp.sum(jnp.where(lane == seq, blk, 0.0), axis=1, keepdims=True)

    for j in range(SAMPLE_ROWS):
        seq = base + j
        bg_row = bg_ref[j:j + 1, :]
        for h in range(H_A):
            kcol, qcol = column(kat_ref, h, seq), column(qat_ref, h, seq)
            beta = _lane_pick(bg_row, h)
            eg = _lane_pick(bg_row, H_A + h)
            s0 = sd_ref[j, h]
            v = va_ref[j:j + 1, h * DV_A:(h + 1) * DV_A]
            ks = jnp.sum(kcol * s0, axis=0, keepdims=True)
            v_new = beta * v - (beta * eg) * ks
            s1 = s0 * eg + kcol * v_new
            sdn_ref[j, h] = s1
            oa_ref[j:j + 1, h * DV_A:(h + 1) * DV_A] = jnp.sum(qcol * s1, axis=0, keepdims=True)
        for h in range(H_B):
            kcol, qcol = column(kbt_ref, h, seq), column(qbt_ref, h, seq)
            gamma = jnp.exp(logg_ref[h][:, 0:1])
            v = vb_ref[j:j + 1, h * DV_B:(h + 1) * DV_B]
            s1 = sr_ref[j, h] * gamma + kcol * v
            srn_ref[j, h] = s1
            ob_ref[j:j + 1, h * DV_B:(h + 1) * DV_B] = jnp.sum(qcol * s1, axis=0, keepdims=True)


def _sample_state(sd, sr, va, proj, bg, qat, kat, qbt, kbt, logg_tab):
    n = sd.shape[0]
    r = SAMPLE_ROWS
    full = lambda a: pl.BlockSpec(a.shape, lambda i: (0,) * a.ndim)
    return pl.pallas_call(
        _sample_state_kernel,
        grid=(n // r,),
        in_specs=[pl.BlockSpec((r, H_A, DK_A, DV_A), lambda i: (i, 0, 0, 0)),
                  pl.BlockSpec((r, H_B, DK_B, DV_B), lambda i: (i, 0, 0, 0)),
                  pl.BlockSpec((r, VA), lambda i: (i, 0)),
                  pl.BlockSpec((r, VB), lambda i: (i, COL_VB // VB)),
                  pl.BlockSpec((r, LANES), lambda i: (i, 0)),
                  full(qat), full(kat), full(qbt), full(kbt), full(logg_tab)],
        out_specs=(pl.BlockSpec((r, H_A, DK_A, DV_A), lambda i: (i, 0, 0, 0)),
                   pl.BlockSpec((r, H_B, DK_B, DV_B), lambda i: (i, 0, 0, 0)),
                   pl.BlockSpec((r, VA), lambda i: (i, 0)),
                   pl.BlockSpec((r, VB), lambda i: (i, 0))),
        out_shape=(jax.ShapeDtypeStruct(sd.shape, F32), jax.ShapeDtypeStruct(sr.shape, F32),
                   jax.ShapeDtypeStruct((n, VA), F32), jax.ShapeDtypeStruct((n, VB), F32)),
        compiler_params=_cparams("parallel"),
        name="sample_state",
    )(sd, sr, va, proj, bg, qat, kat, qbt, kbt, logg_tab)


def _postmix_kernel(oa_ref, ob_ref, ga_ref, gb_ref, sa_ref, sb_ref, x_ref, gna_ref, gnb_ref,
                    wa_ref, wb_ref, wo_ref, nf_ref, wq_ref, keys_ref, h_ref, hnt_ref, st_ref):
    def gated(o_ref, g_ref, gn_ref, heads, dv):
        parts = []
        for h in range(heads):
            o = o_ref[:, h * dv:(h + 1) * dv]
            y = o * lax.rsqrt(jnp.mean(o * o, axis=-1, keepdims=True) + EPS) * gn_ref[...]
            parts.append(y * _silu(g_ref[:, h * dv:(h + 1) * dv]))
        return jnp.concatenate(parts, axis=1)

    br_a = _dot(gated(oa_ref, ga_ref, gna_ref, H_A, DV_A), wa_ref[...])
    br_b = _dot(gated(ob_ref, gb_ref, gnb_ref, H_B, DV_B), wb_ref[...])
    merged = _sigmoid(sa_ref[...]) * br_a + _sigmoid(sb_ref[...]) * br_b
    hres = x_ref[...] + _dot(merged, wo_ref[...])
    h_ref[...] = hres
    hn = hres * lax.rsqrt(jnp.mean(hres * hres, axis=-1, keepdims=True) + EPS) * nf_ref[...]
    hnt_ref[...] = hn.T.astype(BF16)
    q = _dot(hn, wq_ref[...])
    for hp in range(2 * PEER_HEADS):
        st_ref[hp] = _dot_nt(keys_ref[hp], q[:, hp * N_KEYS:(hp + 1) * N_KEYS])


def _postmix(oa, ob, proj, x2, gn_a, gn_b, w_a, w_b, w_o, norm_ffn, w_q, keys):
    t = x2.shape[0]
    tm = min(t, 256)
    full = lambda a: pl.BlockSpec(a.shape, lambda i: (0,) * a.ndim)
    pcol = lambda width, off: pl.BlockSpec((tm, width), lambda i: (i, off // width))
    return pl.pallas_call(
        _postmix_kernel,
        grid=(t // tm,),
        in_specs=[pl.BlockSpec((tm, VA), lambda i: (i, 0)), pl.BlockSpec((tm, VB), lambda i: (i, 0)),
                  pcol(VA, COL_GATE_A), pcol(VB, COL_GATE_B), pcol(D_MODEL, COL_SEL_A),
                  pcol(D_MODEL, COL_SEL_B), pl.BlockSpec((tm, D_MODEL), lambda i: (i, 0)),
                  full(gn_a), full(gn_b), full(w_a), full(w_b), full(w_o), full(norm_ffn), full(w_q),
                  full(keys)],
        out_specs=(pl.BlockSpec((tm, D_MODEL), lambda i: (i, 0)),
                   pl.BlockSpec((D_MODEL, tm), lambda i: (0, i)),
                   pl.BlockSpec((2 * PEER_HEADS, N_KEYS, tm), lambda i: (0, 0, i))),
        out_shape=(jax.ShapeDtypeStruct((t, D_MODEL), F32), jax.ShapeDtypeStruct((D_MODEL, t), BF16),
                   jax.ShapeDtypeStruct((2 * PEER_HEADS, N_KEYS, t), F32)),
        compiler_params=_cparams("parallel"),
        name="postmix",
    )(oa, ob, proj, proj, proj, proj, x2, gn_a, gn_b, w_a, w_b, w_o, norm_ffn, w_q, keys)


A_PER_CHUNK = SUBLANES
E_CHUNK = A_PER_CHUNK * N_KEYS


def _batcher_pairs(n):
    pairs = []
    p = 1
    while p < n:
        k = p
        while k >= 1:
            for j in range(k % p, n - k, 2 * k):
                for i in range(min(k, n - j - k)):
                    if (i + j) // (2 * p) == (i + j + k) // (2 * p):
                        pairs.append((i + j, i + j + k))
            k //= 2
        p *= 2
    return pairs


_SORT_TOPK = _batcher_pairs(PEER_TOPK)


def _vmax(x, y):
    if x is None:
        return y
    return x if y is None else jnp.maximum(x, y)


def _exchange(x, y):
    if x is None or y is None:
        return _vmax(x, y), None
    return jnp.maximum(x, y), jnp.minimum(x, y)


def _sort_desc(v):
    v = list(v)
    for i, j in _SORT_TOPK:
        v[i], v[j] = _exchange(v[i], v[j])
    return v


def _merge_top(v, w):
    n = len(v)
    c = [_vmax(v[i], w[n - 1 - i]) for i in range(n)]
    d = n // 2
    while d >= 1:
        for i in range(n):
            if (i & d) == 0:
                c[i], c[i + d] = _exchange(c[i], c[i + d])
        d //= 2
    return c


def _top_scores(x):
    v = _sort_desc([x[i * SUBLANES:(i + 1) * SUBLANES, :] for i in range(N_KEYS // SUBLANES)])
    shift = SUBLANES // 2
    while shift >= 1:
        v = _merge_top(v, [pltpu.roll(t, shift, axis=0) for t in v])
        shift //= 2
    return v


def _peer_select(st_ref, p1_ref, p2_ref, tau_ref, tb):
    sub = lax.broadcasted_iota(jnp.int32, (SUBLANES, LANES), 0)
    for tg in range(tb // LANES):
        lanes = slice(tg * LANES, (tg + 1) * LANES)
        top = [[None] * PEER_TOPK for _ in range(2)]
        for h in range(PEER_HEADS):
            for p in range(2):
                v = _top_scores(st_ref[2 * h + p, :, lanes])
                for r in range(PEER_TOPK):
                    top[p][r] = v[r] if h == 0 else jnp.where(sub == h, v[r], top[p][r])
        best = None
        for r1 in range(PEER_TOPK):
            col = [top[0][r1] + top[1][r2] if (r1 + 1) * (r2 + 1) <= PEER_TOPK else None
                   for r2 in range(PEER_TOPK)]
            best = col if best is None else _merge_top(best, col)
        z = jnp.ones((SUBLANES, LANES), F32)
        for r in range(1, PEER_TOPK):
            z = z + jnp.exp(best[r] - best[0])
        tau_ref[:, lanes] = best[PEER_TOPK - 1]
        for h in range(PEER_HEADS):
            row = slice(h, h + 1)
            p1_ref[h, :, lanes] = jnp.exp(st_ref[2 * h, :, lanes] - top[0][0][row, :]) / z[row, :]
            p2_ref[h, :, lanes] = jnp.exp(st_ref[2 * h + 1, :, lanes] - top[1][0][row, :])


def _peer_coefficients(chunk, st_ref, p1_ref, p2_ref, tau_ref, act_ref, coef_ref, tb):
    a0 = pl.multiple_of(chunk * A_PER_CHUNK, A_PER_CHUNK)
    for tg in range(tb // LANES):
        lanes = slice(tg * LANES, (tg + 1) * LANES)
        s1t = [st_ref[2 * h, pl.ds(a0, A_PER_CHUNK), lanes] for h in range(PEER_HEADS)]
        p1t = [p1_ref[h, pl.ds(a0, A_PER_CHUNK), lanes] for h in range(PEER_HEADS)]
        for r in range(A_PER_CHUNK):
            rows = slice(r * N_KEYS, (r + 1) * N_KEYS)
            gate = jnp.zeros((N_KEYS, LANES), F32)
            for h in range(PEER_HEADS):
                keep = (s1t[h][r:r + 1, :] + st_ref[2 * h + 1, :, lanes]) >= tau_ref[h:h + 1, lanes]
                gate = gate + jnp.where(keep, p2_ref[h, :, lanes], 0.0) * p1t[h][r:r + 1, :]
            coef_ref[rows, lanes] = (gate * _gelu(act_ref[rows, lanes])).astype(BF16)


def _peer_kernel(hnt_ref, st_ref, u0_ref, uo_ref, ue_ref, vte_ref, vto_ref, ft_ref,
                 p1_ref, p2_ref, tau_ref, acta_ref, actb_ref, coefa_ref, coefb_ref, acc_ref):
    s = pl.program_id(1)
    tb = hnt_ref.shape[1]
    dot = functools.partial(jnp.dot, preferred_element_type=F32)

    @pl.when(s == 0)
    def _():
        _peer_select(st_ref, p1_ref, p2_ref, tau_ref, tb)
        acta_ref[...] = dot(u0_ref[...], hnt_ref[...])
        acc_ref[...] = jnp.zeros_like(acc_ref)

    actb_ref[...] = dot(uo_ref[...], hnt_ref[...])
    _peer_coefficients(2 * s, st_ref, p1_ref, p2_ref, tau_ref, acta_ref, coefa_ref, tb)
    acc_ref[...] += dot(vte_ref[...], coefa_ref[...])
    acta_ref[...] = dot(ue_ref[...], hnt_ref[...])
    _peer_coefficients(2 * s + 1, st_ref, p1_ref, p2_ref, tau_ref, actb_ref, coefb_ref, tb)
    acc_ref[...] += dot(vto_ref[...], coefb_ref[...])

    @pl.when(s == pl.num_programs(1) - 1)
    def _():
        ft_ref[...] = acc_ref[...]


def _peer(hnt, st, u_bf, vt_bf):
    t = hnt.shape[1]
    tb = min(t, 256)
    n_chunks = u_bf.shape[0] // E_CHUNK
    steps = n_chunks // 2
    ublk = lambda fn: pl.BlockSpec((E_CHUNK, D_MODEL), fn)
    vblk = lambda fn: pl.BlockSpec((D_MODEL, E_CHUNK), fn)
    return pl.pallas_call(
        _peer_kernel,
        grid=(t // tb, steps),
        in_specs=[pl.BlockSpec((D_MODEL, tb), lambda i, s: (0, i)),
                  pl.BlockSpec((2 * PEER_HEADS, N_KEYS, tb), lambda i, s: (0, 0, i)),
                  ublk(lambda i, s: (0, 0)),
                  ublk(lambda i, s: (2 * s + 1, 0)),
                  ublk(lambda i, s: (jnp.minimum(2 * s + 2, n_chunks - 1), 0)),
                  vblk(lambda i, s: (0, 2 * s)),
                  vblk(lambda i, s: (0, 2 * s + 1))],
        out_specs=pl.BlockSpec((D_MODEL, tb), lambda i, s: (0, i)),
        out_shape=jax.ShapeDtypeStruct((D_MODEL, t), F32),
        scratch_shapes=[pltpu.VMEM((PEER_HEADS, N_KEYS, tb), F32),
                        pltpu.VMEM((PEER_HEADS, N_KEYS, tb), F32),
                        pltpu.VMEM((PEER_HEADS, tb), F32),
                        pltpu.VMEM((E_CHUNK, tb), F32),
                        pltpu.VMEM((E_CHUNK, tb), F32),
                        pltpu.VMEM((E_CHUNK, tb), BF16),
                        pltpu.VMEM((E_CHUNK, tb), BF16),
                        pltpu.VMEM((D_MODEL, tb), F32)],
        compiler_params=_cparams("parallel", "arbitrary"),
        name="peer_dense",
    )(hnt, st, u_bf, u_bf, u_bf, vt_bf, vt_bf)


def _final_kernel(h_ref, ft_ref, g_ref, y_ref):
    hres = h_ref[...] + ft_ref[...].T
    y_ref[...] = hres * lax.rsqrt(jnp.mean(hres * hres, axis=-1, keepdims=True) + EPS) * g_ref[...]


def _final(h, ft, g):
    t = h.shape[0]
    tm = min(t, 512)
    return pl.pallas_call(
        _final_kernel,
        grid=(t // tm,),
        in_specs=[pl.BlockSpec((tm, D_MODEL), lambda i: (i, 0)),
                  pl.BlockSpec((D_MODEL, tm), lambda i: (0, i)),
                  pl.BlockSpec((1, D_MODEL), lambda i: (0, 0))],
        out_specs=pl.BlockSpec((tm, D_MODEL), lambda i: (i, 0)),
        out_shape=jax.ShapeDtypeStruct((t, D_MODEL), F32),
        compiler_params=_cparams("parallel"),
        name="final_norm",
    )(h, ft, g)


def _pack_weights(norm_mix, w_in, conv_w, a_log, dt_bias, gn_a, gn_b, w_br_a, w_br_b, w_out,
                  norm_ffn, peer_wq, peer_keys, peer_u, peer_v, norm_final):
    w = w_in[0]
    n_small = 2 * H_A
    c0 = CONV_CH + VA
    w_main = jnp.concatenate([w[:, :c0], w[:, c0 + n_small:]], axis=1).astype(BF16)
    w_ba = jnp.pad(w[:, c0:c0 + n_small], ((0, 0), (0, LANES - n_small)))
    wb_hi = w_ba.astype(BF16)
    wb_lo = (w_ba - wb_hi.astype(F32)).astype(BF16)
    pad_row = lambda v: jnp.pad(v.reshape(1, H_A), ((0, 0), (H_A, LANES - 2 * H_A)))
    logg = jnp.log1p(-(2.0 ** (-5.0 - jnp.arange(H_B, dtype=F32))))
    return dict(
        norm_mix=norm_mix[0].reshape(1, D_MODEL), w_main=w_main, wb_hi=wb_hi, wb_lo=wb_lo,
        conv_w=conv_w[0], alog_row=pad_row(a_log[0]), dtb_row=pad_row(dt_bias[0]),
        logg_tab=jnp.broadcast_to(logg[:, None, None], (H_B, 1, LANES)),
        gn_a=gn_a[0].reshape(1, DV_A), gn_b=gn_b[0].reshape(1, DV_B),
        w_a=w_br_a[0].astype(BF16), w_b=w_br_b[0].astype(BF16), w_o=w_out[0].astype(BF16),
        norm_ffn=norm_ffn[0].reshape(1, D_MODEL), w_q=peer_wq[0].astype(BF16),
        keys=peer_keys[0].reshape(2 * PEER_HEADS, N_KEYS, N_KEYS).astype(BF16),
        u_bf=peer_u[0].astype(BF16), vt_bf=peer_v[0].T.astype(BF16),
        norm_final=norm_final.reshape(1, D_MODEL))


def _channel_mix(oa, ob, proj, x2, p):
    h, hnt, st = _postmix(oa, ob, proj, x2, p["gn_a"], p["gn_b"], p["w_a"], p["w_b"], p["w_o"],
                          p["norm_ffn"], p["w_q"], p["keys"])
    ft = _peer(hnt, st, p["u_bf"], p["vt_bf"])
    return _final(h, ft, p["norm_final"])


def _prompt_group(x, p):
    b, n, _ = x.shape
    x2 = x.reshape(b * n, D_MODEL)
    proj, ba = _inproj(x2, p["norm_mix"], p["w_main"], p["wb_hi"], p["wb_lo"])
    proj3 = proj.reshape(b, n, N_MAIN)
    oa, s_delta = _delta_prompt(proj3, ba.reshape(b, n, LANES), p["conv_w"], p["alog_row"], p["dtb_row"])
    cos2, sin2 = _rope_tables(n, 0)
    ob, s_ret = _ret_prompt(proj3, cos2, sin2, p["logg_tab"])
    y = _channel_mix(oa.reshape(b * n, VA), ob.reshape(b * n, VB), proj, x2, p)
    conv_new = proj3[:, n - (CONV_W - 1):, :CONV_CH]
    return y.reshape(b, n, D_MODEL), conv_new[None], s_delta[None], s_ret[None]


def _sample_group(x, conv_buf, s_delta, s_ret, p):
    n = x.shape[0]
    x2 = x.reshape(n, D_MODEL)
    proj, ba = _inproj(x2, p["norm_mix"], p["w_main"], p["wb_hi"], p["wb_lo"])
    cos2, sin2 = _rope_tables(8, PAST_LEN)
    cb3 = jnp.transpose(conv_buf, (1, 0, 2))
    cnew, va, qat, kat, qbt, kbt, bg = _sample_prep(proj, ba, cb3, p["conv_w"], p["alog_row"],
                                                    p["dtb_row"], cos2, sin2)
    sd_new, sr_new, oa, ob = _sample_state(s_delta, s_ret, va, proj, bg, qat, kat, qbt, kbt,
                                           p["logg_tab"])
    y = _channel_mix(oa, ob, proj, x2, p)
    return (y.reshape(n, 1, D_MODEL), jnp.transpose(cnew, (1, 0, 2))[None], sd_new[None], sr_new[None])


def kernel(x_prompt, x_sample, state_conv_a, state_delta, state_ret, norm_mix, w_in, conv_w, a_log, dt_bias, gn_a, gn_b, w_br_a, w_br_b, w_out, norm_ffn, peer_wq, peer_keys, peer_u, peer_v, norm_final):
    assert w_in.shape[0] == 1 and x_sample.shape[1] == 1
    p = _pack_weights(norm_mix, w_in, conv_w, a_log, dt_bias, gn_a, gn_b, w_br_a, w_br_b, w_out,
                      norm_ffn, peer_wq, peer_keys, peer_u, peer_v, norm_final)
    y_p, conv_p, delta_p, ret_p = _prompt_group(x_prompt, p)
    y_s, conv_s, delta_s, ret_s = _sample_group(x_sample, state_conv_a[0], state_delta[0],
                                                state_ret[0], p)
    return (y_p, y_s, conv_p, delta_p, ret_p, conv_s, delta_s, ret_s)
```

```python
import functools
import math

import jax
import jax.numpy as jnp
from jax import lax
from jax.experimental import pallas as pl
from jax.experimental.pallas import tpu as pltpu

F32 = jnp.float32
BF16 = jnp.bfloat16

EPS = 1e-6
D_MODEL = 1024
H_A, DK_A, DV_A, CONV_W = 4, 128, 128, 4
H_B, DK_B, DV_B = 4, 128, 256
ROPE_BASE = 10000.0
PAST_LEN = 16384
N_KEYS = 128
PEER_HEADS = 8
PEER_TOPK = 16
QA, VA, QB, VB = H_A * DK_A, H_A * DV_A, H_B * DK_B, H_B * DV_B
CONV_CH = 2 * QA + VA

COL_GATE_A, COL_QB, COL_KB, COL_VB = 1536, 2048, 2560, 3072
COL_GATE_B, COL_SEL_A, COL_SEL_B, N_MAIN = 4096, 5120, 6144, 7168
LANES = 128
SUBLANES = 8
CHUNK = 128
VMEM_LIMIT = 56 * 1024 * 1024


def _cparams(*sem):
    return pltpu.CompilerParams(dimension_semantics=sem, vmem_limit_bytes=VMEM_LIMIT)


def _dot(a, b):
    return jnp.dot(a.astype(BF16), b.astype(BF16), preferred_element_type=F32)


def _dot_nt(a, b):
    return lax.dot_general(a.astype(BF16), b.astype(BF16), (((1,), (1,)), ((), ())),
                           preferred_element_type=F32)


def _split2(a):
    hi = a.astype(BF16)
    return hi, (a - hi.astype(F32)).astype(BF16)


def _dot3(a, b):
    ah, al = _split2(a)
    bh, bl = _split2(b)
    if a.ndim == 3:
        dims = (((2,), (1,)), ((0,), (0,)))
        d = lambda x, y: lax.dot_general(x, y, dims, preferred_element_type=F32)
    else:
        d = functools.partial(jnp.dot, preferred_element_type=F32)
    return d(ah, bh) + d(al, bh) + d(ah, bl)


def _sigmoid(x):
    return 1.0 / (1.0 + jnp.exp(-x))


def _silu(x):
    return x * _sigmoid(x)


def _softplus(x):
    return jnp.maximum(x, 0.0) + jnp.log1p(jnp.exp(-jnp.abs(x)))


def _gelu(x):
    return 0.5 * x * (1.0 + lax.erf(x * (1.0 / math.sqrt(2.0))))


def _rope_kernel(cos_ref, sin_ref, *, pos0):
    shape = cos_ref.shape
    half = shape[1] // 2
    lane = lax.broadcasted_iota(jnp.int32, shape, 1)
    row = lax.broadcasted_iota(jnp.int32, shape, 0)
    j = jnp.where(lane >= half, lane - half, lane).astype(F32)
    inv = jnp.exp(j * (-math.log(ROPE_BASE) / half))
    ang = (row + pos0).astype(F32) * inv
    s = jnp.sin(ang)
    cos_ref[...] = jnp.cos(ang)
    sin_ref[...] = jnp.where(lane >= half, s, -s)


def _rope_tables(rows, pos0):
    sds = jax.ShapeDtypeStruct((rows, DK_B), F32)
    return pl.pallas_call(functools.partial(_rope_kernel, pos0=pos0), out_shape=(sds, sds),
                          name="rope_tables")()


def _rope(x, cos2, sin2):
    return x * cos2 + pltpu.roll(x, DK_B // 2, axis=1) * sin2


def _inproj_kernel(x_ref, g_ref, w_ref, wbh_ref, wbl_ref, proj_ref, ba_ref, xn_ref):
    @pl.when(pl.program_id(1) == 0)
    def _():
        x = x_ref[...]
        y = x * lax.rsqrt(jnp.mean(x * x, axis=-1, keepdims=True) + EPS) * g_ref[...]
        yh, yl = _split2(y)
        xn_ref[...] = yh
        d = functools.partial(jnp.dot, preferred_element_type=F32)
        ba_ref[...] = d(yh, wbh_ref[...]) + d(yl, wbh_ref[...]) + d(yh, wbl_ref[...])

    proj_ref[...] = jnp.dot(xn_ref[...], w_ref[...], preferred_element_type=F32)


def _inproj(x2, g, w_main, wb_hi, wb_lo):
    t = x2.shape[0]
    tm = min(t, 1024)
    tn = 1024
    return pl.pallas_call(
        _inproj_kernel,
        grid=(t // tm, N_MAIN // tn),
        in_specs=[pl.BlockSpec((tm, D_MODEL), lambda i, j: (i, 0)),
                  pl.BlockSpec((1, D_MODEL), lambda i, j: (0, 0)),
                  pl.BlockSpec((D_MODEL, tn), lambda i, j: (0, j)),
                  pl.BlockSpec((D_MODEL, LANES), lambda i, j: (0, 0)),
                  pl.BlockSpec((D_MODEL, LANES), lambda i, j: (0, 0))],
        out_specs=(pl.BlockSpec((tm, tn), lambda i, j: (i, j)),
                   pl.BlockSpec((tm, LANES), lambda i, j: (i, 0))),
        out_shape=(jax.ShapeDtypeStruct((t, N_MAIN), F32), jax.ShapeDtypeStruct((t, LANES), F32)),
        scratch_shapes=[pltpu.VMEM((tm, D_MODEL), BF16)],
        compiler_params=_cparams("parallel", "arbitrary"),
        name="rms_inproj",
    )(x2, g, w_main, wb_hi, wb_lo)


def _lane_pick(x, idx):
    lane = lax.broadcasted_iota(jnp.int32, x.shape, 1)
    return jnp.sum(jnp.where(lane == idx, x, 0.0), axis=1, keepdims=True)


def _decay_terms(ba, alog_row, dtb_row):
    beta = _sigmoid(ba)
    logd = -jnp.exp(alog_row) * _softplus(ba + dtb_row)
    return beta, logd


def _unit_lower_inverse(lm, masks):
    n = lm.shape[-1]
    row = lax.broadcasted_iota(jnp.int32, (n, n), 0)
    col = lax.broadcasted_iota(jnp.int32, (n, n), 1)
    t = jnp.where(row == col, 1.0, 0.0) - lm * masks[0]
    for m in masks[1:]:
        t = t - _dot3(_dot3(t, lm * m), t)
    return t


def _doubling_masks(n):
    row = lax.broadcasted_iota(jnp.int32, (n, n), 0)
    col = lax.broadcasted_iota(jnp.int32, (n, n), 1)
    masks = []
    lvl = 0
    while (1 << lvl) < n:
        same = (row >> (lvl + 1)) == (col >> (lvl + 1))
        lower = ((row >> lvl) & 1) == 1
        left = ((col >> lvl) & 1) == 0
        masks.append(jnp.where(same & lower & left, 1.0, 0.0))
        lvl += 1
    return masks


def _delta_kernel(q_ref, k_ref, v_ref, ba_ref, wq_ref, wk_ref, wv_ref, alog_ref, dtb_ref,
                  o_ref, s_ref, pad_ref, qs_ref, ks_ref, vs_ref, bt_ref, ld_ref,
                  u_ref, w_ref, qk_ref, qg_ref, kd_ref, gl_ref):
    h = pl.program_id(1)
    n = q_ref.shape[0]
    c = CHUNK

    def conv_silu(x_ref, w_ref):
        pad_ref[0:8, :] = jnp.zeros((8, LANES), F32)
        pad_ref[8:8 + n, :] = x_ref[...]
        base = 8 - (CONV_W - 1)
        acc = pad_ref[base:base + n, :] * w_ref[0:1, :]
        for i in range(1, CONV_W):
            acc = acc + pad_ref[base + i:base + i + n, :] * w_ref[i:i + 1, :]
        return _silu(acc)

    def l2n(x):
        return x * lax.rsqrt(jnp.sum(x * x, axis=-1, keepdims=True) + EPS)

    qs_ref[...] = l2n(conv_silu(q_ref, wq_ref)) * (DK_A ** -0.5)
    ks_ref[...] = l2n(conv_silu(k_ref, wk_ref))
    vs_ref[...] = conv_silu(v_ref, wv_ref)
    beta, logd = _decay_terms(ba_ref[...], alog_ref[...], dtb_ref[...])
    bt_ref[...] = beta
    ld_ref[...] = logd

    row = lax.broadcasted_iota(jnp.int32, (c, c), 0)
    col = lax.broadcasted_iota(jnp.int32, (c, c), 1)
    tril = row >= col
    strict = row > col
    tril_f = jnp.where(tril, 1.0, 0.0).astype(BF16)
    masks = _doubling_masks(c)

    def chunk_terms(ci):
        rows = pl.ds(pl.multiple_of(ci * c, c), c)
        qc, kc, vc = qs_ref[rows, :], ks_ref[rows, :], vs_ref[rows, :]
        bcol = _lane_pick(bt_ref[rows, :], h)
        ldc = ld_ref[rows, :]
        l1 = ldc.astype(BF16)
        r1 = ldc - l1.astype(F32)
        l2 = r1.astype(BF16)
        l3 = (r1 - l2.astype(F32)).astype(BF16)
        d = functools.partial(jnp.dot, preferred_element_type=F32)
        gcol = _lane_pick(d(tril_f, l1) + d(tril_f, l2) + d(tril_f, l3), H_A + h)
        gmat = jnp.broadcast_to(gcol, (c, c))
        diff = gmat - gmat.T
        decay = jnp.where(tril, jnp.exp(jnp.where(tril, diff, 0.0)), 0.0)
        kb = kc * bcol
        lm = jnp.where(strict, _dot_nt(kb, kc) * decay, 0.0)
        eg = jnp.exp(gcol)
        glast = gcol[c - 1:c, :]
        return (lm, vc * bcol, kb * eg, _dot_nt(qc, kc) * decay, qc * eg,
                kc * jnp.exp(glast - gcol), jnp.broadcast_to(jnp.exp(glast), (SUBLANES, LANES)))

    n_chunks = n // c
    group = math.gcd(n_chunks, 8)

    def group_body(gi, carry):
        terms = [chunk_terms(gi * group + j) for j in range(group)]
        t = _unit_lower_inverse(jnp.stack([tm[0] for tm in terms]), masks)
        for j, (_, vb, kbg, qk, qg, kd, gl) in enumerate(terms):
            ci = gi * group + j
            rows = pl.ds(pl.multiple_of(ci * c, c), c)
            u_ref[rows, :] = _dot(t[j], vb)
            w_ref[rows, :] = _dot(t[j], kbg)
            qk_ref[rows, :] = qk
            qg_ref[rows, :] = qg
            kd_ref[rows, :] = kd
            gl_ref[pl.ds(pl.multiple_of(ci * SUBLANES, SUBLANES), SUBLANES), :] = gl
        return carry

    lax.fori_loop(0, n_chunks // group, group_body, 0)

    def state_body(ci, s):
        rows = pl.ds(pl.multiple_of(ci * c, c), c)
        v_new = u_ref[rows, :] - _dot(w_ref[rows, :], s)
        o_ref[rows, :] = _dot(qg_ref[rows, :], s) + _dot(qk_ref[rows, :], v_new)
        g_last = gl_ref[pl.ds(pl.multiple_of(ci * SUBLANES, SUBLANES), SUBLANES), :][0:1, 0:1]
        return s * g_last + _dot(kd_ref[rows, :].T, v_new)

    s_ref[...] = lax.fori_loop(0, n_chunks, state_body, jnp.zeros((DK_A, DV_A), F32))


def _delta_prompt(proj3, ba3, conv_w, alog_row, dtb_row):
    b, n, _ = proj3.shape
    hb = DK_A // LANES

    def col(off):
        return lambda i, h: (i, 0, off // LANES + h * hb)

    def wcol(off):
        return lambda i, h: (0, off // LANES + h * hb)

    seq = lambda off: pl.BlockSpec((None, n, LANES), col(off))
    wsp = lambda off: pl.BlockSpec((CONV_W, LANES), wcol(off))
    row = pl.BlockSpec((1, LANES), lambda i, h: (0, 0))
    return pl.pallas_call(
        _delta_kernel,
        grid=(b, H_A),
        in_specs=[seq(0), seq(QA), seq(2 * QA),
                  pl.BlockSpec((None, n, LANES), lambda i, h: (i, 0, 0)),
                  wsp(0), wsp(QA), wsp(2 * QA), row, row],
        out_specs=(pl.BlockSpec((None, n, DV_A), lambda i, h: (i, 0, h)),
                   pl.BlockSpec((None, None, DK_A, DV_A), lambda i, h: (i, h, 0, 0))),
        out_shape=(jax.ShapeDtypeStruct((b, n, VA), F32),
                   jax.ShapeDtypeStruct((b, H_A, DK_A, DV_A), F32)),
        scratch_shapes=[pltpu.VMEM((n + 8, LANES), F32)] + [pltpu.VMEM((n, LANES), F32)] * 10
                       + [pltpu.VMEM((n // CHUNK * SUBLANES, LANES), F32)],
        compiler_params=_cparams("parallel", "parallel"),
        name="delta_prompt",
    )(proj3, proj3, proj3, ba3, conv_w, conv_w, conv_w, alog_row, dtb_row)


def _ret_kernel(q_ref, k_ref, v_ref, cos_ref, sin_ref, logg_ref, o_ref, s_ref, qs_ref, ks_ref):
    n = q_ref.shape[0]
    c = CHUNK
    cos2, sin2 = cos_ref[...], sin_ref[...]
    qs_ref[...] = _rope(q_ref[...], cos2, sin2)
    ks_ref[...] = _rope(k_ref[...], cos2, sin2) * (DK_B ** -0.5)

    logg = logg_ref[...]
    row = lax.broadcasted_iota(jnp.int32, (c, c), 0)
    col = lax.broadcasted_iota(jnp.int32, (c, c), 1)
    tril = row >= col
    dmat = jnp.where(tril, jnp.exp(jnp.where(tril, (row - col).astype(F32) * logg, 0.0)), 0.0)
    idx = lax.broadcasted_iota(jnp.int32, (c, LANES), 0).astype(F32)
    q_dec = jnp.exp((idx + 1.0) * logg)
    k_dec = jnp.exp((c - 1.0 - idx) * logg)
    g_c = jnp.exp(c * logg[:, 0:1])

    n_chunks = n // c
    group = math.gcd(n_chunks, 4)

    def body(gi, s):
        for j in range(group):
            rows = pl.ds(pl.multiple_of((gi * group + j) * c, c), c)
            qc, kc, vc = qs_ref[rows, :], ks_ref[rows, :], v_ref[rows, :]
            inner = _dot(_dot_nt(qc, kc) * dmat, vc)
            o_ref[rows, :] = inner + _dot(qc * q_dec, s)
            s = s * g_c + _dot((kc * k_dec).T, vc)
        return s

    s_ref[...] = lax.fori_loop(0, n_chunks // group, body, jnp.zeros((DK_B, DV_B), F32))


def _ret_prompt(proj3, cos2, sin2, logg_tab):
    b, n, _ = proj3.shape
    return pl.pallas_call(
        _ret_kernel,
        grid=(b, H_B),
        in_specs=[pl.BlockSpec((None, n, DK_B), lambda i, h: (i, 0, COL_QB // DK_B + h)),
                  pl.BlockSpec((None, n, DK_B), lambda i, h: (i, 0, COL_KB // DK_B + h)),
                  pl.BlockSpec((None, n, DV_B), lambda i, h: (i, 0, COL_VB // DV_B + h)),
                  pl.BlockSpec((n, DK_B), lambda i, h: (0, 0)),
                  pl.BlockSpec((n, DK_B), lambda i, h: (0, 0)),
                  pl.BlockSpec((None, 1, LANES), lambda i, h: (h, 0, 0))],
        out_specs=(pl.BlockSpec((None, n, DV_B), lambda i, h: (i, 0, h)),
                   pl.BlockSpec((None, None, DK_B, DV_B), lambda i, h: (i, h, 0, 0))),
        out_shape=(jax.ShapeDtypeStruct((b, n, VB), F32),
                   jax.ShapeDtypeStruct((b, H_B, DK_B, DV_B), F32)),
        scratch_shapes=[pltpu.VMEM((n, DK_B), F32)] * 2,
        compiler_params=_cparams("parallel", "parallel"),
        name="retention_prompt",
    )(proj3, proj3, proj3, cos2, sin2, logg_tab)


def _sample_prep_kernel(proj_ref, ba_ref, cb_ref, cw_ref, alog_ref, dtb_ref, cos_ref, sin_ref,
                        cnew_ref, va_ref, qat_ref, kat_ref, qbt_ref, kbt_ref, bg_ref):
    x = proj_ref[:, 0:CONV_CH]
    acc = cb_ref[0] * cw_ref[0:1, :]
    for i in range(1, CONV_W - 1):
        acc = acc + cb_ref[i] * cw_ref[i:i + 1, :]
    acc = acc + x * cw_ref[CONV_W - 1:CONV_W, :]
    qkv = _silu(acc)
    for i in range(CONV_W - 2):
        cnew_ref[i] = cb_ref[i + 1]
    cnew_ref[CONV_W - 2] = x

    def l2n(v):
        return v * lax.rsqrt(jnp.sum(v * v, axis=-1, keepdims=True) + EPS)

    cos2, sin2 = cos_ref[0:1, :], sin_ref[0:1, :]
    for h in range(H_A):
        hs = slice(h * DK_A, (h + 1) * DK_A)
        qat_ref[hs, :] = (l2n(qkv[:, h * DK_A:(h + 1) * DK_A]) * (DK_A ** -0.5)).T
        kat_ref[hs, :] = l2n(qkv[:, QA + h * DK_A:QA + (h + 1) * DK_A]).T
    va_ref[...] = qkv[:, 2 * QA:]
    for h in range(H_B):
        hs = slice(h * DK_B, (h + 1) * DK_B)
        qbt_ref[hs, :] = _rope(proj_ref[:, COL_QB + h * DK_B:COL_QB + (h + 1) * DK_B], cos2, sin2).T
        kbt_ref[hs, :] = (_rope(proj_ref[:, COL_KB + h * DK_B:COL_KB + (h + 1) * DK_B], cos2, sin2)
                          * (DK_B ** -0.5)).T
    beta, logd = _decay_terms(ba_ref[...], alog_ref[...], dtb_ref[...])
    lane = lax.broadcasted_iota(jnp.int32, beta.shape, 1)
    bg_ref[...] = jnp.where(lane < H_A, beta, jnp.exp(logd))


def _sample_prep(proj, ba, cb3, conv_w, alog_row, dtb_row, cos2, sin2):
    n = proj.shape[0]
    sd = lambda *s: jax.ShapeDtypeStruct(s, F32)
    return pl.pallas_call(
        _sample_prep_kernel,
        out_shape=(sd(CONV_W - 1, n, CONV_CH), sd(n, VA), sd(QA, n), sd(QA, n), sd(QB, n), sd(QB, n),
                   sd(n, LANES)),
        compiler_params=pltpu.CompilerParams(vmem_limit_bytes=VMEM_LIMIT),
        name="sample_prep",
    )(proj, ba, cb3, conv_w, alog_row, dtb_row, cos2, sin2)


SAMPLE_ROWS = 8


def _sample_state_kernel(sd_ref, sr_ref, va_ref, vb_ref, bg_ref, qat_ref, kat_ref, qbt_ref, kbt_ref,
                         logg_ref, sdn_ref, srn_ref, oa_ref, ob_ref):
    base = pl.program_id(0) * SAMPLE_ROWS
    nseq = qat_ref.shape[1]
    lane = lax.broadcasted_iota(jnp.int32, (DK_A, nseq), 1)

    def column(t_ref, h, seq):
        blk = t_ref[h * DK_A:(h + 1) * DK_A, :]
        return jnp.sum(jnp.where(lane == seq, blk, 0.0), axis=1, keepdims=True)

    for j in range(SAMPLE_ROWS):
        seq = base + j
        bg_row = bg_ref[j:j + 1, :]
        for h in range(H_A):
            kcol, qcol = column(kat_ref, h, seq), column(qat_ref, h, seq)
            beta = _lane_pick(bg_row, h)
            eg = _lane_pick(bg_row, H_A + h)
            s0 = sd_ref[j, h]
            v = va_ref[j:j + 1, h * DV_A:(h + 1) * DV_A]
            ks = jnp.sum(kcol * s0, axis=0, keepdims=True)
            v_new = beta * v - (beta * eg) * ks
            s1 = s0 * eg + kcol * v_new
            sdn_ref[j, h] = s1
            oa_ref[j:j + 1, h * DV_A:(h + 1) * DV_A] = jnp.sum(qcol * s1, axis=0, keepdims=True)
        for h in range(H_B):
            kcol, qcol = column(kbt_ref, h, seq), column(qbt_ref, h, seq)
            gamma = jnp.exp(logg_ref[h][:, 0:1])
            v = vb_ref[j:j + 1, h * DV_B:(h + 1) * DV_B]
            s1 = sr_ref[j, h] * gamma + kcol * v
            srn_ref[j, h] = s1
            ob_ref[j:j + 1, h * DV_B:(h + 1) * DV_B] = jnp.sum(qcol * s1, axis=0, keepdims=True)


def _sample_state(sd, sr, va, proj, bg, qat, kat, qbt, kbt, logg_tab):
    n = sd.shape[0]
    r = SAMPLE_ROWS
    full = lambda a: pl.BlockSpec(a.shape, lambda i: (0,) * a.ndim)
    return pl.pallas_call(
        _sample_state_kernel,
        grid=(n // r,),
        in_specs=[pl.BlockSpec((r, H_A, DK_A, DV_A), lambda i: (i, 0, 0, 0)),
                  pl.BlockSpec((r, H_B, DK_B, DV_B), lambda i: (i, 0, 0, 0)),
                  pl.BlockSpec((r, VA), lambda i: (i, 0)),
                  pl.BlockSpec((r, VB), lambda i: (i, COL_VB // VB)),
                  pl.BlockSpec((r, LANES), lambda i: (i, 0)),
                  full(qat), full(kat), full(qbt), full(kbt), full(logg_tab)],
        out_specs=(pl.BlockSpec((r, H_A, DK_A, DV_A), lambda i: (i, 0, 0, 0)),
                   pl.BlockSpec((r, H_B, DK_B, DV_B), lambda i: (i, 0, 0, 0)),
                   pl.BlockSpec((r, VA), lambda i: (i, 0)),
                   pl.BlockSpec((r, VB), lambda i: (i, 0))),
        out_shape=(jax.ShapeDtypeStruct(sd.shape, F32), jax.ShapeDtypeStruct(sr.shape, F32),
                   jax.ShapeDtypeStruct((n, VA), F32), jax.ShapeDtypeStruct((n, VB), F32)),
        compiler_params=_cparams("parallel"),
        name="sample_state",
    )(sd, sr, va, proj, bg, qat, kat, qbt, kbt, logg_tab)


def _postmix_kernel(oa_ref, ob_ref, ga_ref, gb_ref, sa_ref, sb_ref, x_ref, gna_ref, gnb_ref,
                    wa_ref, wb_ref, wo_ref, nf_ref, wq_ref, keys_ref, h_ref, hnt_ref, st_ref):
    def gated(o_ref, g_ref, gn_ref, heads, dv):
        parts = []
        for h in range(heads):
            o = o_ref[:, h * dv:(h + 1) * dv]
            y = o * lax.rsqrt(jnp.mean(o * o, axis=-1, keepdims=True) + EPS) * gn_ref[...]
            parts.append(y * _silu(g_ref[:, h * dv:(h + 1) * dv]))
        return jnp.concatenate(parts, axis=1)

    br_a = _dot(gated(oa_ref, ga_ref, gna_ref, H_A, DV_A), wa_ref[...])
    br_b = _dot(gated(ob_ref, gb_ref, gnb_ref, H_B, DV_B), wb_ref[...])
    merged = _sigmoid(sa_ref[...]) * br_a + _sigmoid(sb_ref[...]) * br_b
    hres = x_ref[...] + _dot(merged, wo_ref[...])
    h_ref[...] = hres
    hn = hres * lax.rsqrt(jnp.mean(hres * hres, axis=-1, keepdims=True) + EPS) * nf_ref[...]
    hnt_ref[...] = hn.T.astype(BF16)
    q = _dot(hn, wq_ref[...])
    for hp in range(2 * PEER_HEADS):
        st_ref[hp] = _dot_nt(keys_ref[hp], q[:, hp * N_KEYS:(hp + 1) * N_KEYS])


def _postmix(oa, ob, proj, x2, gn_a, gn_b, w_a, w_b, w_o, norm_ffn, w_q, keys):
    t = x2.shape[0]
    tm = min(t, 256)
    full = lambda a: pl.BlockSpec(a.shape, lambda i: (0,) * a.ndim)
    pcol = lambda width, off: pl.BlockSpec((tm, width), lambda i: (i, off // width))
    return pl.pallas_call(
        _postmix_kernel,
        grid=(t // tm,),
        in_specs=[pl.BlockSpec((tm, VA), lambda i: (i, 0)), pl.BlockSpec((tm, VB), lambda i: (i, 0)),
                  pcol(VA, COL_GATE_A), pcol(VB, COL_GATE_B), pcol(D_MODEL, COL_SEL_A),
                  pcol(D_MODEL, COL_SEL_B), pl.BlockSpec((tm, D_MODEL), lambda i: (i, 0)),
                  full(gn_a), full(gn_b), full(w_a), full(w_b), full(w_o), full(norm_ffn), full(w_q),
                  full(keys)],
        out_specs=(pl.BlockSpec((tm, D_MODEL), lambda i: (i, 0)),
                   pl.BlockSpec((D_MODEL, tm), lambda i: (0, i)),
                   pl.BlockSpec((2 * PEER_HEADS, N_KEYS, tm), lambda i: (0, 0, i))),
        out_shape=(jax.ShapeDtypeStruct((t, D_MODEL), F32), jax.ShapeDtypeStruct((D_MODEL, t), BF16),
                   jax.ShapeDtypeStruct((2 * PEER_HEADS, N_KEYS, t), F32)),
        compiler_params=_cparams("parallel"),
        name="postmix",
    )(oa, ob, proj, proj, proj, proj, x2, gn_a, gn_b, w_a, w_b, w_o, norm_ffn, w_q, keys)


A_PER_CHUNK = SUBLANES
E_CHUNK = A_PER_CHUNK * N_KEYS


def _batcher_pairs(n):
    pairs = []
    p = 1
    while p < n:
        k = p
        while k >= 1:
            for j in range(k % p, n - k, 2 * k):
                for i in range(min(k, n - j - k)):
                    if (i + j) // (2 * p) == (i + j + k) // (2 * p):
                        pairs.append((i + j, i + j + k))
            k //= 2
        p *= 2
    return pairs


_SORT_TOPK = _batcher_pairs(PEER_TOPK)


def _vmax(x, y):
    if x is None:
        return y
    return x if y is None else jnp.maximum(x, y)


def _exchange(x, y):
    if x is None or y is None:
        return _vmax(x, y), None
    return jnp.maximum(x, y), jnp.minimum(x, y)


def _sort_desc(v):
    v = list(v)
    for i, j in _SORT_TOPK:
        v[i], v[j] = _exchange(v[i], v[j])
    return v


def _merge_top(v, w):
    n = len(v)
    c = [_vmax(v[i], w[n - 1 - i]) for i in range(n)]
    d = n // 2
    while d >= 1:
        for i in range(n):
            if (i & d) == 0:
                c[i], c[i + d] = _exchange(c[i], c[i + d])
        d //= 2
    return c


def _top_scores(x):
    v = _sort_desc([x[i * SUBLANES:(i + 1) * SUBLANES, :] for i in range(N_KEYS // SUBLANES)])
    shift = SUBLANES // 2
    while shift >= 1:
        v = _merge_top(v, [pltpu.roll(t, shift, axis=0) for t in v])
        shift //= 2
    return v


def _peer_select(st_ref, p1_ref, p2_ref, tau_ref, tb):
    sub = lax.broadcasted_iota(jnp.int32, (SUBLANES, LANES), 0)
    for tg in range(tb // LANES):
        lanes = slice(tg * LANES, (tg + 1) * LANES)
        top = [[None] * PEER_TOPK for _ in range(2)]
        for h in range(PEER_HEADS):
            for p in range(2):
                v = _top_scores(st_ref[2 * h + p, :, lanes])
                for r in range(PEER_TOPK):
                    top[p][r] = v[r] if h == 0 else jnp.where(sub == h, v[r], top[p][r])
        best = None
        for r1 in range(PEER_TOPK):
            col = [top[0][r1] + top[1][r2] if (r1 + 1) * (r2 + 1) <= PEER_TOPK else None
                   for r2 in range(PEER_TOPK)]
            best = col if best is None else _merge_top(best, col)
        z = jnp.ones((SUBLANES, LANES), F32)
        for r in range(1, PEER_TOPK):
            z = z + jnp.exp(best[r] - best[0])
        tau_ref[:, lanes] = best[PEER_TOPK - 1]
        for h in range(PEER_HEADS):
            row = slice(h, h + 1)
            p1_ref[h, :, lanes] = jnp.exp(st_ref[2 * h, :, lanes] - top[0][0][row, :]) / z[row, :]
            p2_ref[h, :, lanes] = jnp.exp(st_ref[2 * h + 1, :, lanes] - top[1][0][row, :])


def _peer_coefficients(chunk, st_ref, p1_ref, p2_ref, tau_ref, act_ref, coef_ref, tb):
    a0 = pl.multiple_of(chunk * A_PER_CHUNK, A_PER_CHUNK)
    for tg in range(tb // LANES):
        lanes = slice(tg * LANES, (tg + 1) * LANES)
        s1t = [st_ref[2 * h, pl.ds(a0, A_PER_CHUNK), lanes] for h in range(PEER_HEADS)]
        p1t = [p1_ref[h, pl.ds(a0, A_PER_CHUNK), lanes] for h in range(PEER_HEADS)]
        for r in range(A_PER_CHUNK):
            rows = slice(r * N_KEYS, (r + 1) * N_KEYS)
            gate = jnp.zeros((N_KEYS, LANES), F32)
            for h in range(PEER_HEADS):
                keep = (s1t[h][r:r + 1, :] + st_ref[2 * h + 1, :, lanes]) >= tau_ref[h:h + 1, lanes]
                gate = gate + jnp.where(keep, p2_ref[h, :, lanes], 0.0) * p1t[h][r:r + 1, :]
            coef_ref[rows, lanes] = (gate * _gelu(act_ref[rows, lanes])).astype(BF16)


def _peer_kernel(hnt_ref, st_ref, u0_ref, uo_ref, ue_ref, vte_ref, vto_ref, ft_ref,
                 p1_ref, p2_ref, tau_ref, acta_ref, actb_ref, coefa_ref, coefb_ref, acc_ref):
    s = pl.program_id(1)
    tb = hnt_ref.shape[1]
    dot = functools.partial(jnp.dot, preferred_element_type=F32)

    @pl.when(s == 0)
    def _():
        _peer_select(st_ref, p1_ref, p2_ref, tau_ref, tb)
        acta_ref[...] = dot(u0_ref[...], hnt_ref[...])
        acc_ref[...] = jnp.zeros_like(acc_ref)

    actb_ref[...] = dot(uo_ref[...], hnt_ref[...])
    _peer_coefficients(2 * s, st_ref, p1_ref, p2_ref, tau_ref, acta_ref, coefa_ref, tb)
    acc_ref[...] += dot(vte_ref[...], coefa_ref[...])
    acta_ref[...] = dot(ue_ref[...], hnt_ref[...])
    _peer_coefficients(2 * s + 1, st_ref, p1_ref, p2_ref, tau_ref, actb_ref, coefb_ref, tb)
    acc_ref[...] += dot(vto_ref[...], coefb_ref[...])

    @pl.when(s == pl.num_programs(1) - 1)
    def _():
        ft_ref[...] = acc_ref[...]


def _peer(hnt, st, u_bf, vt_bf):
    t = hnt.shape[1]
    tb = min(t, 256)
    n_chunks = u_bf.shape[0] // E_CHUNK
    steps = n_chunks // 2
    ublk = lambda fn: pl.BlockSpec((E_CHUNK, D_MODEL), fn)
    vblk = lambda fn: pl.BlockSpec((None, D_MODEL, E_CHUNK), fn)
    return pl.pallas_call(
        _peer_kernel,
        grid=(t // tb, steps),
        in_specs=[pl.BlockSpec((D_MODEL, tb), lambda i, s: (0, i)),
                  pl.BlockSpec((2 * PEER_HEADS, N_KEYS, tb), lambda i, s: (0, 0, i)),
                  ublk(lambda i, s: (0, 0)),
                  ublk(lambda i, s: (2 * s + 1, 0)),
                  ublk(lambda i, s: (jnp.minimum(2 * s + 2, n_chunks - 1), 0)),
                  vblk(lambda i, s: (2 * s, 0, 0)),
                  vblk(lambda i, s: (2 * s + 1, 0, 0))],
        out_specs=pl.BlockSpec((D_MODEL, tb), lambda i, s: (0, i)),
        out_shape=jax.ShapeDtypeStruct((D_MODEL, t), F32),
        scratch_shapes=[pltpu.VMEM((PEER_HEADS, N_KEYS, tb), F32),
                        pltpu.VMEM((PEER_HEADS, N_KEYS, tb), F32),
                        pltpu.VMEM((PEER_HEADS, tb), F32),
                        pltpu.VMEM((E_CHUNK, tb), F32),
                        pltpu.VMEM((E_CHUNK, tb), F32),
                        pltpu.VMEM((E_CHUNK, tb), BF16),
                        pltpu.VMEM((E_CHUNK, tb), BF16),
                        pltpu.VMEM((D_MODEL, tb), F32)],
        compiler_params=_cparams("parallel", "arbitrary"),
        name="peer_dense",
    )(hnt, st, u_bf, u_bf, u_bf, vt_bf, vt_bf)


def _final_kernel(h_ref, ft_ref, g_ref, y_ref):
    hres = h_ref[...] + ft_ref[...].T
    y_ref[...] = hres * lax.rsqrt(jnp.mean(hres * hres, axis=-1, keepdims=True) + EPS) * g_ref[...]


def _final(h, ft, g):
    t = h.shape[0]
    tm = min(t, 512)
    return pl.pallas_call(
        _final_kernel,
        grid=(t // tm,),
        in_specs=[pl.BlockSpec((tm, D_MODEL), lambda i: (i, 0)),
                  pl.BlockSpec((D_MODEL, tm), lambda i: (0, i)),
                  pl.BlockSpec((1, D_MODEL), lambda i: (0, 0))],
        out_specs=pl.BlockSpec((tm, D_MODEL), lambda i: (i, 0)),
        out_shape=jax.ShapeDtypeStruct((t, D_MODEL), F32),
        compiler_params=_cparams("parallel"),
        name="final_norm",
    )(h, ft, g)


def _pack_weights(norm_mix, w_in, conv_w, a_log, dt_bias, gn_a, gn_b, w_br_a, w_br_b, w_out,
                  norm_ffn, peer_wq, peer_keys, peer_u, peer_v, norm_final):
    w = w_in[0]
    n_small = 2 * H_A
    c0 = CONV_CH + VA
    w_main = jnp.concatenate([w[:, :c0], w[:, c0 + n_small:]], axis=1).astype(BF16)
    w_ba = jnp.pad(w[:, c0:c0 + n_small], ((0, 0), (0, LANES - n_small)))
    wb_hi = w_ba.astype(BF16)
    wb_lo = (w_ba - wb_hi.astype(F32)).astype(BF16)
    pad_row = lambda v: jnp.pad(v.reshape(1, H_A), ((0, 0), (H_A, LANES - 2 * H_A)))
    logg = jnp.log1p(-(2.0 ** (-5.0 - jnp.arange(H_B, dtype=F32))))
    return dict(
        norm_mix=norm_mix[0].reshape(1, D_MODEL), w_main=w_main, wb_hi=wb_hi, wb_lo=wb_lo,
        conv_w=conv_w[0], alog_row=pad_row(a_log[0]), dtb_row=pad_row(dt_bias[0]),
        logg_tab=jnp.broadcast_to(logg[:, None, None], (H_B, 1, LANES)),
        gn_a=gn_a[0].reshape(1, DV_A), gn_b=gn_b[0].reshape(1, DV_B),
        w_a=w_br_a[0].astype(BF16), w_b=w_br_b[0].astype(BF16), w_o=w_out[0].astype(BF16),
        norm_ffn=norm_ffn[0].reshape(1, D_MODEL), w_q=peer_wq[0].astype(BF16),
        keys=peer_keys[0].reshape(2 * PEER_HEADS, N_KEYS, N_KEYS).astype(BF16),
        u_bf=peer_u[0].astype(BF16),
        vt_bf=jnp.transpose(peer_v[0].astype(BF16).reshape(-1, E_CHUNK, D_MODEL), (0, 2, 1)),
        norm_final=norm_final.reshape(1, D_MODEL))


def _channel_mix(oa, ob, proj, x2, p):
    h, hnt, st = _postmix(oa, ob, proj, x2, p["gn_a"], p["gn_b"], p["w_a"], p["w_b"], p["w_o"],
                          p["norm_ffn"], p["w_q"], p["keys"])
    ft = _peer(hnt, st, p["u_bf"], p["vt_bf"])
    return _final(h, ft, p["norm_final"])


def _prompt_group(x, p):
    b, n, _ = x.shape
    x2 = x.reshape(b * n, D_MODEL)
    proj, ba = _inproj(x2, p["norm_mix"], p["w_main"], p["wb_hi"], p["wb_lo"])
    proj3 = proj.reshape(b, n, N_MAIN)
    oa, s_delta = _delta_prompt(proj3, ba.reshape(b, n, LANES), p["conv_w"], p["alog_row"], p["dtb_row"])
    cos2, sin2 = _rope_tables(n, 0)
    ob, s_ret = _ret_prompt(proj3, cos2, sin2, p["logg_tab"])
    y = _channel_mix(oa.reshape(b * n, VA), ob.reshape(b * n, VB), proj, x2, p)
    conv_new = proj3[:, n - (CONV_W - 1):, :CONV_CH]
    return y.reshape(b, n, D_MODEL), conv_new[None], s_delta[None], s_ret[None]


def _sample_group(x, conv_buf, s_delta, s_ret, p):
    n = x.shape[0]
    x2 = x.reshape(n, D_MODEL)
    proj, ba = _inproj(x2, p["norm_mix"], p["w_main"], p["wb_hi"], p["wb_lo"])
    cos2, sin2 = _rope_tables(8, PAST_LEN)
    cb3 = jnp.transpose(conv_buf, (1, 0, 2))
    cnew, va, qat, kat, qbt, kbt, bg = _sample_prep(proj, ba, cb3, p["conv_w"], p["alog_row"],
                                                    p["dtb_row"], cos2, sin2)
    sd_new, sr_new, oa, ob = _sample_state(s_delta, s_ret, va, proj, bg, qat, kat, qbt, kbt,
                                           p["logg_tab"])
    y = _channel_mix(oa, ob, proj, x2, p)
    return (y.reshape(n, 1, D_MODEL), jnp.transpose(cnew, (1, 0, 2))[None], sd_new[None], sr_new[None])


def kernel(x_prompt, x_sample, state_conv_a, state_delta, state_ret, norm_mix, w_in, conv_w, a_log, dt_bias, gn_a, gn_b, w_br_a, w_br_b, w_out, norm_ffn, peer_wq, peer_keys, peer_u, peer_v, norm_final):
    assert w_in.shape[0] == 1 and x_sample.shape[1] == 1
    p = _pack_weights(norm_mix, w_in, conv_w, a_log, dt_bias, gn_a, gn_b, w_br_a, w_br_b, w_out,
                      norm_ffn, peer_wq, peer_keys, peer_u, peer_v, norm_final)
    y_p, conv_p, delta_p, ret_p = _prompt_group(x_prompt, p)
    y_s, conv_s, delta_s, ret_s = _sample_group(x_sample, state_conv_a[0], state_delta[0],
                                                state_ret[0], p)
    return (y_p, y_s, conv_p, delta_p, ret_p, conv_s, delta_s, ret_s)
```

```python
import functools
import math

import jax
import jax.numpy as jnp
from jax import lax
from jax.experimental import pallas as pl
from jax.experimental.pallas import tpu as pltpu

F32 = jnp.float32
BF16 = jnp.bfloat16

EPS = 1e-6
D_MODEL = 1024
H_A, DK_A, DV_A, CONV_W = 4, 128, 128, 4
H_B, DK_B, DV_B = 4, 128, 256
ROPE_BASE = 10000.0
PAST_LEN = 16384
N_KEYS = 128
PEER_HEADS = 8
PEER_TOPK = 16
QA, VA, QB, VB = H_A * DK_A, H_A * DV_A, H_B * DK_B, H_B * DV_B
CONV_CH = 2 * QA + VA

COL_GATE_A, COL_QB, COL_KB, COL_VB = 1536, 2048, 2560, 3072
COL_GATE_B, COL_SEL_A, COL_SEL_B, N_MAIN = 4096, 5120, 6144, 7168
LANES = 128
SUBLANES = 8
CHUNK = 128
VMEM_LIMIT = 56 * 1024 * 1024


def _cparams(*sem):
    return pltpu.CompilerParams(dimension_semantics=sem, vmem_limit_bytes=VMEM_LIMIT)


def _dot(a, b):
    return jnp.dot(a.astype(BF16), b.astype(BF16), preferred_element_type=F32)


def _dot_nt(a, b):
    return lax.dot_general(a.astype(BF16), b.astype(BF16), (((1,), (1,)), ((), ())),
                           preferred_element_type=F32)


def _split2(a):
    hi = a.astype(BF16)
    return hi, (a - hi.astype(F32)).astype(BF16)


def _dot3(a, b):
    ah, al = _split2(a)
    bh, bl = _split2(b)
    if a.ndim == 3:
        dims = (((2,), (1,)), ((0,), (0,)))
        d = lambda x, y: lax.dot_general(x, y, dims, preferred_element_type=F32)
    else:
        d = functools.partial(jnp.dot, preferred_element_type=F32)
    return d(ah, bh) + d(al, bh) + d(ah, bl)


def _sigmoid(x):
    return 1.0 / (1.0 + jnp.exp(-x))


def _silu(x):
    return x * _sigmoid(x)


def _softplus(x):
    return jnp.maximum(x, 0.0) + jnp.log1p(jnp.exp(-jnp.abs(x)))


def _gelu_times2(x):
    return x * (1.0 + lax.erf(x * (1.0 / math.sqrt(2.0))))


def _rope_kernel(cos_ref, sin_ref, *, pos0):
    shape = cos_ref.shape
    half = shape[1] // 2
    lane = lax.broadcasted_iota(jnp.int32, shape, 1)
    row = lax.broadcasted_iota(jnp.int32, shape, 0)
    j = jnp.where(lane >= half, lane - half, lane).astype(F32)
    inv = jnp.exp(j * (-math.log(ROPE_BASE) / half))
    ang = (row + pos0).astype(F32) * inv
    s = jnp.sin(ang)
    cos_ref[...] = jnp.cos(ang)
    sin_ref[...] = jnp.where(lane >= half, s, -s)


def _rope_tables(rows, pos0):
    sds = jax.ShapeDtypeStruct((rows, DK_B), F32)
    return pl.pallas_call(functools.partial(_rope_kernel, pos0=pos0), out_shape=(sds, sds),
                          name="rope_tables")()


def _rope(x, cos2, sin2):
    return x * cos2 + pltpu.roll(x, DK_B // 2, axis=1) * sin2


def _inproj_kernel(x_ref, g_ref, w_ref, wbh_ref, wbl_ref, proj_ref, ba_ref, xn_ref):
    @pl.when(pl.program_id(1) == 0)
    def _():
        x = x_ref[...]
        y = x * lax.rsqrt(jnp.mean(x * x, axis=-1, keepdims=True) + EPS) * g_ref[...]
        yh, yl = _split2(y)
        xn_ref[...] = yh
        d = functools.partial(jnp.dot, preferred_element_type=F32)
        ba_ref[...] = d(yh, wbh_ref[...]) + d(yl, wbh_ref[...]) + d(yh, wbl_ref[...])

    proj_ref[...] = jnp.dot(xn_ref[...], w_ref[...], preferred_element_type=F32)


def _inproj(x2, g, w_main, wb_hi, wb_lo):
    t = x2.shape[0]
    tm = min(t, 1024)
    tn = 1024
    return pl.pallas_call(
        _inproj_kernel,
        grid=(t // tm, N_MAIN // tn),
        in_specs=[pl.BlockSpec((tm, D_MODEL), lambda i, j: (i, 0)),
                  pl.BlockSpec((1, D_MODEL), lambda i, j: (0, 0)),
                  pl.BlockSpec((D_MODEL, tn), lambda i, j: (0, j)),
                  pl.BlockSpec((D_MODEL, LANES), lambda i, j: (0, 0)),
                  pl.BlockSpec((D_MODEL, LANES), lambda i, j: (0, 0))],
        out_specs=(pl.BlockSpec((tm, tn), lambda i, j: (i, j)),
                   pl.BlockSpec((tm, LANES), lambda i, j: (i, 0))),
        out_shape=(jax.ShapeDtypeStruct((t, N_MAIN), F32), jax.ShapeDtypeStruct((t, LANES), F32)),
        scratch_shapes=[pltpu.VMEM((tm, D_MODEL), BF16)],
        compiler_params=_cparams("parallel", "arbitrary"),
        name="rms_inproj",
    )(x2, g, w_main, wb_hi, wb_lo)


def _lane_pick(x, idx):
    lane = lax.broadcasted_iota(jnp.int32, x.shape, 1)
    return jnp.sum(jnp.where(lane == idx, x, 0.0), axis=1, keepdims=True)


def _decay_terms(ba, alog_row, dtb_row):
    beta = _sigmoid(ba)
    logd = -jnp.exp(alog_row) * _softplus(ba + dtb_row)
    return beta, logd


def _unit_lower_inverse(lm, masks):
    n = lm.shape[-1]
    row = lax.broadcasted_iota(jnp.int32, (n, n), 0)
    col = lax.broadcasted_iota(jnp.int32, (n, n), 1)
    t = jnp.where(row == col, 1.0, 0.0) - lm * masks[0]
    for m in masks[1:]:
        t = t - _dot3(_dot3(t, lm * m), t)
    return t


def _doubling_masks(n):
    row = lax.broadcasted_iota(jnp.int32, (n, n), 0)
    col = lax.broadcasted_iota(jnp.int32, (n, n), 1)
    masks = []
    lvl = 0
    while (1 << lvl) < n:
        same = (row >> (lvl + 1)) == (col >> (lvl + 1))
        lower = ((row >> lvl) & 1) == 1
        left = ((col >> lvl) & 1) == 0
        masks.append(jnp.where(same & lower & left, 1.0, 0.0))
        lvl += 1
    return masks


def _delta_kernel(q_ref, k_ref, v_ref, ba_ref, wq_ref, wk_ref, wv_ref, alog_ref, dtb_ref,
                  o_ref, s_ref, pad_ref, qs_ref, ks_ref, vs_ref, bt_ref, ld_ref,
                  u_ref, w_ref, qk_ref, qg_ref, kd_ref, gl_ref):
    h = pl.program_id(1)
    n = q_ref.shape[0]
    c = CHUNK

    def conv_silu(x_ref, w_ref):
        pad_ref[0:8, :] = jnp.zeros((8, LANES), F32)
        pad_ref[8:8 + n, :] = x_ref[...]
        base = 8 - (CONV_W - 1)
        acc = pad_ref[base:base + n, :] * w_ref[0:1, :]
        for i in range(1, CONV_W):
            acc = acc + pad_ref[base + i:base + i + n, :] * w_ref[i:i + 1, :]
        return _silu(acc)

    def l2n(x):
        return x * lax.rsqrt(jnp.sum(x * x, axis=-1, keepdims=True) + EPS)

    qs_ref[...] = l2n(conv_silu(q_ref, wq_ref)) * (DK_A ** -0.5)
    ks_ref[...] = l2n(conv_silu(k_ref, wk_ref))
    vs_ref[...] = conv_silu(v_ref, wv_ref)
    beta, logd = _decay_terms(ba_ref[...], alog_ref[...], dtb_ref[...])
    bt_ref[...] = beta
    ld_ref[...] = logd

    row = lax.broadcasted_iota(jnp.int32, (c, c), 0)
    col = lax.broadcasted_iota(jnp.int32, (c, c), 1)
    tril = row >= col
    strict = row > col
    tril_f = jnp.where(tril, 1.0, 0.0).astype(BF16)
    masks = _doubling_masks(c)

    def chunk_terms(ci):
        rows = pl.ds(pl.multiple_of(ci * c, c), c)
        qc, kc, vc = qs_ref[rows, :], ks_ref[rows, :], vs_ref[rows, :]
        bcol = _lane_pick(bt_ref[rows, :], h)
        ldc = ld_ref[rows, :]
        l1 = ldc.astype(BF16)
        r1 = ldc - l1.astype(F32)
        l2 = r1.astype(BF16)
        l3 = (r1 - l2.astype(F32)).astype(BF16)
        d = functools.partial(jnp.dot, preferred_element_type=F32)
        gcol = _lane_pick(d(tril_f, l1) + d(tril_f, l2) + d(tril_f, l3), H_A + h)
        gmat = jnp.broadcast_to(gcol, (c, c))
        diff = gmat - gmat.T
        decay = jnp.where(tril, jnp.exp(jnp.where(tril, diff, 0.0)), 0.0)
        kb = kc * bcol
        lm = jnp.where(strict, _dot_nt(kb, kc) * decay, 0.0)
        eg = jnp.exp(gcol)
        glast = gcol[c - 1:c, :]
        return (lm, vc * bcol, kb * eg, _dot_nt(qc, kc) * decay, qc * eg,
                kc * jnp.exp(glast - gcol), jnp.broadcast_to(jnp.exp(glast), (SUBLANES, LANES)))

    n_chunks = n // c
    group = math.gcd(n_chunks, 8)

    def group_body(gi, carry):
        terms = [chunk_terms(gi * group + j) for j in range(group)]
        t = _unit_lower_inverse(jnp.stack([tm[0] for tm in terms]), masks)
        for j, (_, vb, kbg, qk, qg, kd, gl) in enumerate(terms):
            ci = gi * group + j
            rows = pl.ds(pl.multiple_of(ci * c, c), c)
            u_ref[rows, :] = _dot(t[j], vb)
            w_ref[rows, :] = _dot(t[j], kbg)
            qk_ref[rows, :] = qk
            qg_ref[rows, :] = qg
            kd_ref[rows, :] = kd
            gl_ref[pl.ds(pl.multiple_of(ci * SUBLANES, SUBLANES), SUBLANES), :] = gl
        return carry

    lax.fori_loop(0, n_chunks // group, group_body, 0)

    def state_body(ci, s):
        rows = pl.ds(pl.multiple_of(ci * c, c), c)
        v_new = u_ref[rows, :] - _dot(w_ref[rows, :], s)
        o_ref[rows, :] = _dot(qg_ref[rows, :], s) + _dot(qk_ref[rows, :], v_new)
        g_last = gl_ref[pl.ds(pl.multiple_of(ci * SUBLANES, SUBLANES), SUBLANES), :][0:1, 0:1]
        return s * g_last + _dot(kd_ref[rows, :].T, v_new)

    s_ref[...] = lax.fori_loop(0, n_chunks, state_body, jnp.zeros((DK_A, DV_A), F32))


def _delta_prompt(proj3, ba3, conv_w, alog_row, dtb_row):
    b, n, _ = proj3.shape
    hb = DK_A // LANES

    def col(off):
        return lambda i, h: (i, 0, off // LANES + h * hb)

    def wcol(off):
        return lambda i, h: (0, off // LANES + h * hb)

    seq = lambda off: pl.BlockSpec((None, n, LANES), col(off))
    wsp = lambda off: pl.BlockSpec((CONV_W, LANES), wcol(off))
    row = pl.BlockSpec((1, LANES), lambda i, h: (0, 0))
    return pl.pallas_call(
        _delta_kernel,
        grid=(b, H_A),
        in_specs=[seq(0), seq(QA), seq(2 * QA),
                  pl.BlockSpec((None, n, LANES), lambda i, h: (i, 0, 0)),
                  wsp(0), wsp(QA), wsp(2 * QA), row, row],
        out_specs=(pl.BlockSpec((None, n, DV_A), lambda i, h: (i, 0, h)),
                   pl.BlockSpec((None, None, DK_A, DV_A), lambda i, h: (i, h, 0, 0))),
        out_shape=(jax.ShapeDtypeStruct((b, n, VA), F32),
                   jax.ShapeDtypeStruct((b, H_A, DK_A, DV_A), F32)),
        scratch_shapes=[pltpu.VMEM((n + 8, LANES), F32)] + [pltpu.VMEM((n, LANES), F32)] * 10
                       + [pltpu.VMEM((n // CHUNK * SUBLANES, LANES), F32)],
        compiler_params=_cparams("parallel", "parallel"),
        name="delta_prompt",
    )(proj3, proj3, proj3, ba3, conv_w, conv_w, conv_w, alog_row, dtb_row)


def _ret_kernel(q_ref, k_ref, v_ref, cos_ref, sin_ref, logg_ref, o_ref, s_ref, qs_ref, ks_ref):
    n = q_ref.shape[0]
    c = CHUNK
    cos2, sin2 = cos_ref[...], sin_ref[...]
    qs_ref[...] = _rope(q_ref[...], cos2, sin2)
    ks_ref[...] = _rope(k_ref[...], cos2, sin2) * (DK_B ** -0.5)

    logg = logg_ref[...]
    row = lax.broadcasted_iota(jnp.int32, (c, c), 0)
    col = lax.broadcasted_iota(jnp.int32, (c, c), 1)
    tril = row >= col
    dmat = jnp.where(tril, jnp.exp(jnp.where(tril, (row - col).astype(F32) * logg, 0.0)), 0.0)
    idx = lax.broadcasted_iota(jnp.int32, (c, LANES), 0).astype(F32)
    q_dec = jnp.exp((idx + 1.0) * logg)
    k_dec = jnp.exp((c - 1.0 - idx) * logg)
    g_c = jnp.exp(c * logg[:, 0:1])

    n_chunks = n // c
    group = math.gcd(n_chunks, 4)

    def body(gi, s):
        for j in range(group):
            rows = pl.ds(pl.multiple_of((gi * group + j) * c, c), c)
            qc, kc, vc = qs_ref[rows, :], ks_ref[rows, :], v_ref[rows, :]
            inner = _dot(_dot_nt(qc, kc) * dmat, vc)
            o_ref[rows, :] = inner + _dot(qc * q_dec, s)
            s = s * g_c + _dot((kc * k_dec).T, vc)
        return s

    s_ref[...] = lax.fori_loop(0, n_chunks // group, body, jnp.zeros((DK_B, DV_B), F32))


def _ret_prompt(proj3, cos2, sin2, logg_tab):
    b, n, _ = proj3.shape
    return pl.pallas_call(
        _ret_kernel,
        grid=(b, H_B),
        in_specs=[pl.BlockSpec((None, n, DK_B), lambda i, h: (i, 0, COL_QB // DK_B + h)),
                  pl.BlockSpec((None, n, DK_B), lambda i, h: (i, 0, COL_KB // DK_B + h)),
                  pl.BlockSpec((None, n, DV_B), lambda i, h: (i, 0, COL_VB // DV_B + h)),
                  pl.BlockSpec((n, DK_B), lambda i, h: (0, 0)),
                  pl.BlockSpec((n, DK_B), lambda i, h: (0, 0)),
                  pl.BlockSpec((None, 1, LANES), lambda i, h: (h, 0, 0))],
        out_specs=(pl.BlockSpec((None, n, DV_B), lambda i, h: (i, 0, h)),
                   pl.BlockSpec((None, None, DK_B, DV_B), lambda i, h: (i, h, 0, 0))),
        out_shape=(jax.ShapeDtypeStruct((b, n, VB), F32),
                   jax.ShapeDtypeStruct((b, H_B, DK_B, DV_B), F32)),
        scratch_shapes=[pltpu.VMEM((n, DK_B), F32)] * 2,
        compiler_params=_cparams("parallel", "parallel"),
        name="retention_prompt",
    )(proj3, proj3, proj3, cos2, sin2, logg_tab)


def _sample_prep_kernel(proj_ref, ba_ref, cb_ref, cw_ref, alog_ref, dtb_ref, cos_ref, sin_ref,
                        cnew_ref, va_ref, qat_ref, kat_ref, qbt_ref, kbt_ref, bg_ref):
    x = proj_ref[:, 0:CONV_CH]
    acc = cb_ref[0] * cw_ref[0:1, :]
    for i in range(1, CONV_W - 1):
        acc = acc + cb_ref[i] * cw_ref[i:i + 1, :]
    acc = acc + x * cw_ref[CONV_W - 1:CONV_W, :]
    qkv = _silu(acc)
    for i in range(CONV_W - 2):
        cnew_ref[i] = cb_ref[i + 1]
    cnew_ref[CONV_W - 2] = x

    def l2n(v):
        return v * lax.rsqrt(jnp.sum(v * v, axis=-1, keepdims=True) + EPS)

    cos2, sin2 = cos_ref[0:1, :], sin_ref[0:1, :]
    for h in range(H_A):
        hs = slice(h * DK_A, (h + 1) * DK_A)
        qat_ref[hs, :] = (l2n(qkv[:, h * DK_A:(h + 1) * DK_A]) * (DK_A ** -0.5)).T
        kat_ref[hs, :] = l2n(qkv[:, QA + h * DK_A:QA + (h + 1) * DK_A]).T
    va_ref[...] = qkv[:, 2 * QA:]
    for h in range(H_B):
        hs = slice(h * DK_B, (h + 1) * DK_B)
        qbt_ref[hs, :] = _rope(proj_ref[:, COL_QB + h * DK_B:COL_QB + (h + 1) * DK_B], cos2, sin2).T
        kbt_ref[hs, :] = (_rope(proj_ref[:, COL_KB + h * DK_B:COL_KB + (h + 1) * DK_B], cos2, sin2)
                          * (DK_B ** -0.5)).T
    beta, logd = _decay_terms(ba_ref[...], alog_ref[...], dtb_ref[...])
    lane = lax.broadcasted_iota(jnp.int32, beta.shape, 1)
    bg_ref[...] = jnp.where(lane < H_A, beta, jnp.exp(logd))


def _sample_prep(proj, ba, cb3, conv_w, alog_row, dtb_row, cos2, sin2):
    n = proj.shape[0]
    sd = lambda *s: jax.ShapeDtypeStruct(s, F32)
    return pl.pallas_call(
        _sample_prep_kernel,
        out_shape=(sd(CONV_W - 1, n, CONV_CH), sd(n, VA), sd(QA, n), sd(QA, n), sd(QB, n), sd(QB, n),
                   sd(n, LANES)),
        compiler_params=pltpu.CompilerParams(vmem_limit_bytes=VMEM_LIMIT),
        name="sample_prep",
    )(proj, ba, cb3, conv_w, alog_row, dtb_row, cos2, sin2)


SAMPLE_ROWS = 8


def _sample_state_kernel(sd_ref, sr_ref, va_ref, vb_ref, bg_ref, qat_ref, kat_ref, qbt_ref, kbt_ref,
                         logg_ref, sdn_ref, srn_ref, oa_ref, ob_ref):
    base = pl.program_id(0) * SAMPLE_ROWS
    nseq = qat_ref.shape[1]
    lane = lax.broadcasted_iota(jnp.int32, (DK_A, nseq), 1)

    def column(t_ref, h, seq):
        blk = t_ref[h * DK_A:(h + 1) * DK_A, :]
        return jnp.sum(jnp.where(lane == seq, blk, 0.0), axis=1, keepdims=True)

    for j in range(SAMPLE_ROWS):
        seq = base + j
        bg_row = bg_ref[j:j + 1, :]
        for h in range(H_A):
            kcol, qcol = column(kat_ref, h, seq), column(qat_ref, h, seq)
            beta = _lane_pick(bg_row, h)
            eg = _lane_pick(bg_row, H_A + h)
            s0 = sd_ref[j, h]
            v = va_ref[j:j + 1, h * DV_A:(h + 1) * DV_A]
            ks = jnp.sum(kcol * s0, axis=0, keepdims=True)
            v_new = beta * v - (beta * eg) * ks
            s1 = s0 * eg + kcol * v_new
            sdn_ref[j, h] = s1
            oa_ref[j:j + 1, h * DV_A:(h + 1) * DV_A] = jnp.sum(qcol * s1, axis=0, keepdims=True)
        for h in range(H_B):
            kcol, qcol = column(kbt_ref, h, seq), column(qbt_ref, h, seq)
            gamma = jnp.exp(logg_ref[h][:, 0:1])
            v = vb_ref[j:j + 1, h * DV_B:(h + 1) * DV_B]
            s1 = sr_ref[j, h] * gamma + kcol * v
            srn_ref[j, h] = s1
            ob_ref[j:j + 1, h * DV_B:(h + 1) * DV_B] = jnp.sum(qcol * s1, axis=0, keepdims=True)


def _sample_state(sd, sr, va, proj, bg, qat, kat, qbt, kbt, logg_tab):
    n = sd.shape[0]
    r = SAMPLE_ROWS
    full = lambda a: pl.BlockSpec(a.shape, lambda i: (0,) * a.ndim)
    return pl.pallas_call(
        _sample_state_kernel,
        grid=(n // r,),
        in_specs=[pl.BlockSpec((r, H_A, DK_A, DV_A), lambda i: (i, 0, 0, 0)),
                  pl.BlockSpec((r, H_B, DK_B, DV_B), lambda i: (i, 0, 0, 0)),
                  pl.BlockSpec((r, VA), lambda i: (i, 0)),
                  pl.BlockSpec((r, VB), lambda i: (i, COL_VB // VB)),
                  pl.BlockSpec((r, LANES), lambda i: (i, 0)),
                  full(qat), full(kat), full(qbt), full(kbt), full(logg_tab)],
        out_specs=(pl.BlockSpec((r, H_A, DK_A, DV_A), lambda i: (i, 0, 0, 0)),
                   pl.BlockSpec((r, H_B, DK_B, DV_B), lambda i: (i, 0, 0, 0)),
                   pl.BlockSpec((r, VA), lambda i: (i, 0)),
                   pl.BlockSpec((r, VB), lambda i: (i, 0))),
        out_shape=(jax.ShapeDtypeStruct(sd.shape, F32), jax.ShapeDtypeStruct(sr.shape, F32),
                   jax.ShapeDtypeStruct((n, VA), F32), jax.ShapeDtypeStruct((n, VB), F32)),
        compiler_params=_cparams("parallel"),
        name="sample_state",
    )(sd, sr, va, proj, bg, qat, kat, qbt, kbt, logg_tab)


def _postmix_kernel(oa_ref, ob_ref, ga_ref, gb_ref, sa_ref, sb_ref, x_ref, gna_ref, gnb_ref,
                    wa_ref, wb_ref, wo_ref, nf_ref, wq_ref, keys_ref, h_ref, hnt_ref, st_ref):
    def gated(o_ref, g_ref, gn_ref, heads, dv):
        parts = []
        for h in range(heads):
            o = o_ref[:, h * dv:(h + 1) * dv]
            y = o * lax.rsqrt(jnp.mean(o * o, axis=-1, keepdims=True) + EPS) * gn_ref[...]
            parts.append(y * _silu(g_ref[:, h * dv:(h + 1) * dv]))
        return jnp.concatenate(parts, axis=1)

    br_a = _dot(gated(oa_ref, ga_ref, gna_ref, H_A, DV_A), wa_ref[...])
    br_b = _dot(gated(ob_ref, gb_ref, gnb_ref, H_B, DV_B), wb_ref[...])
    merged = _sigmoid(sa_ref[...]) * br_a + _sigmoid(sb_ref[...]) * br_b
    hres = x_ref[...] + _dot(merged, wo_ref[...])
    h_ref[...] = hres
    hn = hres * lax.rsqrt(jnp.mean(hres * hres, axis=-1, keepdims=True) + EPS) * nf_ref[...]
    hnt_ref[...] = hn.T.astype(BF16)
    q = _dot(hn, wq_ref[...])
    for hp in range(2 * PEER_HEADS):
        st_ref[hp] = _dot_nt(keys_ref[hp], q[:, hp * N_KEYS:(hp + 1) * N_KEYS])


def _postmix(oa, ob, proj, x2, gn_a, gn_b, w_a, w_b, w_o, norm_ffn, w_q, keys):
    t = x2.shape[0]
    tm = min(t, 256)
    full = lambda a: pl.BlockSpec(a.shape, lambda i: (0,) * a.ndim)
    pcol = lambda width, off: pl.BlockSpec((tm, width), lambda i: (i, off // width))
    return pl.pallas_call(
        _postmix_kernel,
        grid=(t // tm,),
        in_specs=[pl.BlockSpec((tm, VA), lambda i: (i, 0)), pl.BlockSpec((tm, VB), lambda i: (i, 0)),
                  pcol(VA, COL_GATE_A), pcol(VB, COL_GATE_B), pcol(D_MODEL, COL_SEL_A),
                  pcol(D_MODEL, COL_SEL_B), pl.BlockSpec((tm, D_MODEL), lambda i: (i, 0)),
                  full(gn_a), full(gn_b), full(w_a), full(w_b), full(w_o), full(norm_ffn), full(w_q),
                  full(keys)],
        out_specs=(pl.BlockSpec((tm, D_MODEL), lambda i: (i, 0)),
                   pl.BlockSpec((D_MODEL, tm), lambda i: (0, i)),
                   pl.BlockSpec((2 * PEER_HEADS, N_KEYS, tm), lambda i: (0, 0, i))),
        out_shape=(jax.ShapeDtypeStruct((t, D_MODEL), F32), jax.ShapeDtypeStruct((D_MODEL, t), BF16),
                   jax.ShapeDtypeStruct((2 * PEER_HEADS, N_KEYS, t), F32)),
        compiler_params=_cparams("parallel"),
        name="postmix",
    )(oa, ob, proj, proj, proj, proj, x2, gn_a, gn_b, w_a, w_b, w_o, norm_ffn, w_q, keys)


A_PER_CHUNK = SUBLANES
E_CHUNK = A_PER_CHUNK * N_KEYS

def _batcher_pairs(n):
    pairs = []
    p = 1
    while p < n:
        k = p
        while k >= 1:
            for j in range(k % p, n - k, 2 * k):
                for i in range(min(k, n - j - k)):
                    if (i + j) // (2 * p) == (i + j + k) // (2 * p):
                        pairs.append((i + j, i + j + k))
            k //= 2
        p *= 2
    return pairs


_SORT_TOPK = _batcher_pairs(PEER_TOPK)


def _vmax(x, y):
    if x is None:
        return y
    return x if y is None else jnp.maximum(x, y)


def _exchange(x, y):
    if x is None or y is None:
        return _vmax(x, y), None
    return jnp.maximum(x, y), jnp.minimum(x, y)


def _sort_desc(v):
    v = list(v)
    for i, j in _SORT_TOPK:
        v[i], v[j] = _exchange(v[i], v[j])
    return v


def _merge_top(v, w):
    n = len(v)
    c = [_vmax(v[i], w[n - 1 - i]) for i in range(n)]
    d = n // 2
    while d >= 1:
        for i in range(n):
            if (i & d) == 0:
                c[i], c[i + d] = _exchange(c[i], c[i + d])
        d //= 2
    return c


def _top_scores(x):
    v = _sort_desc([x[i * SUBLANES:(i + 1) * SUBLANES, :] for i in range(N_KEYS // SUBLANES)])
    shift = SUBLANES // 2
    while shift >= 1:
        v = _merge_top(v, [pltpu.roll(t, shift, axis=0) for t in v])
        shift //= 2
    return v


def _peer_select(st_ref, p1_ref, p2_ref, th_ref, tb):
    sub = lax.broadcasted_iota(jnp.int32, (SUBLANES, LANES), 0)
    for tg in range(tb // LANES):
        lanes = slice(tg * LANES, (tg + 1) * LANES)
        top = [[None] * PEER_TOPK for _ in range(2)]
        for h in range(PEER_HEADS):
            for p in range(2):
                v = _top_scores(st_ref[2 * h + p, :, lanes])
                for r in range(PEER_TOPK):
                    top[p][r] = v[r] if h == 0 else jnp.where(sub == h, v[r], top[p][r])
        sums = [[top[0][r1] + top[1][r2] if (r1 + 1) * (r2 + 1) <= PEER_TOPK else None
                 for r2 in range(PEER_TOPK)] for r1 in range(PEER_TOPK)]
        best = sums[0]
        for r1 in range(1, PEER_TOPK):
            best = _merge_top(best, sums[r1])
        z = jnp.ones((SUBLANES, LANES), F32)
        for r in range(1, PEER_TOPK):
            z = z + jnp.exp(best[r] - best[0])
        tau = best[PEER_TOPK - 1]
        theta = []
        for r1 in range(PEER_TOPK):
            th = None
            for r2 in range(PEER_TOPK):
                if sums[r1][r2] is not None:
                    cand = jnp.where(sums[r1][r2] >= tau, top[1][r2], jnp.inf)
                    th = cand if th is None else jnp.minimum(th, cand)
            theta.append(th)
        for h in range(PEER_HEADS):
            row = slice(h, h + 1)
            s1 = st_ref[2 * h, :, lanes]
            th_a = jnp.full(s1.shape, jnp.inf, F32)
            for r1 in range(PEER_TOPK):
                th_a = jnp.where(s1 == top[0][r1][row, :], theta[r1][row, :], th_a)
            th_ref[h, :, lanes] = th_a
            p1_ref[h, :, lanes] = jnp.exp(s1 - top[0][0][row, :]) / z[row, :] * 0.5
            p2_ref[h, :, lanes] = jnp.exp(st_ref[2 * h + 1, :, lanes] - top[1][0][row, :])


def _peer_coefficients(chunk, st_ref, p1_ref, p2_ref, th_ref, act_ref, coef_ref, tb):
    a0 = pl.multiple_of(chunk * A_PER_CHUNK, A_PER_CHUNK)
    for tg in range(tb // LANES):
        lanes = slice(tg * LANES, (tg + 1) * LANES)
        tht = [th_ref[h, pl.ds(a0, A_PER_CHUNK), lanes] for h in range(PEER_HEADS)]
        p1t = [p1_ref[h, pl.ds(a0, A_PER_CHUNK), lanes] for h in range(PEER_HEADS)]
        for r in range(A_PER_CHUNK):
            rows = slice(r * N_KEYS, (r + 1) * N_KEYS)
            gate = jnp.zeros((N_KEYS, LANES), F32)
            for h in range(PEER_HEADS):
                keep = st_ref[2 * h + 1, :, lanes] >= tht[h][r:r + 1, :]
                gate = gate + jnp.where(keep, p2_ref[h, :, lanes], 0.0) * p1t[h][r:r + 1, :]
            coef_ref[rows, lanes] = (gate * _gelu_times2(act_ref[rows, lanes])).astype(BF16)


def _peer_kernel(hnt_ref, st_ref, u0_ref, uo_ref, ue_ref, vte_ref, vto_ref, ft_ref,
                 p1_ref, p2_ref, th_ref, acta_ref, actb_ref, coefa_ref, coefb_ref, acc_ref):
    s = pl.program_id(1)
    tb = hnt_ref.shape[1]
    dot = functools.partial(jnp.dot, preferred_element_type=F32)

    @pl.when(s == 0)
    def _():
        _peer_select(st_ref, p1_ref, p2_ref, th_ref, tb)
        acta_ref[...] = dot(u0_ref[...], hnt_ref[...])
        acc_ref[...] = jnp.zeros_like(acc_ref)

    actb_ref[...] = dot(uo_ref[...], hnt_ref[...])
    _peer_coefficients(2 * s, st_ref, p1_ref, p2_ref, th_ref, acta_ref, coefa_ref, tb)
    acc_ref[...] += dot(vte_ref[...], coefa_ref[...])
    acta_ref[...] = dot(ue_ref[...], hnt_ref[...])
    _peer_coefficients(2 * s + 1, st_ref, p1_ref, p2_ref, th_ref, actb_ref, coefb_ref, tb)
    acc_ref[...] += dot(vto_ref[...], coefb_ref[...])

    @pl.when(s == pl.num_programs(1) - 1)
    def _():
        ft_ref[...] = acc_ref[...]


def _peer(hnt, st, u_bf, vt_bf):
    t = hnt.shape[1]
    tb = min(t, 256)
    n_chunks = u_bf.shape[0] // E_CHUNK
    steps = n_chunks // 2
    ublk = lambda fn: pl.BlockSpec((E_CHUNK, D_MODEL), fn)
    vblk = lambda fn: pl.BlockSpec((None, D_MODEL, E_CHUNK), fn)
    return pl.pallas_call(
        _peer_kernel,
        grid=(t // tb, steps),
        in_specs=[pl.BlockSpec((D_MODEL, tb), lambda i, s: (0, i)),
                  pl.BlockSpec((2 * PEER_HEADS, N_KEYS, tb), lambda i, s: (0, 0, i)),
                  ublk(lambda i, s: (0, 0)),
                  ublk(lambda i, s: (2 * s + 1, 0)),
                  ublk(lambda i, s: (jnp.minimum(2 * s + 2, n_chunks - 1), 0)),
                  vblk(lambda i, s: (2 * s, 0, 0)),
                  vblk(lambda i, s: (2 * s + 1, 0, 0))],
        out_specs=pl.BlockSpec((D_MODEL, tb), lambda i, s: (0, i)),
        out_shape=jax.ShapeDtypeStruct((D_MODEL, t), F32),
        scratch_shapes=[pltpu.VMEM((PEER_HEADS, N_KEYS, tb), F32),
                        pltpu.VMEM((PEER_HEADS, N_KEYS, tb), F32),
                        pltpu.VMEM((PEER_HEADS, N_KEYS, tb), F32),
                        pltpu.VMEM((E_CHUNK, tb), F32),
                        pltpu.VMEM((E_CHUNK, tb), F32),
                        pltpu.VMEM((E_CHUNK, tb), BF16),
                        pltpu.VMEM((E_CHUNK, tb), BF16),
                        pltpu.VMEM((D_MODEL, tb), F32)],
        compiler_params=_cparams("parallel", "arbitrary"),
        name="peer_dense",
    )(hnt, st, u_bf, u_bf, u_bf, vt_bf, vt_bf)


def _final_kernel(h_ref, ft_ref, g_ref, y_ref):
    hres = h_ref[...] + ft_ref[...].T
    y_ref[...] = hres * lax.rsqrt(jnp.mean(hres * hres, axis=-1, keepdims=True) + EPS) * g_ref[...]


def _final(h, ft, g):
    t = h.shape[0]
    tm = min(t, 512)
    return pl.pallas_call(
        _final_kernel,
        grid=(t // tm,),
        in_specs=[pl.BlockSpec((tm, D_MODEL), lambda i: (i, 0)),
                  pl.BlockSpec((D_MODEL, tm), lambda i: (0, i)),
                  pl.BlockSpec((1, D_MODEL), lambda i: (0, 0))],
        out_specs=pl.BlockSpec((tm, D_MODEL), lambda i: (i, 0)),
        out_shape=jax.ShapeDtypeStruct((t, D_MODEL), F32),
        compiler_params=_cparams("parallel"),
        name="final_norm",
    )(h, ft, g)


def _pack_weights(norm_mix, w_in, conv_w, a_log, dt_bias, gn_a, gn_b, w_br_a, w_br_b, w_out,
                  norm_ffn, peer_wq, peer_keys, peer_u, peer_v, norm_final):
    w = w_in[0]
    n_small = 2 * H_A
    c0 = CONV_CH + VA
    w_main = jnp.concatenate([w[:, :c0], w[:, c0 + n_small:]], axis=1).astype(BF16)
    w_ba = jnp.pad(w[:, c0:c0 + n_small], ((0, 0), (0, LANES - n_small)))
    wb_hi = w_ba.astype(BF16)
    wb_lo = (w_ba - wb_hi.astype(F32)).astype(BF16)
    pad_row = lambda v: jnp.pad(v.reshape(1, H_A), ((0, 0), (H_A, LANES - 2 * H_A)))
    logg = jnp.log1p(-(2.0 ** (-5.0 - jnp.arange(H_B, dtype=F32))))
    return dict(
        norm_mix=norm_mix[0].reshape(1, D_MODEL), w_main=w_main, wb_hi=wb_hi, wb_lo=wb_lo,
        conv_w=conv_w[0], alog_row=pad_row(a_log[0]), dtb_row=pad_row(dt_bias[0]),
        logg_tab=jnp.broadcast_to(logg[:, None, None], (H_B, 1, LANES)),
        gn_a=gn_a[0].reshape(1, DV_A), gn_b=gn_b[0].reshape(1, DV_B),
        w_a=w_br_a[0].astype(BF16), w_b=w_br_b[0].astype(BF16), w_o=w_out[0].astype(BF16),
        norm_ffn=norm_ffn[0].reshape(1, D_MODEL), w_q=peer_wq[0].astype(BF16),
        keys=peer_keys[0].reshape(2 * PEER_HEADS, N_KEYS, N_KEYS).astype(BF16),
        u_bf=peer_u[0].astype(BF16),
        vt_bf=jnp.transpose(peer_v[0].astype(BF16).reshape(-1, E_CHUNK, D_MODEL), (0, 2, 1)),
        norm_final=norm_final.reshape(1, D_MODEL))


def _channel_mix(oa, ob, proj, x2, p):
    h, hnt, st = _postmix(oa, ob, proj, x2, p["gn_a"], p["gn_b"], p["w_a"], p["w_b"], p["w_o"],
                          p["norm_ffn"], p["w_q"], p["keys"])
    ft = _peer(hnt, st, p["u_bf"], p["vt_bf"])
    return _final(h, ft, p["norm_final"])


def _prompt_group(x, p):
    b, n, _ = x.shape
    x2 = x.reshape(b * n, D_MODEL)
    proj, ba = _inproj(x2, p["norm_mix"], p["w_main"], p["wb_hi"], p["wb_lo"])
    proj3 = proj.reshape(b, n, N_MAIN)
    oa, s_delta = _delta_prompt(proj3, ba.reshape(b, n, LANES), p["conv_w"], p["alog_row"], p["dtb_row"])
    cos2, sin2 = _rope_tables(n, 0)
    ob, s_ret = _ret_prompt(proj3, cos2, sin2, p["logg_tab"])
    y = _channel_mix(oa.reshape(b * n, VA), ob.reshape(b * n, VB), proj, x2, p)
    conv_new = proj3[:, n - (CONV_W - 1):, :CONV_CH]
    return y.reshape(b, n, D_MODEL), conv_new[None], s_delta[None], s_ret[None]


def _sample_group(x, conv_buf, s_delta, s_ret, p):
    n = x.shape[0]
    x2 = x.reshape(n, D_MODEL)
    proj, ba = _inproj(x2, p["norm_mix"], p["w_main"], p["wb_hi"], p["wb_lo"])
    cos2, sin2 = _rope_tables(8, PAST_LEN)
    cb3 = jnp.transpose(conv_buf, (1, 0, 2))
    cnew, va, qat, kat, qbt, kbt, bg = _sample_prep(proj, ba, cb3, p["conv_w"], p["alog_row"],
                                                    p["dtb_row"], cos2, sin2)
    sd_new, sr_new, oa, ob = _sample_state(s_delta, s_ret, va, proj, bg, qat, kat, qbt, kbt,
                                           p["logg_tab"])
    y = _channel_mix(oa, ob, proj, x2, p)
    return (y.reshape(n, 1, D_MODEL), jnp.transpose(cnew, (1, 0, 2))[None], sd_new[None], sr_new[None])


def kernel(x_prompt, x_sample, state_conv_a, state_delta, state_ret, norm_mix, w_in, conv_w, a_log, dt_bias, gn_a, gn_b, w_br_a, w_br_b, w_out, norm_ffn, peer_wq, peer_keys, peer_u, peer_v, norm_final):
    assert w_in.shape[0] == 1 and x_sample.shape[1] == 1
    p = _pack_weights(norm_mix, w_in, conv_w, a_log, dt_bias, gn_a, gn_b, w_br_a, w_br_b, w_out,
                      norm_ffn, peer_wq, peer_keys, peer_u, peer_v, norm_final)
    y_p, conv_p, delta_p, ret_p = _prompt_group(x_prompt, p)
    y_s, conv_s, delta_s, ret_s = _sample_group(x_sample, state_conv_a[0], state_delta[0],
                                                state_ret[0], p)
    return (y_p, y_s, conv_p, delta_p, ret_p, conv_s, delta_s, ret_s)
```

```python
import functools
import math

import jax
import jax.numpy as jnp
from jax import lax
from jax.experimental import pallas as pl
from jax.experimental.pallas import tpu as pltpu

F32 = jnp.float32
BF16 = jnp.bfloat16

EPS = 1e-6
D_MODEL = 1024
H_A, DK_A, DV_A, CONV_W = 4, 128, 128, 4
H_B, DK_B, DV_B = 4, 128, 256
ROPE_BASE = 10000.0
PAST_LEN = 16384
N_KEYS = 128
PEER_HEADS = 8
PEER_TOPK = 16
QA, VA, QB, VB = H_A * DK_A, H_A * DV_A, H_B * DK_B, H_B * DV_B
CONV_CH = 2 * QA + VA

COL_GATE_A, COL_QB, COL_KB, COL_VB = 1536, 2048, 2560, 3072
COL_GATE_B, COL_SEL_A, COL_SEL_B, N_MAIN = 4096, 5120, 6144, 7168
LANES = 128
SUBLANES = 8
CHUNK = 128
VMEM_LIMIT = 56 * 1024 * 1024


def _cparams(*sem):
    return pltpu.CompilerParams(dimension_semantics=sem, vmem_limit_bytes=VMEM_LIMIT)


def _dot(a, b):
    return jnp.dot(a.astype(BF16), b.astype(BF16), preferred_element_type=F32)


def _dot_nt(a, b):
    return lax.dot_general(a.astype(BF16), b.astype(BF16), (((1,), (1,)), ((), ())),
                           preferred_element_type=F32)


def _split2(a):
    hi = a.astype(BF16)
    return hi, (a - hi.astype(F32)).astype(BF16)


def _dot3(a, b):
    ah, al = _split2(a)
    bh, bl = _split2(b)
    if a.ndim == 3:
        dims = (((2,), (1,)), ((0,), (0,)))
        d = lambda x, y: lax.dot_general(x, y, dims, preferred_element_type=F32)
    else:
        d = functools.partial(jnp.dot, preferred_element_type=F32)
    return d(ah, bh) + d(al, bh) + d(ah, bl)


def _sigmoid(x):
    return 1.0 / (1.0 + jnp.exp(-x))


def _silu(x):
    return x * _sigmoid(x)


def _softplus(x):
    return jnp.maximum(x, 0.0) + jnp.log1p(jnp.exp(-jnp.abs(x)))


def _gelu_times2(x):
    return x * (1.0 + lax.erf(x * (1.0 / math.sqrt(2.0))))


def _rope_kernel(cos_ref, sin_ref, *, pos0):
    shape = cos_ref.shape
    half = shape[1] // 2
    lane = lax.broadcasted_iota(jnp.int32, shape, 1)
    row = lax.broadcasted_iota(jnp.int32, shape, 0)
    j = jnp.where(lane >= half, lane - half, lane).astype(F32)
    inv = jnp.exp(j * (-math.log(ROPE_BASE) / half))
    ang = (row + pos0).astype(F32) * inv
    s = jnp.sin(ang)
    cos_ref[...] = jnp.cos(ang)
    sin_ref[...] = jnp.where(lane >= half, s, -s)


def _rope_tables(rows, pos0):
    sds = jax.ShapeDtypeStruct((rows, DK_B), F32)
    return pl.pallas_call(functools.partial(_rope_kernel, pos0=pos0), out_shape=(sds, sds),
                          name="rope_tables")()


def _rope(x, cos2, sin2):
    return x * cos2 + pltpu.roll(x, DK_B // 2, axis=1) * sin2


def _inproj_kernel(x_ref, g_ref, w_ref, wbh_ref, wbl_ref, proj_ref, ba_ref, xn_ref):
    @pl.when(pl.program_id(1) == 0)
    def _():
        x = x_ref[...]
        y = x * lax.rsqrt(jnp.mean(x * x, axis=-1, keepdims=True) + EPS) * g_ref[...]
        yh, yl = _split2(y)
        xn_ref[...] = yh
        d = functools.partial(jnp.dot, preferred_element_type=F32)
        ba_ref[...] = d(yh, wbh_ref[...]) + d(yl, wbh_ref[...]) + d(yh, wbl_ref[...])

    proj_ref[...] = jnp.dot(xn_ref[...], w_ref[...], preferred_element_type=F32)


def _inproj(x2, g, w_main, wb_hi, wb_lo):
    t = x2.shape[0]
    tm = min(t, 1024)
    tn = 1024
    return pl.pallas_call(
        _inproj_kernel,
        grid=(t // tm, N_MAIN // tn),
        in_specs=[pl.BlockSpec((tm, D_MODEL), lambda i, j: (i, 0)),
                  pl.BlockSpec((1, D_MODEL), lambda i, j: (0, 0)),
                  pl.BlockSpec((D_MODEL, tn), lambda i, j: (0, j)),
                  pl.BlockSpec((D_MODEL, LANES), lambda i, j: (0, 0)),
                  pl.BlockSpec((D_MODEL, LANES), lambda i, j: (0, 0))],
        out_specs=(pl.BlockSpec((tm, tn), lambda i, j: (i, j)),
                   pl.BlockSpec((tm, LANES), lambda i, j: (i, 0))),
        out_shape=(jax.ShapeDtypeStruct((t, N_MAIN), F32), jax.ShapeDtypeStruct((t, LANES), F32)),
        scratch_shapes=[pltpu.VMEM((tm, D_MODEL), BF16)],
        compiler_params=_cparams("parallel", "arbitrary"),
        name="rms_inproj",
    )(x2, g, w_main, wb_hi, wb_lo)


def _lane_pick(x, idx):
    lane = lax.broadcasted_iota(jnp.int32, x.shape, 1)
    return jnp.sum(jnp.where(lane == idx, x, 0.0), axis=1, keepdims=True)


def _decay_terms(ba, alog_row, dtb_row):
    beta = _sigmoid(ba)
    logd = -jnp.exp(alog_row) * _softplus(ba + dtb_row)
    return beta, logd


def _unit_lower_inverse(lm, masks):
    n = lm.shape[-1]
    row = lax.broadcasted_iota(jnp.int32, (n, n), 0)
    col = lax.broadcasted_iota(jnp.int32, (n, n), 1)
    t = jnp.where(row == col, 1.0, 0.0) - lm * masks[0]
    for m in masks[1:]:
        t = t - _dot3(_dot3(t, lm * m), t)
    return t


def _doubling_masks(n):
    row = lax.broadcasted_iota(jnp.int32, (n, n), 0)
    col = lax.broadcasted_iota(jnp.int32, (n, n), 1)
    masks = []
    lvl = 0
    while (1 << lvl) < n:
        same = (row >> (lvl + 1)) == (col >> (lvl + 1))
        lower = ((row >> lvl) & 1) == 1
        left = ((col >> lvl) & 1) == 0
        masks.append(jnp.where(same & lower & left, 1.0, 0.0))
        lvl += 1
    return masks


def _delta_kernel(q_ref, k_ref, v_ref, ba_ref, wq_ref, wk_ref, wv_ref, alog_ref, dtb_ref,
                  o_ref, s_ref, pad_ref, qs_ref, ks_ref, vs_ref, bt_ref, ld_ref,
                  u_ref, w_ref, qk_ref, qg_ref, kd_ref, gl_ref):
    h = pl.program_id(1)
    n = q_ref.shape[0]
    c = CHUNK

    def conv_silu(x_ref, w_ref):
        pad_ref[0:8, :] = jnp.zeros((8, LANES), F32)
        pad_ref[8:8 + n, :] = x_ref[...]
        base = 8 - (CONV_W - 1)
        acc = pad_ref[base:base + n, :] * w_ref[0:1, :]
        for i in range(1, CONV_W):
            acc = acc + pad_ref[base + i:base + i + n, :] * w_ref[i:i + 1, :]
        return _silu(acc)

    def l2n(x):
        return x * lax.rsqrt(jnp.sum(x * x, axis=-1, keepdims=True) + EPS)

    qs_ref[...] = l2n(conv_silu(q_ref, wq_ref)) * (DK_A ** -0.5)
    ks_ref[...] = l2n(conv_silu(k_ref, wk_ref))
    vs_ref[...] = conv_silu(v_ref, wv_ref)
    beta, logd = _decay_terms(ba_ref[...], alog_ref[...], dtb_ref[...])
    bt_ref[...] = beta
    ld_ref[...] = logd

    row = lax.broadcasted_iota(jnp.int32, (c, c), 0)
    col = lax.broadcasted_iota(jnp.int32, (c, c), 1)
    tril = row >= col
    strict = row > col
    tril_f = jnp.where(tril, 1.0, 0.0).astype(BF16)
    masks = _doubling_masks(c)

    def chunk_terms(ci):
        rows = pl.ds(pl.multiple_of(ci * c, c), c)
        qc, kc, vc = qs_ref[rows, :], ks_ref[rows, :], vs_ref[rows, :]
        bcol = _lane_pick(bt_ref[rows, :], h)
        ldc = ld_ref[rows, :]
        l1 = ldc.astype(BF16)
        r1 = ldc - l1.astype(F32)
        l2 = r1.astype(BF16)
        l3 = (r1 - l2.astype(F32)).astype(BF16)
        d = functools.partial(jnp.dot, preferred_element_type=F32)
        gcol = _lane_pick(d(tril_f, l1) + d(tril_f, l2) + d(tril_f, l3), H_A + h)
        gmat = jnp.broadcast_to(gcol, (c, c))
        diff = gmat - gmat.T
        decay = jnp.where(tril, jnp.exp(jnp.where(tril, diff, 0.0)), 0.0)
        kb = kc * bcol
        lm = jnp.where(strict, _dot_nt(kb, kc) * decay, 0.0)
        eg = jnp.exp(gcol)
        glast = gcol[c - 1:c, :]
        return (lm, vc * bcol, kb * eg, _dot_nt(qc, kc) * decay, qc * eg,
                kc * jnp.exp(glast - gcol), jnp.broadcast_to(jnp.exp(glast), (SUBLANES, LANES)))

    n_chunks = n // c
    group = math.gcd(n_chunks, 8)

    def group_body(gi, carry):
        terms = [chunk_terms(gi * group + j) for j in range(group)]
        t = _unit_lower_inverse(jnp.stack([tm[0] for tm in terms]), masks)
        for j, (_, vb, kbg, qk, qg, kd, gl) in enumerate(terms):
            ci = gi * group + j
            rows = pl.ds(pl.multiple_of(ci * c, c), c)
            u_ref[rows, :] = _dot(t[j], vb)
            w_ref[rows, :] = _dot(t[j], kbg)
            qk_ref[rows, :] = qk
            qg_ref[rows, :] = qg
            kd_ref[rows, :] = kd
            gl_ref[pl.ds(pl.multiple_of(ci * SUBLANES, SUBLANES), SUBLANES), :] = gl
        return carry

    lax.fori_loop(0, n_chunks // group, group_body, 0)

    def state_body(ci, s):
        rows = pl.ds(pl.multiple_of(ci * c, c), c)
        v_new = u_ref[rows, :] - _dot(w_ref[rows, :], s)
        o_ref[rows, :] = _dot(qg_ref[rows, :], s) + _dot(qk_ref[rows, :], v_new)
        g_last = gl_ref[pl.ds(pl.multiple_of(ci * SUBLANES, SUBLANES), SUBLANES), :][0:1, 0:1]
        return s * g_last + _dot(kd_ref[rows, :].T, v_new)

    s_ref[...] = lax.fori_loop(0, n_chunks, state_body, jnp.zeros((DK_A, DV_A), F32))


def _delta_prompt(proj3, ba3, conv_w, alog_row, dtb_row):
    b, n, _ = proj3.shape
    hb = DK_A // LANES

    def col(off):
        return lambda i, h: (i, 0, off // LANES + h * hb)

    def wcol(off):
        return lambda i, h: (0, off // LANES + h * hb)

    seq = lambda off: pl.BlockSpec((None, n, LANES), col(off))
    wsp = lambda off: pl.BlockSpec((CONV_W, LANES), wcol(off))
    row = pl.BlockSpec((1, LANES), lambda i, h: (0, 0))
    return pl.pallas_call(
        _delta_kernel,
        grid=(b, H_A),
        in_specs=[seq(0), seq(QA), seq(2 * QA),
                  pl.BlockSpec((None, n, LANES), lambda i, h: (i, 0, 0)),
                  wsp(0), wsp(QA), wsp(2 * QA), row, row],
        out_specs=(pl.BlockSpec((None, n, DV_A), lambda i, h: (i, 0, h)),
                   pl.BlockSpec((None, None, DK_A, DV_A), lambda i, h: (i, h, 0, 0))),
        out_shape=(jax.ShapeDtypeStruct((b, n, VA), F32),
                   jax.ShapeDtypeStruct((b, H_A, DK_A, DV_A), F32)),
        scratch_shapes=[pltpu.VMEM((n + 8, LANES), F32)] + [pltpu.VMEM((n, LANES), F32)] * 10
                       + [pltpu.VMEM((n // CHUNK * SUBLANES, LANES), F32)],
        compiler_params=_cparams("parallel", "parallel"),
        name="delta_prompt",
    )(proj3, proj3, proj3, ba3, conv_w, conv_w, conv_w, alog_row, dtb_row)


def _ret_kernel(q_ref, k_ref, v_ref, cos_ref, sin_ref, logg_ref, o_ref, s_ref, qs_ref, ks_ref):
    n = q_ref.shape[0]
    c = CHUNK
    cos2, sin2 = cos_ref[...], sin_ref[...]
    qs_ref[...] = _rope(q_ref[...], cos2, sin2)
    ks_ref[...] = _rope(k_ref[...], cos2, sin2) * (DK_B ** -0.5)

    logg = logg_ref[...]
    row = lax.broadcasted_iota(jnp.int32, (c, c), 0)
    col = lax.broadcasted_iota(jnp.int32, (c, c), 1)
    tril = row >= col
    dmat = jnp.where(tril, jnp.exp(jnp.where(tril, (row - col).astype(F32) * logg, 0.0)), 0.0)
    idx = lax.broadcasted_iota(jnp.int32, (c, LANES), 0).astype(F32)
    q_dec = jnp.exp((idx + 1.0) * logg)
    k_dec = jnp.exp((c - 1.0 - idx) * logg)
    g_c = jnp.exp(c * logg[:, 0:1])

    n_chunks = n // c
    group = math.gcd(n_chunks, 4)

    def body(gi, s):
        for j in range(group):
            rows = pl.ds(pl.multiple_of((gi * group + j) * c, c), c)
            qc, kc, vc = qs_ref[rows, :], ks_ref[rows, :], v_ref[rows, :]
            inner = _dot(_dot_nt(qc, kc) * dmat, vc)
            o_ref[rows, :] = inner + _dot(qc * q_dec, s)
            s = s * g_c + _dot((kc * k_dec).T, vc)
        return s

    s_ref[...] = lax.fori_loop(0, n_chunks // group, body, jnp.zeros((DK_B, DV_B), F32))


def _ret_prompt(proj3, cos2, sin2, logg_tab):
    b, n, _ = proj3.shape
    return pl.pallas_call(
        _ret_kernel,
        grid=(b, H_B),
        in_specs=[pl.BlockSpec((None, n, DK_B), lambda i, h: (i, 0, COL_QB // DK_B + h)),
                  pl.BlockSpec((None, n, DK_B), lambda i, h: (i, 0, COL_KB // DK_B + h)),
                  pl.BlockSpec((None, n, DV_B), lambda i, h: (i, 0, COL_VB // DV_B + h)),
                  pl.BlockSpec((n, DK_B), lambda i, h: (0, 0)),
                  pl.BlockSpec((n, DK_B), lambda i, h: (0, 0)),
                  pl.BlockSpec((None, 1, LANES), lambda i, h: (h, 0, 0))],
        out_specs=(pl.BlockSpec((None, n, DV_B), lambda i, h: (i, 0, h)),
                   pl.BlockSpec((None, None, DK_B, DV_B), lambda i, h: (i, h, 0, 0))),
        out_shape=(jax.ShapeDtypeStruct((b, n, VB), F32),
                   jax.ShapeDtypeStruct((b, H_B, DK_B, DV_B), F32)),
        scratch_shapes=[pltpu.VMEM((n, DK_B), F32)] * 2,
        compiler_params=_cparams("parallel", "parallel"),
        name="retention_prompt",
    )(proj3, proj3, proj3, cos2, sin2, logg_tab)


def _sample_prep_kernel(proj_ref, ba_ref, cb_ref, cw_ref, alog_ref, dtb_ref, cos_ref, sin_ref,
                        cnew_ref, va_ref, qat_ref, kat_ref, qbt_ref, kbt_ref, bg_ref):
    x = proj_ref[:, 0:CONV_CH]
    acc = cb_ref[0] * cw_ref[0:1, :]
    for i in range(1, CONV_W - 1):
        acc = acc + cb_ref[i] * cw_ref[i:i + 1, :]
    acc = acc + x * cw_ref[CONV_W - 1:CONV_W, :]
    qkv = _silu(acc)
    for i in range(CONV_W - 2):
        cnew_ref[i] = cb_ref[i + 1]
    cnew_ref[CONV_W - 2] = x

    def l2n(v):
        return v * lax.rsqrt(jnp.sum(v * v, axis=-1, keepdims=True) + EPS)

    cos2, sin2 = cos_ref[0:1, :], sin_ref[0:1, :]
    for h in range(H_A):
        hs = slice(h * DK_A, (h + 1) * DK_A)
        qat_ref[hs, :] = (l2n(qkv[:, h * DK_A:(h + 1) * DK_A]) * (DK_A ** -0.5)).T
        kat_ref[hs, :] = l2n(qkv[:, QA + h * DK_A:QA + (h + 1) * DK_A]).T
    va_ref[...] = qkv[:, 2 * QA:]
    for h in range(H_B):
        hs = slice(h * DK_B, (h + 1) * DK_B)
        qbt_ref[hs, :] = _rope(proj_ref[:, COL_QB + h * DK_B:COL_QB + (h + 1) * DK_B], cos2, sin2).T
        kbt_ref[hs, :] = (_rope(proj_ref[:, COL_KB + h * DK_B:COL_KB + (h + 1) * DK_B], cos2, sin2)
                          * (DK_B ** -0.5)).T
    beta, logd = _decay_terms(ba_ref[...], alog_ref[...], dtb_ref[...])
    lane = lax.broadcasted_iota(jnp.int32, beta.shape, 1)
    bg_ref[...] = jnp.where(lane < H_A, beta, jnp.exp(logd))


def _sample_prep(proj, ba, cb3, conv_w, alog_row, dtb_row, cos2, sin2):
    n = proj.shape[0]
    sd = lambda *s: jax.ShapeDtypeStruct(s, F32)
    return pl.pallas_call(
        _sample_prep_kernel,
        out_shape=(sd(CONV_W - 1, n, CONV_CH), sd(n, VA), sd(QA, n), sd(QA, n), sd(QB, n), sd(QB, n),
                   sd(n, LANES)),
        compiler_params=pltpu.CompilerParams(vmem_limit_bytes=VMEM_LIMIT),
        name="sample_prep",
    )(proj, ba, cb3, conv_w, alog_row, dtb_row, cos2, sin2)


SAMPLE_ROWS = 8


def _sample_state_kernel(sd_ref, sr_ref, va_ref, vb_ref, bg_ref, qat_ref, kat_ref, qbt_ref, kbt_ref,
                         logg_ref, sdn_ref, srn_ref, oa_ref, ob_ref):
    base = pl.program_id(0) * SAMPLE_ROWS
    nseq = qat_ref.shape[1]
    lane = lax.broadcasted_iota(jnp.int32, (DK_A, nseq), 1)

    def column(t_ref, h, seq):
        blk = t_ref[h * DK_A:(h + 1) * DK_A, :]
        return jnp.sum(jnp.where(lane == seq, blk, 0.0), axis=1, keepdims=True)

    for j in range(SAMPLE_ROWS):
        seq = base + j
        bg_row = bg_ref[j:j + 1, :]
        for h in range(H_A):
            kcol, qcol = column(kat_ref, h, seq), column(qat_ref, h, seq)
            beta = _lane_pick(bg_row, h)
            eg = _lane_pick(bg_row, H_A + h)
            s0 = sd_ref[j, h]
            v = va_ref[j:j + 1, h * DV_A:(h + 1) * DV_A]
            ks = jnp.sum(kcol * s0, axis=0, keepdims=True)
            v_new = beta * v - (beta * eg) * ks
            s1 = s0 * eg + kcol * v_new
            sdn_ref[j, h] = s1
            oa_ref[j:j + 1, h * DV_A:(h + 1) * DV_A] = jnp.sum(qcol * s1, axis=0, keepdims=True)
        for h in range(H_B):
            kcol, qcol = column(kbt_ref, h, seq), column(qbt_ref, h, seq)
            gamma = jnp.exp(logg_ref[h][:, 0:1])
            v = vb_ref[j:j + 1, h * DV_B:(h + 1) * DV_B]
            s1 = sr_ref[j, h] * gamma + kcol * v
            srn_ref[j, h] = s1
            ob_ref[j:j + 1, h * DV_B:(h + 1) * DV_B] = jnp.sum(qcol * s1, axis=0, keepdims=True)


def _sample_state(sd, sr, va, proj, bg, qat, kat, qbt, kbt, logg_tab):
    n = sd.shape[0]
    r = SAMPLE_ROWS
    full = lambda a: pl.BlockSpec(a.shape, lambda i: (0,) * a.ndim)
    return pl.pallas_call(
        _sample_state_kernel,
        grid=(n // r,),
        in_specs=[pl.BlockSpec((r, H_A, DK_A, DV_A), lambda i: (i, 0, 0, 0)),
                  pl.BlockSpec((r, H_B, DK_B, DV_B), lambda i: (i, 0, 0, 0)),
                  pl.BlockSpec((r, VA), lambda i: (i, 0)),
                  pl.BlockSpec((r, VB), lambda i: (i, COL_VB // VB)),
                  pl.BlockSpec((r, LANES), lambda i: (i, 0)),
                  full(qat), full(kat), full(qbt), full(kbt), full(logg_tab)],
        out_specs=(pl.BlockSpec((r, H_A, DK_A, DV_A), lambda i: (i, 0, 0, 0)),
                   pl.BlockSpec((r, H_B, DK_B, DV_B), lambda i: (i, 0, 0, 0)),
                   pl.BlockSpec((r, VA), lambda i: (i, 0)),
                   pl.BlockSpec((r, VB), lambda i: (i, 0))),
        out_shape=(jax.ShapeDtypeStruct(sd.shape, F32), jax.ShapeDtypeStruct(sr.shape, F32),
                   jax.ShapeDtypeStruct((n, VA), F32), jax.ShapeDtypeStruct((n, VB), F32)),
        compiler_params=_cparams("parallel"),
        name="sample_state",
    )(sd, sr, va, proj, bg, qat, kat, qbt, kbt, logg_tab)


def _postmix_kernel(oa_ref, ob_ref, ga_ref, gb_ref, sa_ref, sb_ref, x_ref, gna_ref, gnb_ref,
                    wa_ref, wb_ref, wo_ref, nf_ref, wq_ref, keys_ref, h_ref, hnt_ref, st_ref):
    def gated(o_ref, g_ref, gn_ref, heads, dv):
        parts = []
        for h in range(heads):
            o = o_ref[:, h * dv:(h + 1) * dv]
            y = o * lax.rsqrt(jnp.mean(o * o, axis=-1, keepdims=True) + EPS) * gn_ref[...]
            parts.append(y * _silu(g_ref[:, h * dv:(h + 1) * dv]))
        return jnp.concatenate(parts, axis=1)

    br_a = _dot(gated(oa_ref, ga_ref, gna_ref, H_A, DV_A), wa_ref[...])
    br_b = _dot(gated(ob_ref, gb_ref, gnb_ref, H_B, DV_B), wb_ref[...])
    merged = _sigmoid(sa_ref[...]) * br_a + _sigmoid(sb_ref[...]) * br_b
    hres = x_ref[...] + _dot(merged, wo_ref[...])
    h_ref[...] = hres
    hn = hres * lax.rsqrt(jnp.mean(hres * hres, axis=-1, keepdims=True) + EPS) * nf_ref[...]
    hnt_ref[...] = hn.T.astype(BF16)
    q = _dot(hn, wq_ref[...])
    for hp in range(2 * PEER_HEADS):
        st_ref[hp] = _dot_nt(keys_ref[hp], q[:, hp * N_KEYS:(hp + 1) * N_KEYS])


def _postmix(oa, ob, proj, x2, gn_a, gn_b, w_a, w_b, w_o, norm_ffn, w_q, keys):
    t = x2.shape[0]
    tm = min(t, 256)
    full = lambda a: pl.BlockSpec(a.shape, lambda i: (0,) * a.ndim)
    pcol = lambda width, off: pl.BlockSpec((tm, width), lambda i: (i, off // width))
    return pl.pallas_call(
        _postmix_kernel,
        grid=(t // tm,),
        in_specs=[pl.BlockSpec((tm, VA), lambda i: (i, 0)), pl.BlockSpec((tm, VB), lambda i: (i, 0)),
                  pcol(VA, COL_GATE_A), pcol(VB, COL_GATE_B), pcol(D_MODEL, COL_SEL_A),
                  pcol(D_MODEL, COL_SEL_B), pl.BlockSpec((tm, D_MODEL), lambda i: (i, 0)),
                  full(gn_a), full(gn_b), full(w_a), full(w_b), full(w_o), full(norm_ffn), full(w_q),
                  full(keys)],
        out_specs=(pl.BlockSpec((tm, D_MODEL), lambda i: (i, 0)),
                   pl.BlockSpec((D_MODEL, tm), lambda i: (0, i)),
                   pl.BlockSpec((2 * PEER_HEADS, N_KEYS, tm), lambda i: (0, 0, i))),
        out_shape=(jax.ShapeDtypeStruct((t, D_MODEL), F32), jax.ShapeDtypeStruct((D_MODEL, t), BF16),
                   jax.ShapeDtypeStruct((2 * PEER_HEADS, N_KEYS, t), F32)),
        compiler_params=_cparams("parallel"),
        name="postmix",
    )(oa, ob, proj, proj, proj, proj, x2, gn_a, gn_b, w_a, w_b, w_o, norm_ffn, w_q, keys)


A_PER_CHUNK = SUBLANES
E_CHUNK = A_PER_CHUNK * N_KEYS

def _batcher_pairs(n):
    pairs = []
    p = 1
    while p < n:
        k = p
        while k >= 1:
            for j in range(k % p, n - k, 2 * k):
                for i in range(min(k, n - j - k)):
                    if (i + j) // (2 * p) == (i + j + k) // (2 * p):
                        pairs.append((i + j, i + j + k))
            k //= 2
        p *= 2
    return pairs


_SORT_TOPK = _batcher_pairs(PEER_TOPK)


def _vmax(x, y):
    if x is None:
        return y
    return x if y is None else jnp.maximum(x, y)


def _exchange(x, y):
    if x is None or y is None:
        return _vmax(x, y), None
    return jnp.maximum(x, y), jnp.minimum(x, y)


def _sort_desc(v):
    v = list(v)
    for i, j in _SORT_TOPK:
        v[i], v[j] = _exchange(v[i], v[j])
    return v


def _merge_top(v, w):
    n = len(v)
    c = [_vmax(v[i], w[n - 1 - i]) for i in range(n)]
    d = n // 2
    while d >= 1:
        for i in range(n):
            if (i & d) == 0:
                c[i], c[i + d] = _exchange(c[i], c[i + d])
        d //= 2
    return c


def _top_scores(x):
    v = _sort_desc([x[i * SUBLANES:(i + 1) * SUBLANES, :] for i in range(N_KEYS // SUBLANES)])
    shift = SUBLANES // 2
    while shift >= 1:
        v = _merge_top(v, [pltpu.roll(t, shift, axis=0) for t in v])
        shift //= 2
    return v


def _peer_select(st_ref, p1_ref, p2_ref, na_ref, rk_ref, tb):
    sub = lax.broadcasted_iota(jnp.int32, (SUBLANES, LANES), 0)
    for tg in range(tb // LANES):
        lanes = slice(tg * LANES, (tg + 1) * LANES)
        top = [[None] * PEER_TOPK for _ in range(2)]
        for h in range(PEER_HEADS):
            for p in range(2):
                v = _top_scores(st_ref[2 * h + p, :, lanes])
                for r in range(PEER_TOPK):
                    top[p][r] = v[r] if h == 0 else jnp.where(sub == h, v[r], top[p][r])
        sums = [[top[0][r1] + top[1][r2] if (r1 + 1) * (r2 + 1) <= PEER_TOPK else None
                 for r2 in range(PEER_TOPK)] for r1 in range(PEER_TOPK)]
        best = sums[0]
        for r1 in range(1, PEER_TOPK):
            best = _merge_top(best, sums[r1])
        z = jnp.ones((SUBLANES, LANES), F32)
        for r in range(1, PEER_TOPK):
            z = z + jnp.exp(best[r] - best[0])
        tau = best[PEER_TOPK - 1]
        count = []
        for r1 in range(PEER_TOPK):
            cnt = jnp.zeros((SUBLANES, LANES), F32)
            for r2 in range(PEER_TOPK):
                if sums[r1][r2] is not None:
                    cnt = cnt + jnp.where(sums[r1][r2] >= tau, 1.0, 0.0)
            count.append(cnt)
        half = N_KEYS // 2
        for h in range(PEER_HEADS):
            row = slice(h, h + 1)
            for k0 in (0, half):
                keys = slice(k0, k0 + half)
                s1 = st_ref[2 * h, keys, lanes]
                n_a = jnp.zeros(s1.shape, F32)
                for r in range(PEER_TOPK):
                    n_a = jnp.where(s1 == top[0][r][row, :], count[r][row, :], n_a)
                na_ref[h, keys, lanes] = n_a
                p1_ref[h, keys, lanes] = jnp.exp(s1 - top[0][0][row, :]) / z[row, :] * 0.5
                s2 = st_ref[2 * h + 1, keys, lanes]
                rank = jnp.zeros(s2.shape, F32)
                for r in range(PEER_TOPK):
                    rank = rank + jnp.where(s2 < top[1][r][row, :], 1.0, 0.0)
                rk_ref[h, keys, lanes] = rank.astype(BF16)
                p2_ref[h, keys, lanes] = jnp.exp(s2 - top[1][0][row, :]).astype(BF16)


def _peer_coefficients(chunk, p1_ref, p2_ref, na_ref, rk_ref, act_ref, coef_ref, tb):
    a0 = pl.multiple_of(chunk * A_PER_CHUNK, A_PER_CHUNK)
    tile = (N_KEYS, LANES)
    for tg in range(tb // LANES):
        lanes = slice(tg * LANES, (tg + 1) * LANES)
        nat = [na_ref[h, pl.ds(a0, A_PER_CHUNK), lanes] for h in range(PEER_HEADS)]
        p1t = [p1_ref[h, pl.ds(a0, A_PER_CHUNK), lanes] for h in range(PEER_HEADS)]
        for r in range(A_PER_CHUNK):
            rows = slice(r * N_KEYS, (r + 1) * N_KEYS)
            gate = jnp.zeros(tile, BF16)
            for h in range(PEER_HEADS):
                n_row = jnp.broadcast_to(nat[h][r:r + 1, :], tile).astype(BF16)
                p1_row = jnp.broadcast_to(p1t[h][r:r + 1, :], tile).astype(BF16)
                keep = rk_ref[h, :, lanes] < n_row
                gate = gate + jnp.where(keep, p2_ref[h, :, lanes], jnp.zeros(tile, BF16)) * p1_row
            coef_ref[rows, lanes] = gate * _gelu_times2(act_ref[rows, lanes]).astype(BF16)


def _peer_kernel(hnt_ref, st_ref, u0_ref, uo_ref, ue_ref, vte_ref, vto_ref, ft_ref,
                 p1_ref, p2_ref, na_ref, rk_ref, acta_ref, actb_ref, coefa_ref, coefb_ref, acc_ref):
    s = pl.program_id(1)
    tb = hnt_ref.shape[1]
    dot = functools.partial(jnp.dot, preferred_element_type=F32)

    @pl.when(s == 0)
    def _():
        _peer_select(st_ref, p1_ref, p2_ref, na_ref, rk_ref, tb)
        acta_ref[...] = dot(u0_ref[...], hnt_ref[...])
        acc_ref[...] = jnp.zeros_like(acc_ref)

    actb_ref[...] = dot(uo_ref[...], hnt_ref[...])
    _peer_coefficients(2 * s, p1_ref, p2_ref, na_ref, rk_ref, acta_ref, coefa_ref, tb)
    acc_ref[...] += dot(vte_ref[...], coefa_ref[...])
    acta_ref[...] = dot(ue_ref[...], hnt_ref[...])
    _peer_coefficients(2 * s + 1, p1_ref, p2_ref, na_ref, rk_ref, actb_ref, coefb_ref, tb)
    acc_ref[...] += dot(vto_ref[...], coefb_ref[...])

    @pl.when(s == pl.num_programs(1) - 1)
    def _():
        ft_ref[...] = acc_ref[...]


def _peer(hnt, st, u_bf, vt_bf):
    t = hnt.shape[1]
    tb = min(t, 256)
    n_chunks = u_bf.shape[0] // E_CHUNK
    steps = n_chunks // 2
    ublk = lambda fn: pl.BlockSpec((E_CHUNK, D_MODEL), fn)
    vblk = lambda fn: pl.BlockSpec((None, D_MODEL, E_CHUNK), fn)
    return pl.pallas_call(
        _peer_kernel,
        grid=(t // tb, steps),
        in_specs=[pl.BlockSpec((D_MODEL, tb), lambda i, s: (0, i)),
                  pl.BlockSpec((2 * PEER_HEADS, N_KEYS, tb), lambda i, s: (0, 0, i)),
                  ublk(lambda i, s: (0, 0)),
                  ublk(lambda i, s: (2 * s + 1, 0)),
                  ublk(lambda i, s: (jnp.minimum(2 * s + 2, n_chunks - 1), 0)),
                  vblk(lambda i, s: (2 * s, 0, 0)),
                  vblk(lambda i, s: (2 * s + 1, 0, 0))],
        out_specs=pl.BlockSpec((D_MODEL, tb), lambda i, s: (0, i)),
        out_shape=jax.ShapeDtypeStruct((D_MODEL, t), F32),
        scratch_shapes=[pltpu.VMEM((PEER_HEADS, N_KEYS, tb), F32),
                        pltpu.VMEM((PEER_HEADS, N_KEYS, tb), BF16),
                        pltpu.VMEM((PEER_HEADS, N_KEYS, tb), F32),
                        pltpu.VMEM((PEER_HEADS, N_KEYS, tb), BF16),
                        pltpu.VMEM((E_CHUNK, tb), F32),
                        pltpu.VMEM((E_CHUNK, tb), F32),
                        pltpu.VMEM((E_CHUNK, tb), BF16),
                        pltpu.VMEM((E_CHUNK, tb), BF16),
                        pltpu.VMEM((D_MODEL, tb), F32)],
        compiler_params=_cparams("parallel", "arbitrary"),
        name="peer_dense",
    )(hnt, st, u_bf, u_bf, u_bf, vt_bf, vt_bf)


def _final_kernel(h_ref, ft_ref, g_ref, y_ref):
    hres = h_ref[...] + ft_ref[...].T
    y_ref[...] = hres * lax.rsqrt(jnp.mean(hres * hres, axis=-1, keepdims=True) + EPS) * g_ref[...]


def _final(h, ft, g):
    t = h.shape[0]
    tm = min(t, 512)
    return pl.pallas_call(
        _final_kernel,
        grid=(t // tm,),
        in_specs=[pl.BlockSpec((tm, D_MODEL), lambda i: (i, 0)),
                  pl.BlockSpec((D_MODEL, tm), lambda i: (0, i)),
                  pl.BlockSpec((1, D_MODEL), lambda i: (0, 0))],
        out_specs=pl.BlockSpec((tm, D_MODEL), lambda i: (i, 0)),
        out_shape=jax.ShapeDtypeStruct((t, D_MODEL), F32),
        compiler_params=_cparams("parallel"),
        name="final_norm",
    )(h, ft, g)


def _pack_weights(norm_mix, w_in, conv_w, a_log, dt_bias, gn_a, gn_b, w_br_a, w_br_b, w_out,
                  norm_ffn, peer_wq, peer_keys, peer_u, peer_v, norm_final):
    w = w_in[0]
    n_small = 2 * H_A
    c0 = CONV_CH + VA
    w_main = jnp.concatenate([w[:, :c0], w[:, c0 + n_small:]], axis=1).astype(BF16)
    w_ba = jnp.pad(w[:, c0:c0 + n_small], ((0, 0), (0, LANES - n_small)))
    wb_hi = w_ba.astype(BF16)
    wb_lo = (w_ba - wb_hi.astype(F32)).astype(BF16)
    pad_row = lambda v: jnp.pad(v.reshape(1, H_A), ((0, 0), (H_A, LANES - 2 * H_A)))
    logg = jnp.log1p(-(2.0 ** (-5.0 - jnp.arange(H_B, dtype=F32))))
    return dict(
        norm_mix=norm_mix[0].reshape(1, D_MODEL), w_main=w_main, wb_hi=wb_hi, wb_lo=wb_lo,
        conv_w=conv_w[0], alog_row=pad_row(a_log[0]), dtb_row=pad_row(dt_bias[0]),
        logg_tab=jnp.broadcast_to(logg[:, None, None], (H_B, 1, LANES)),
        gn_a=gn_a[0].reshape(1, DV_A), gn_b=gn_b[0].reshape(1, DV_B),
        w_a=w_br_a[0].astype(BF16), w_b=w_br_b[0].astype(BF16), w_o=w_out[0].astype(BF16),
        norm_ffn=norm_ffn[0].reshape(1, D_MODEL), w_q=peer_wq[0].astype(BF16),
        keys=peer_keys[0].reshape(2 * PEER_HEADS, N_KEYS, N_KEYS).astype(BF16),
        u_bf=peer_u[0].astype(BF16),
        vt_bf=jnp.transpose(peer_v[0].astype(BF16).reshape(-1, E_CHUNK, D_MODEL), (0, 2, 1)),
        norm_final=norm_final.reshape(1, D_MODEL))


def _channel_mix(oa, ob, proj, x2, p):
    h, hnt, st = _postmix(oa, ob, proj, x2, p["gn_a"], p["gn_b"], p["w_a"], p["w_b"], p["w_o"],
                          p["norm_ffn"], p["w_q"], p["keys"])
    ft = _peer(hnt, st, p["u_bf"], p["vt_bf"])
    return _final(h, ft, p["norm_final"])


def _prompt_group(x, p):
    b, n, _ = x.shape
    x2 = x.reshape(b * n, D_MODEL)
    proj, ba = _inproj(x2, p["norm_mix"], p["w_main"], p["wb_hi"], p["wb_lo"])
    proj3 = proj.reshape(b, n, N_MAIN)
    oa, s_delta = _delta_prompt(proj3, ba.reshape(b, n, LANES), p["conv_w"], p["alog_row"], p["dtb_row"])
    cos2, sin2 = _rope_tables(n, 0)
    ob, s_ret = _ret_prompt(proj3, cos2, sin2, p["logg_tab"])
    y = _channel_mix(oa.reshape(b * n, VA), ob.reshape(b * n, VB), proj, x2, p)
    conv_new = proj3[:, n - (CONV_W - 1):, :CONV_CH]
    return y.reshape(b, n, D_MODEL), conv_new[None], s_delta[None], s_ret[None]


def _sample_group(x, conv_buf, s_delta, s_ret, p):
    n = x.shape[0]
    x2 = x.reshape(n, D_MODEL)
    proj, ba = _inproj(x2, p["norm_mix"], p["w_main"], p["wb_hi"], p["wb_lo"])
    cos2, sin2 = _rope_tables(8, PAST_LEN)
    cb3 = jnp.transpose(conv_buf, (1, 0, 2))
    cnew, va, qat, kat, qbt, kbt, bg = _sample_prep(proj, ba, cb3, p["conv_w"], p["alog_row"],
                                                    p["dtb_row"], cos2, sin2)
    sd_new, sr_new, oa, ob = _sample_state(s_delta, s_ret, va, proj, bg, qat, kat, qbt, kbt,
                                           p["logg_tab"])
    y = _channel_mix(oa, ob, proj, x2, p)
    return (y.reshape(n, 1, D_MODEL), jnp.transpose(cnew, (1, 0, 2))[None], sd_new[None], sr_new[None])


def kernel(x_prompt, x_sample, state_conv_a, state_delta, state_ret, norm_mix, w_in, conv_w, a_log, dt_bias, gn_a, gn_b, w_br_a, w_br_b, w_out, norm_ffn, peer_wq, peer_keys, peer_u, peer_v, norm_final):
    assert w_in.shape[0] == 1 and x_sample.shape[1] == 1
    p = _pack_weights(norm_mix, w_in, conv_w, a_log, dt_bias, gn_a, gn_b, w_br_a, w_br_b, w_out,
                      norm_ffn, peer_wq, peer_keys, peer_u, peer_v, norm_final)
    y_p, conv_p, delta_p, ret_p = _prompt_group(x_prompt, p)
    y_s, conv_s, delta_s, ret_s = _sample_group(x_sample, state_conv_a[0], state_delta[0],
                                                state_ret[0], p)
    return (y_p, y_s, conv_p, delta_p, ret_p, conv_s, delta_s, ret_s)
```

```python
import functools
import math

import jax
import jax.numpy as jnp
from jax import lax
from jax.experimental import pallas as pl
from jax.experimental.pallas import tpu as pltpu

F32 = jnp.float32
BF16 = jnp.bfloat16

EPS = 1e-6
D_MODEL = 1024
H_A, DK_A, DV_A, CONV_W = 4, 128, 128, 4
H_B, DK_B, DV_B = 4, 128, 256
ROPE_BASE = 10000.0
PAST_LEN = 16384
N_KEYS = 128
PEER_HEADS = 8
PEER_TOPK = 16
QA, VA, QB, VB = H_A * DK_A, H_A * DV_A, H_B * DK_B, H_B * DV_B
CONV_CH = 2 * QA + VA

COL_GATE_A, COL_QB, COL_KB, COL_VB = 1536, 2048, 2560, 3072
COL_GATE_B, COL_SEL_A, COL_SEL_B, N_MAIN = 4096, 5120, 6144, 7168
LANES = 128
SUBLANES = 8
CHUNK = 128
VMEM_LIMIT = 56 * 1024 * 1024


def _cparams(*sem):
    return pltpu.CompilerParams(dimension_semantics=sem, vmem_limit_bytes=VMEM_LIMIT)


def _dot(a, b):
    return jnp.dot(a.astype(BF16), b.astype(BF16), preferred_element_type=F32)


def _dot_nt(a, b):
    return lax.dot_general(a.astype(BF16), b.astype(BF16), (((1,), (1,)), ((), ())),
                           preferred_element_type=F32)


def _split2(a):
    hi = a.astype(BF16)
    return hi, (a - hi.astype(F32)).astype(BF16)


def _dot3(a, b):
    ah, al = _split2(a)
    bh, bl = _split2(b)
    if a.ndim == 3:
        dims = (((2,), (1,)), ((0,), (0,)))
        d = lambda x, y: lax.dot_general(x, y, dims, preferred_element_type=F32)
    else:
        d = functools.partial(jnp.dot, preferred_element_type=F32)
    return d(ah, bh) + d(al, bh) + d(ah, bl)


def _sigmoid(x):
    return 1.0 / (1.0 + jnp.exp(-x))


def _silu(x):
    return x * _sigmoid(x)


def _softplus(x):
    return jnp.maximum(x, 0.0) + jnp.log1p(jnp.exp(-jnp.abs(x)))


def _gelu_times2(x):
    return x * (1.0 + lax.erf(x * (1.0 / math.sqrt(2.0))))


def _rope_kernel(cos_ref, sin_ref, *, pos0):
    shape = cos_ref.shape
    half = shape[1] // 2
    lane = lax.broadcasted_iota(jnp.int32, shape, 1)
    row = lax.broadcasted_iota(jnp.int32, shape, 0)
    j = jnp.where(lane >= half, lane - half, lane).astype(F32)
    inv = jnp.exp(j * (-math.log(ROPE_BASE) / half))
    ang = (row + pos0).astype(F32) * inv
    s = jnp.sin(ang)
    cos_ref[...] = jnp.cos(ang)
    sin_ref[...] = jnp.where(lane >= half, s, -s)


def _rope_tables(rows, pos0):
    sds = jax.ShapeDtypeStruct((rows, DK_B), F32)
    return pl.pallas_call(functools.partial(_rope_kernel, pos0=pos0), out_shape=(sds, sds),
                          name="rope_tables")()


def _rope(x, cos2, sin2):
    return x * cos2 + pltpu.roll(x, DK_B // 2, axis=1) * sin2


def _inproj_kernel(x_ref, g_ref, w_ref, wbh_ref, wbl_ref, proj_ref, ba_ref, xn_ref):
    @pl.when(pl.program_id(1) == 0)
    def _():
        x = x_ref[...]
        y = x * lax.rsqrt(jnp.mean(x * x, axis=-1, keepdims=True) + EPS) * g_ref[...]
        yh, yl = _split2(y)
        xn_ref[...] = yh
        d = functools.partial(jnp.dot, preferred_element_type=F32)
        ba_ref[...] = d(yh, wbh_ref[...]) + d(yl, wbh_ref[...]) + d(yh, wbl_ref[...])

    proj_ref[...] = jnp.dot(xn_ref[...], w_ref[...], preferred_element_type=F32)


def _inproj(x2, g, w_main, wb_hi, wb_lo):
    t = x2.shape[0]
    tm = min(t, 1024)
    tn = 1024
    return pl.pallas_call(
        _inproj_kernel,
        grid=(t // tm, N_MAIN // tn),
        in_specs=[pl.BlockSpec((tm, D_MODEL), lambda i, j: (i, 0)),
                  pl.BlockSpec((1, D_MODEL), lambda i, j: (0, 0)),
                  pl.BlockSpec((D_MODEL, tn), lambda i, j: (0, j)),
                  pl.BlockSpec((D_MODEL, LANES), lambda i, j: (0, 0)),
                  pl.BlockSpec((D_MODEL, LANES), lambda i, j: (0, 0))],
        out_specs=(pl.BlockSpec((tm, tn), lambda i, j: (i, j)),
                   pl.BlockSpec((tm, LANES), lambda i, j: (i, 0))),
        out_shape=(jax.ShapeDtypeStruct((t, N_MAIN), F32), jax.ShapeDtypeStruct((t, LANES), F32)),
        scratch_shapes=[pltpu.VMEM((tm, D_MODEL), BF16)],
        compiler_params=_cparams("parallel", "arbitrary"),
        name="rms_inproj",
    )(x2, g, w_main, wb_hi, wb_lo)


def _lane_pick(x, idx):
    lane = lax.broadcasted_iota(jnp.int32, x.shape, 1)
    return jnp.sum(jnp.where(lane == idx, x, 0.0), axis=1, keepdims=True)


def _decay_terms(ba, alog_row, dtb_row):
    beta = _sigmoid(ba)
    logd = -jnp.exp(alog_row) * _softplus(ba + dtb_row)
    return beta, logd


def _unit_lower_inverse(lm, masks):
    n = lm.shape[-1]
    row = lax.broadcasted_iota(jnp.int32, (n, n), 0)
    col = lax.broadcasted_iota(jnp.int32, (n, n), 1)
    t = jnp.where(row == col, 1.0, 0.0) - lm * masks[0]
    for m in masks[1:]:
        t = t - _dot3(_dot3(t, lm * m), t)
    return t


def _doubling_masks(n):
    row = lax.broadcasted_iota(jnp.int32, (n, n), 0)
    col = lax.broadcasted_iota(jnp.int32, (n, n), 1)
    masks = []
    lvl = 0
    while (1 << lvl) < n:
        same = (row >> (lvl + 1)) == (col >> (lvl + 1))
        lower = ((row >> lvl) & 1) == 1
        left = ((col >> lvl) & 1) == 0
        masks.append(jnp.where(same & lower & left, 1.0, 0.0))
        lvl += 1
    return masks


def _delta_kernel(q_ref, k_ref, v_ref, ba_ref, wq_ref, wk_ref, wv_ref, alog_ref, dtb_ref,
                  o_ref, s_ref, pad_ref, qs_ref, ks_ref, vs_ref, bt_ref, ld_ref,
                  u_ref, w_ref, qk_ref, qg_ref, kd_ref, gl_ref):
    h = pl.program_id(1)
    n = q_ref.shape[0]
    c = CHUNK

    def conv_silu(x_ref, w_ref):
        pad_ref[0:8, :] = jnp.zeros((8, LANES), F32)
        pad_ref[8:8 + n, :] = x_ref[...]
        base = 8 - (CONV_W - 1)
        acc = pad_ref[base:base + n, :] * w_ref[0:1, :]
        for i in range(1, CONV_W):
            acc = acc + pad_ref[base + i:base + i + n, :] * w_ref[i:i + 1, :]
        return _silu(acc)

    def l2n(x):
        return x * lax.rsqrt(jnp.sum(x * x, axis=-1, keepdims=True) + EPS)

    qs_ref[...] = l2n(conv_silu(q_ref, wq_ref)) * (DK_A ** -0.5)
    ks_ref[...] = l2n(conv_silu(k_ref, wk_ref))
    vs_ref[...] = conv_silu(v_ref, wv_ref)
    beta, logd = _decay_terms(ba_ref[...], alog_ref[...], dtb_ref[...])
    bt_ref[...] = beta
    ld_ref[...] = logd

    row = lax.broadcasted_iota(jnp.int32, (c, c), 0)
    col = lax.broadcasted_iota(jnp.int32, (c, c), 1)
    tril = row >= col
    strict = row > col
    tril_f = jnp.where(tril, 1.0, 0.0).astype(BF16)
    masks = _doubling_masks(c)

    def chunk_terms(ci):
        rows = pl.ds(pl.multiple_of(ci * c, c), c)
        qc, kc, vc = qs_ref[rows, :], ks_ref[rows, :], vs_ref[rows, :]
        bcol = _lane_pick(bt_ref[rows, :], h)
        ldc = ld_ref[rows, :]
        l1 = ldc.astype(BF16)
        r1 = ldc - l1.astype(F32)
        l2 = r1.astype(BF16)
        l3 = (r1 - l2.astype(F32)).astype(BF16)
        d = functools.partial(jnp.dot, preferred_element_type=F32)
        gcol = _lane_pick(d(tril_f, l1) + d(tril_f, l2) + d(tril_f, l3), H_A + h)
        gmat = jnp.broadcast_to(gcol, (c, c))
        diff = gmat - gmat.T
        decay = jnp.where(tril, jnp.exp(jnp.where(tril, diff, 0.0)), 0.0)
        kb = kc * bcol
        lm = jnp.where(strict, _dot_nt(kb, kc) * decay, 0.0)
        eg = jnp.exp(gcol)
        glast = gcol[c - 1:c, :]
        return (lm, vc * bcol, kb * eg, _dot_nt(qc, kc) * decay, qc * eg,
                kc * jnp.exp(glast - gcol), jnp.broadcast_to(jnp.exp(glast), (SUBLANES, LANES)))

    n_chunks = n // c
    group = math.gcd(n_chunks, 8)

    def group_body(gi, carry):
        terms = [chunk_terms(gi * group + j) for j in range(group)]
        t = _unit_lower_inverse(jnp.stack([tm[0] for tm in terms]), masks)
        for j, (_, vb, kbg, qk, qg, kd, gl) in enumerate(terms):
            ci = gi * group + j
            rows = pl.ds(pl.multiple_of(ci * c, c), c)
            u_ref[rows, :] = _dot(t[j], vb)
            w_ref[rows, :] = _dot(t[j], kbg)
            qk_ref[rows, :] = qk
            qg_ref[rows, :] = qg
            kd_ref[rows, :] = kd
            gl_ref[pl.ds(pl.multiple_of(ci * SUBLANES, SUBLANES), SUBLANES), :] = gl
        return carry

    lax.fori_loop(0, n_chunks // group, group_body, 0)

    def state_body(ci, s):
        rows = pl.ds(pl.multiple_of(ci * c, c), c)
        v_new = u_ref[rows, :] - _dot(w_ref[rows, :], s)
        o_ref[rows, :] = _dot(qg_ref[rows, :], s) + _dot(qk_ref[rows, :], v_new)
        g_last = gl_ref[pl.ds(pl.multiple_of(ci * SUBLANES, SUBLANES), SUBLANES), :][0:1, 0:1]
        return s * g_last + _dot(kd_ref[rows, :].T, v_new)

    s_ref[...] = lax.fori_loop(0, n_chunks, state_body, jnp.zeros((DK_A, DV_A), F32))


def _delta_prompt(proj3, ba3, conv_w, alog_row, dtb_row):
    b, n, _ = proj3.shape
    hb = DK_A // LANES

    def col(off):
        return lambda i, h: (i, 0, off // LANES + h * hb)

    def wcol(off):
        return lambda i, h: (0, off // LANES + h * hb)

    seq = lambda off: pl.BlockSpec((None, n, LANES), col(off))
    wsp = lambda off: pl.BlockSpec((CONV_W, LANES), wcol(off))
    row = pl.BlockSpec((1, LANES), lambda i, h: (0, 0))
    return pl.pallas_call(
        _delta_kernel,
        grid=(b, H_A),
        in_specs=[seq(0), seq(QA), seq(2 * QA),
                  pl.BlockSpec((None, n, LANES), lambda i, h: (i, 0, 0)),
                  wsp(0), wsp(QA), wsp(2 * QA), row, row],
        out_specs=(pl.BlockSpec((None, n, DV_A), lambda i, h: (i, 0, h)),
                   pl.BlockSpec((None, None, DK_A, DV_A), lambda i, h: (i, h, 0, 0))),
        out_shape=(jax.ShapeDtypeStruct((b, n, VA), F32),
                   jax.ShapeDtypeStruct((b, H_A, DK_A, DV_A), F32)),
        scratch_shapes=[pltpu.VMEM((n + 8, LANES), F32)] + [pltpu.VMEM((n, LANES), F32)] * 10
                       + [pltpu.VMEM((n // CHUNK * SUBLANES, LANES), F32)],
        compiler_params=_cparams("parallel", "parallel"),
        name="delta_prompt",
    )(proj3, proj3, proj3, ba3, conv_w, conv_w, conv_w, alog_row, dtb_row)


def _ret_kernel(q_ref, k_ref, v_ref, cos_ref, sin_ref, logg_ref, o_ref, s_ref, qs_ref, ks_ref):
    n = q_ref.shape[0]
    c = CHUNK
    cos2, sin2 = cos_ref[...], sin_ref[...]
    qs_ref[...] = _rope(q_ref[...], cos2, sin2)
    ks_ref[...] = _rope(k_ref[...], cos2, sin2) * (DK_B ** -0.5)

    logg = logg_ref[...]
    row = lax.broadcasted_iota(jnp.int32, (c, c), 0)
    col = lax.broadcasted_iota(jnp.int32, (c, c), 1)
    tril = row >= col
    dmat = jnp.where(tril, jnp.exp(jnp.where(tril, (row - col).astype(F32) * logg, 0.0)), 0.0)
    idx = lax.broadcasted_iota(jnp.int32, (c, LANES), 0).astype(F32)
    q_dec = jnp.exp((idx + 1.0) * logg)
    k_dec = jnp.exp((c - 1.0 - idx) * logg)
    g_c = jnp.exp(c * logg[:, 0:1])

    n_chunks = n // c
    group = math.gcd(n_chunks, 4)

    def body(gi, s):
        for j in range(group):
            rows = pl.ds(pl.multiple_of((gi * group + j) * c, c), c)
            qc, kc, vc = qs_ref[rows, :], ks_ref[rows, :], v_ref[rows, :]
            inner = _dot(_dot_nt(qc, kc) * dmat, vc)
            o_ref[rows, :] = inner + _dot(qc * q_dec, s)
            s = s * g_c + _dot((kc * k_dec).T, vc)
        return s

    s_ref[...] = lax.fori_loop(0, n_chunks // group, body, jnp.zeros((DK_B, DV_B), F32))


def _ret_prompt(proj3, cos2, sin2, logg_tab):
    b, n, _ = proj3.shape
    return pl.pallas_call(
        _ret_kernel,
        grid=(b, H_B),
        in_specs=[pl.BlockSpec((None, n, DK_B), lambda i, h: (i, 0, COL_QB // DK_B + h)),
                  pl.BlockSpec((None, n, DK_B), lambda i, h: (i, 0, COL_KB // DK_B + h)),
                  pl.BlockSpec((None, n, DV_B), lambda i, h: (i, 0, COL_VB // DV_B + h)),
                  pl.BlockSpec((n, DK_B), lambda i, h: (0, 0)),
                  pl.BlockSpec((n, DK_B), lambda i, h: (0, 0)),
                  pl.BlockSpec((None, 1, LANES), lambda i, h: (h, 0, 0))],
        out_specs=(pl.BlockSpec((None, n, DV_B), lambda i, h: (i, 0, h)),
                   pl.BlockSpec((None, None, DK_B, DV_B), lambda i, h: (i, h, 0, 0))),
        out_shape=(jax.ShapeDtypeStruct((b, n, VB), F32),
                   jax.ShapeDtypeStruct((b, H_B, DK_B, DV_B), F32)),
        scratch_shapes=[pltpu.VMEM((n, DK_B), F32)] * 2,
        compiler_params=_cparams("parallel", "parallel"),
        name="retention_prompt",
    )(proj3, proj3, proj3, cos2, sin2, logg_tab)


def _sample_prep_kernel(proj_ref, ba_ref, cb_ref, cw_ref, alog_ref, dtb_ref, cos_ref, sin_ref,
                        cnew_ref, va_ref, qat_ref, kat_ref, qbt_ref, kbt_ref, bg_ref):
    x = proj_ref[:, 0:CONV_CH]
    acc = cb_ref[0] * cw_ref[0:1, :]
    for i in range(1, CONV_W - 1):
        acc = acc + cb_ref[i] * cw_ref[i:i + 1, :]
    acc = acc + x * cw_ref[CONV_W - 1:CONV_W, :]
    qkv = _silu(acc)
    for i in range(CONV_W - 2):
        cnew_ref[i] = cb_ref[i + 1]
    cnew_ref[CONV_W - 2] = x

    def l2n(v):
        return v * lax.rsqrt(jnp.sum(v * v, axis=-1, keepdims=True) + EPS)

    cos2, sin2 = cos_ref[0:1, :], sin_ref[0:1, :]
    for h in range(H_A):
        hs = slice(h * DK_A, (h + 1) * DK_A)
        qat_ref[hs, :] = (l2n(qkv[:, h * DK_A:(h + 1) * DK_A]) * (DK_A ** -0.5)).T
        kat_ref[hs, :] = l2n(qkv[:, QA + h * DK_A:QA + (h + 1) * DK_A]).T
    va_ref[...] = qkv[:, 2 * QA:]
    for h in range(H_B):
        hs = slice(h * DK_B, (h + 1) * DK_B)
        qbt_ref[hs, :] = _rope(proj_ref[:, COL_QB + h * DK_B:COL_QB + (h + 1) * DK_B], cos2, sin2).T
        kbt_ref[hs, :] = (_rope(proj_ref[:, COL_KB + h * DK_B:COL_KB + (h + 1) * DK_B], cos2, sin2)
                          * (DK_B ** -0.5)).T
    beta, logd = _decay_terms(ba_ref[...], alog_ref[...], dtb_ref[...])
    lane = lax.broadcasted_iota(jnp.int32, beta.shape, 1)
    bg_ref[...] = jnp.where(lane < H_A, beta, jnp.exp(logd))


def _sample_prep(proj, ba, cb3, conv_w, alog_row, dtb_row, cos2, sin2):
    n = proj.shape[0]
    sd = lambda *s: jax.ShapeDtypeStruct(s, F32)
    return pl.pallas_call(
        _sample_prep_kernel,
        out_shape=(sd(CONV_W - 1, n, CONV_CH), sd(n, VA), sd(QA, n), sd(QA, n), sd(QB, n), sd(QB, n),
                   sd(n, LANES)),
        compiler_params=pltpu.CompilerParams(vmem_limit_bytes=VMEM_LIMIT),
        name="sample_prep",
    )(proj, ba, cb3, conv_w, alog_row, dtb_row, cos2, sin2)


SAMPLE_ROWS = 8


def _sample_state_kernel(sd_ref, sr_ref, va_ref, vb_ref, bg_ref, qat_ref, kat_ref, qbt_ref, kbt_ref,
                         logg_ref, sdn_ref, srn_ref, oa_ref, ob_ref):
    base = pl.program_id(0) * SAMPLE_ROWS
    nseq = qat_ref.shape[1]
    lane = lax.broadcasted_iota(jnp.int32, (DK_A, nseq), 1)

    def column(t_ref, h, seq):
        blk = t_ref[h * DK_A:(h + 1) * DK_A, :]
        return jnp.sum(jnp.where(lane == seq, blk, 0.0), axis=1, keepdims=True)

    for j in range(SAMPLE_ROWS):
        seq = base + j
        bg_row = bg_ref[j:j + 1, :]
        for h in range(H_A):
            kcol, qcol = column(kat_ref, h, seq), column(qat_ref, h, seq)
            beta = _lane_pick(bg_row, h)
            eg = _lane_pick(bg_row, H_A + h)
            s0 = sd_ref[j, h]
            v = va_ref[j:j + 1, h * DV_A:(h + 1) * DV_A]
            ks = jnp.sum(kcol * s0, axis=0, keepdims=True)
            v_new = beta * v - (beta * eg) * ks
            s1 = s0 * eg + kcol * v_new
            sdn_ref[j, h] = s1
            oa_ref[j:j + 1, h * DV_A:(h + 1) * DV_A] = jnp.sum(qcol * s1, axis=0, keepdims=True)
        for h in range(H_B):
            kcol, qcol = column(kbt_ref, h, seq), column(qbt_ref, h, seq)
            gamma = jnp.exp(logg_ref[h][:, 0:1])
            v = vb_ref[j:j + 1, h * DV_B:(h + 1) * DV_B]
            s1 = sr_ref[j, h] * gamma + kcol * v
            srn_ref[j, h] = s1
            ob_ref[j:j + 1, h * DV_B:(h + 1) * DV_B] = jnp.sum(qcol * s1, axis=0, keepdims=True)


def _sample_state(sd, sr, va, proj, bg, qat, kat, qbt, kbt, logg_tab):
    n = sd.shape[0]
    r = SAMPLE_ROWS
    full = lambda a: pl.BlockSpec(a.shape, lambda i: (0,) * a.ndim)
    return pl.pallas_call(
        _sample_state_kernel,
        grid=(n // r,),
        in_specs=[pl.BlockSpec((r, H_A, DK_A, DV_A), lambda i: (i, 0, 0, 0)),
                  pl.BlockSpec((r, H_B, DK_B, DV_B), lambda i: (i, 0, 0, 0)),
                  pl.BlockSpec((r, VA), lambda i: (i, 0)),
                  pl.BlockSpec((r, VB), lambda i: (i, COL_VB // VB)),
                  pl.BlockSpec((r, LANES), lambda i: (i, 0)),
                  full(qat), full(kat), full(qbt), full(kbt), full(logg_tab)],
        out_specs=(pl.BlockSpec((r, H_A, DK_A, DV_A), lambda i: (i, 0, 0, 0)),
                   pl.BlockSpec((r, H_B, DK_B, DV_B), lambda i: (i, 0, 0, 0)),
                   pl.BlockSpec((r, VA), lambda i: (i, 0)),
                   pl.BlockSpec((r, VB), lambda i: (i, 0))),
        out_shape=(jax.ShapeDtypeStruct(sd.shape, F32), jax.ShapeDtypeStruct(sr.shape, F32),
                   jax.ShapeDtypeStruct((n, VA), F32), jax.ShapeDtypeStruct((n, VB), F32)),
        compiler_params=_cparams("parallel"),
        name="sample_state",
    )(sd, sr, va, proj, bg, qat, kat, qbt, kbt, logg_tab)


def _postmix_kernel(oa_ref, ob_ref, ga_ref, gb_ref, sa_ref, sb_ref, x_ref, gna_ref, gnb_ref,
                    wa_ref, wb_ref, wo_ref, nf_ref, wq_ref, keys_ref, h_ref, hnt_ref, st_ref):
    def gated(o_ref, g_ref, gn_ref, heads, dv):
        parts = []
        for h in range(heads):
            o = o_ref[:, h * dv:(h + 1) * dv]
            y = o * lax.rsqrt(jnp.mean(o * o, axis=-1, keepdims=True) + EPS) * gn_ref[...]
            parts.append(y * _silu(g_ref[:, h * dv:(h + 1) * dv]))
        return jnp.concatenate(parts, axis=1)

    br_a = _dot(gated(oa_ref, ga_ref, gna_ref, H_A, DV_A), wa_ref[...])
    br_b = _dot(gated(ob_ref, gb_ref, gnb_ref, H_B, DV_B), wb_ref[...])
    merged = _sigmoid(sa_ref[...]) * br_a + _sigmoid(sb_ref[...]) * br_b
    hres = x_ref[...] + _dot(merged, wo_ref[...])
    h_ref[...] = hres
    hn = hres * lax.rsqrt(jnp.mean(hres * hres, axis=-1, keepdims=True) + EPS) * nf_ref[...]
    hnt_ref[...] = hn.T.astype(BF16)
    q = _dot(hn, wq_ref[...])
    for hp in range(2 * PEER_HEADS):
        st_ref[hp] = _dot_nt(keys_ref[hp], q[:, hp * N_KEYS:(hp + 1) * N_KEYS])


def _postmix(oa, ob, proj, x2, gn_a, gn_b, w_a, w_b, w_o, norm_ffn, w_q, keys):
    t = x2.shape[0]
    tm = min(t, 256)
    full = lambda a: pl.BlockSpec(a.shape, lambda i: (0,) * a.ndim)
    pcol = lambda width, off: pl.BlockSpec((tm, width), lambda i: (i, off // width))
    return pl.pallas_call(
        _postmix_kernel,
        grid=(t // tm,),
        in_specs=[pl.BlockSpec((tm, VA), lambda i: (i, 0)), pl.BlockSpec((tm, VB), lambda i: (i, 0)),
                  pcol(VA, COL_GATE_A), pcol(VB, COL_GATE_B), pcol(D_MODEL, COL_SEL_A),
                  pcol(D_MODEL, COL_SEL_B), pl.BlockSpec((tm, D_MODEL), lambda i: (i, 0)),
                  full(gn_a), full(gn_b), full(w_a), full(w_b), full(w_o), full(norm_ffn), full(w_q),
                  full(keys)],
        out_specs=(pl.BlockSpec((tm, D_MODEL), lambda i: (i, 0)),
                   pl.BlockSpec((D_MODEL, tm), lambda i: (0, i)),
                   pl.BlockSpec((2 * PEER_HEADS, N_KEYS, tm), lambda i: (0, 0, i))),
        out_shape=(jax.ShapeDtypeStruct((t, D_MODEL), F32), jax.ShapeDtypeStruct((D_MODEL, t), BF16),
                   jax.ShapeDtypeStruct((2 * PEER_HEADS, N_KEYS, t), F32)),
        compiler_params=_cparams("parallel"),
        name="postmix",
    )(oa, ob, proj, proj, proj, proj, x2, gn_a, gn_b, w_a, w_b, w_o, norm_ffn, w_q, keys)


A_PER_CHUNK = SUBLANES
E_CHUNK = A_PER_CHUNK * N_KEYS
PEER_TOKENS = 512

def _batcher_pairs(n):
    pairs = []
    p = 1
    while p < n:
        k = p
        while k >= 1:
            for j in range(k % p, n - k, 2 * k):
                for i in range(min(k, n - j - k)):
                    if (i + j) // (2 * p) == (i + j + k) // (2 * p):
                        pairs.append((i + j, i + j + k))
            k //= 2
        p *= 2
    return pairs


_SORT_TOPK = _batcher_pairs(PEER_TOPK)


def _vmax(x, y):
    if x is None:
        return y
    return x if y is None else jnp.maximum(x, y)


def _exchange(x, y):
    if x is None or y is None:
        return _vmax(x, y), None
    return jnp.maximum(x, y), jnp.minimum(x, y)


def _sort_desc(v):
    v = list(v)
    for i, j in _SORT_TOPK:
        v[i], v[j] = _exchange(v[i], v[j])
    return v


def _merge_top(v, w):
    n = len(v)
    c = [_vmax(v[i], w[n - 1 - i]) for i in range(n)]
    d = n // 2
    while d >= 1:
        for i in range(n):
            if (i & d) == 0:
                c[i], c[i + d] = _exchange(c[i], c[i + d])
        d //= 2
    return c


def _top_scores(x):
    v = _sort_desc([x[i * SUBLANES:(i + 1) * SUBLANES, :] for i in range(N_KEYS // SUBLANES)])
    shift = SUBLANES // 2
    while shift >= 1:
        v = _merge_top(v, [pltpu.roll(t, shift, axis=0) for t in v])
        shift //= 2
    return v


def _peer_select(st_ref, p1_ref, p2_ref, na_ref, rk_ref, tb):
    sub = lax.broadcasted_iota(jnp.int32, (SUBLANES, LANES), 0)
    for tg in range(tb // LANES):
        lanes = slice(tg * LANES, (tg + 1) * LANES)
        top = [[None] * PEER_TOPK for _ in range(2)]
        for h in range(PEER_HEADS):
            for p in range(2):
                v = _top_scores(st_ref[2 * h + p, :, lanes])
                for r in range(PEER_TOPK):
                    top[p][r] = v[r] if h == 0 else jnp.where(sub == h, v[r], top[p][r])
        sums = [[top[0][r1] + top[1][r2] if (r1 + 1) * (r2 + 1) <= PEER_TOPK else None
                 for r2 in range(PEER_TOPK)] for r1 in range(PEER_TOPK)]
        best = sums[0]
        for r1 in range(1, PEER_TOPK):
            best = _merge_top(best, sums[r1])
        z = jnp.ones((SUBLANES, LANES), F32)
        for r in range(1, PEER_TOPK):
            z = z + jnp.exp(best[r] - best[0])
        tau = best[PEER_TOPK - 1]
        count = []
        for r1 in range(PEER_TOPK):
            cnt = jnp.zeros((SUBLANES, LANES), F32)
            for r2 in range(PEER_TOPK):
                if sums[r1][r2] is not None:
                    cnt = cnt + jnp.where(sums[r1][r2] >= tau, 1.0, 0.0)
            count.append(cnt)
        half = N_KEYS // 2
        for h in range(PEER_HEADS):
            row = slice(h, h + 1)
            for k0 in (0, half):
                keys = slice(k0, k0 + half)
                s1 = st_ref[2 * h, keys, lanes]
                n_a = jnp.zeros(s1.shape, F32)
                for r in range(PEER_TOPK):
                    n_a = jnp.where(s1 == top[0][r][row, :], count[r][row, :], n_a)
                na_ref[h, keys, lanes] = n_a
                p1_ref[h, keys, lanes] = jnp.exp(s1 - top[0][0][row, :]) / z[row, :] * 0.5
                s2 = st_ref[2 * h + 1, keys, lanes]
                rank = jnp.zeros(s2.shape, F32)
                for r in range(PEER_TOPK):
                    rank = rank + jnp.where(s2 < top[1][r][row, :], 1.0, 0.0)
                rk_ref[h, keys, lanes] = rank.astype(BF16)
                p2_ref[h, keys, lanes] = jnp.exp(s2 - top[1][0][row, :]).astype(BF16)


def _peer_coefficients(chunk, p1_ref, p2_ref, na_ref, rk_ref, act_ref, coef_ref, tb):
    a0 = pl.multiple_of(chunk * A_PER_CHUNK, A_PER_CHUNK)
    tile = (N_KEYS, LANES)
    for tg in range(tb // LANES):
        lanes = slice(tg * LANES, (tg + 1) * LANES)
        nat = [na_ref[h, pl.ds(a0, A_PER_CHUNK), lanes] for h in range(PEER_HEADS)]
        p1t = [p1_ref[h, pl.ds(a0, A_PER_CHUNK), lanes] for h in range(PEER_HEADS)]
        for r in range(A_PER_CHUNK):
            rows = slice(r * N_KEYS, (r + 1) * N_KEYS)
            gate = jnp.zeros(tile, BF16)
            for h in range(PEER_HEADS):
                n_row = jnp.broadcast_to(nat[h][r:r + 1, :], tile).astype(BF16)
                p1_row = jnp.broadcast_to(p1t[h][r:r + 1, :], tile).astype(BF16)
                keep = rk_ref[h, :, lanes] < n_row
                gate = gate + jnp.where(keep, p2_ref[h, :, lanes], jnp.zeros(tile, BF16)) * p1_row
            coef_ref[rows, lanes] = gate * _gelu_times2(act_ref[rows, lanes]).astype(BF16)


def _peer_kernel(hnt_ref, st_ref, u0_ref, uo_ref, ue_ref, vte_ref, vto_ref, ft_ref,
                 p1_ref, p2_ref, na_ref, rk_ref, acta_ref, actb_ref, coefa_ref, coefb_ref, acc_ref):
    s = pl.program_id(1)
    tb = hnt_ref.shape[1]
    dot = functools.partial(jnp.dot, preferred_element_type=F32)

    @pl.when(s == 0)
    def _():
        _peer_select(st_ref, p1_ref, p2_ref, na_ref, rk_ref, tb)
        acta_ref[...] = dot(u0_ref[...], hnt_ref[...])
        acc_ref[...] = jnp.zeros_like(acc_ref)

    actb_ref[...] = dot(uo_ref[...], hnt_ref[...])
    _peer_coefficients(2 * s, p1_ref, p2_ref, na_ref, rk_ref, acta_ref, coefa_ref, tb)
    acc_ref[...] += dot(vte_ref[...], coefa_ref[...])
    acta_ref[...] = dot(ue_ref[...], hnt_ref[...])
    _peer_coefficients(2 * s + 1, p1_ref, p2_ref, na_ref, rk_ref, actb_ref, coefb_ref, tb)
    acc_ref[...] += dot(vto_ref[...], coefb_ref[...])

    @pl.when(s == pl.num_programs(1) - 1)
    def _():
        ft_ref[...] = acc_ref[...]


def _peer(hnt, st, u_bf, vt_bf):
    t = hnt.shape[1]
    tb = min(t, PEER_TOKENS)
    n_chunks = u_bf.shape[0] // E_CHUNK
    steps = n_chunks // 2
    ublk = lambda fn: pl.BlockSpec((E_CHUNK, D_MODEL), fn)
    vblk = lambda fn: pl.BlockSpec((None, D_MODEL, E_CHUNK), fn)
    return pl.pallas_call(
        _peer_kernel,
        grid=(t // tb, steps),
        in_specs=[pl.BlockSpec((D_MODEL, tb), lambda i, s: (0, i)),
                  pl.BlockSpec((2 * PEER_HEADS, N_KEYS, tb), lambda i, s: (0, 0, i)),
                  ublk(lambda i, s: (0, 0)),
                  ublk(lambda i, s: (2 * s + 1, 0)),
                  ublk(lambda i, s: (jnp.minimum(2 * s + 2, n_chunks - 1), 0)),
                  vblk(lambda i, s: (2 * s, 0, 0)),
                  vblk(lambda i, s: (2 * s + 1, 0, 0))],
        out_specs=pl.BlockSpec((D_MODEL, tb), lambda i, s: (0, i)),
        out_shape=jax.ShapeDtypeStruct((D_MODEL, t), F32),
        scratch_shapes=[pltpu.VMEM((PEER_HEADS, N_KEYS, tb), F32),
                        pltpu.VMEM((PEER_HEADS, N_KEYS, tb), BF16),
                        pltpu.VMEM((PEER_HEADS, N_KEYS, tb), F32),
                        pltpu.VMEM((PEER_HEADS, N_KEYS, tb), BF16),
                        pltpu.VMEM((E_CHUNK, tb), F32),
                        pltpu.VMEM((E_CHUNK, tb), F32),
                        pltpu.VMEM((E_CHUNK, tb), BF16),
                        pltpu.VMEM((E_CHUNK, tb), BF16),
                        pltpu.VMEM((D_MODEL, tb), F32)],
        compiler_params=_cparams("parallel", "arbitrary"),
        name="peer_dense",
    )(hnt, st, u_bf, u_bf, u_bf, vt_bf, vt_bf)


def _final_kernel(h_ref, ft_ref, g_ref, y_ref):
    hres = h_ref[...] + ft_ref[...].T
    y_ref[...] = hres * lax.rsqrt(jnp.mean(hres * hres, axis=-1, keepdims=True) + EPS) * g_ref[...]


def _final(h, ft, g):
    t = h.shape[0]
    tm = min(t, 512)
    return pl.pallas_call(
        _final_kernel,
        grid=(t // tm,),
        in_specs=[pl.BlockSpec((tm, D_MODEL), lambda i: (i, 0)),
                  pl.BlockSpec((D_MODEL, tm), lambda i: (0, i)),
                  pl.BlockSpec((1, D_MODEL), lambda i: (0, 0))],
        out_specs=pl.BlockSpec((tm, D_MODEL), lambda i: (i, 0)),
        out_shape=jax.ShapeDtypeStruct((t, D_MODEL), F32),
        compiler_params=_cparams("parallel"),
        name="final_norm",
    )(h, ft, g)


def _pack_weights(norm_mix, w_in, conv_w, a_log, dt_bias, gn_a, gn_b, w_br_a, w_br_b, w_out,
                  norm_ffn, peer_wq, peer_keys, peer_u, peer_v, norm_final):
    w = w_in[0]
    n_small = 2 * H_A
    c0 = CONV_CH + VA
    w_main = jnp.concatenate([w[:, :c0], w[:, c0 + n_small:]], axis=1).astype(BF16)
    w_ba = jnp.pad(w[:, c0:c0 + n_small], ((0, 0), (0, LANES - n_small)))
    wb_hi = w_ba.astype(BF16)
    wb_lo = (w_ba - wb_hi.astype(F32)).astype(BF16)
    pad_row = lambda v: jnp.pad(v.reshape(1, H_A), ((0, 0), (H_A, LANES - 2 * H_A)))
    logg = jnp.log1p(-(2.0 ** (-5.0 - jnp.arange(H_B, dtype=F32))))
    return dict(
        norm_mix=norm_mix[0].reshape(1, D_MODEL), w_main=w_main, wb_hi=wb_hi, wb_lo=wb_lo,
        conv_w=conv_w[0], alog_row=pad_row(a_log[0]), dtb_row=pad_row(dt_bias[0]),
        logg_tab=jnp.broadcast_to(logg[:, None, None], (H_B, 1, LANES)),
        gn_a=gn_a[0].reshape(1, DV_A), gn_b=gn_b[0].reshape(1, DV_B),
        w_a=w_br_a[0].astype(BF16), w_b=w_br_b[0].astype(BF16), w_o=w_out[0].astype(BF16),
        norm_ffn=norm_ffn[0].reshape(1, D_MODEL), w_q=peer_wq[0].astype(BF16),
        keys=peer_keys[0].reshape(2 * PEER_HEADS, N_KEYS, N_KEYS).astype(BF16),
        u_bf=peer_u[0].astype(BF16),
        vt_bf=jnp.transpose(peer_v[0].astype(BF16).reshape(-1, E_CHUNK, D_MODEL), (0, 2, 1)),
        norm_final=norm_final.reshape(1, D_MODEL))


def _channel_mix(oa, ob, proj, x2, p):
    h, hnt, st = _postmix(oa, ob, proj, x2, p["gn_a"], p["gn_b"], p["w_a"], p["w_b"], p["w_o"],
                          p["norm_ffn"], p["w_q"], p["keys"])
    ft = _peer(hnt, st, p["u_bf"], p["vt_bf"])
    return _final(h, ft, p["norm_final"])


def _prompt_group(x, p):
    b, n, _ = x.shape
    x2 = x.reshape(b * n, D_MODEL)
    proj, ba = _inproj(x2, p["norm_mix"], p["w_main"], p["wb_hi"], p["wb_lo"])
    proj3 = proj.reshape(b, n, N_MAIN)
    oa, s_delta = _delta_prompt(proj3, ba.reshape(b, n, LANES), p["conv_w"], p["alog_row"], p["dtb_row"])
    cos2, sin2 = _rope_tables(n, 0)
    ob, s_ret = _ret_prompt(proj3, cos2, sin2, p["logg_tab"])
    y = _channel_mix(oa.reshape(b * n, VA), ob.reshape(b * n, VB), proj, x2, p)
    conv_new = proj3[:, n - (CONV_W - 1):, :CONV_CH]
    return y.reshape(b, n, D_MODEL), conv_new[None], s_delta[None], s_ret[None]


def _sample_group(x, conv_buf, s_delta, s_ret, p):
    n = x.shape[0]
    x2 = x.reshape(n, D_MODEL)
    proj, ba = _inproj(x2, p["norm_mix"], p["w_main"], p["wb_hi"], p["wb_lo"])
    cos2, sin2 = _rope_tables(8, PAST_LEN)
    cb3 = jnp.transpose(conv_buf, (1, 0, 2))
    cnew, va, qat, kat, qbt, kbt, bg = _sample_prep(proj, ba, cb3, p["conv_w"], p["alog_row"],
                                                    p["dtb_row"], cos2, sin2)
    sd_new, sr_new, oa, ob = _sample_state(s_delta, s_ret, va, proj, bg, qat, kat, qbt, kbt,
                                           p["logg_tab"])
    y = _channel_mix(oa, ob, proj, x2, p)
    return (y.reshape(n, 1, D_MODEL), jnp.transpose(cnew, (1, 0, 2))[None], sd_new[None], sr_new[None])


def kernel(x_prompt, x_sample, state_conv_a, state_delta, state_ret, norm_mix, w_in, conv_w, a_log, dt_bias, gn_a, gn_b, w_br_a, w_br_b, w_out, norm_ffn, peer_wq, peer_keys, peer_u, peer_v, norm_final):
    assert w_in.shape[0] == 1 and x_sample.shape[1] == 1
    p = _pack_weights(norm_mix, w_in, conv_w, a_log, dt_bias, gn_a, gn_b, w_br_a, w_br_b, w_out,
                      norm_ffn, peer_wq, peer_keys, peer_u, peer_v, norm_final)
    y_p, conv_p, delta_p, ret_p = _prompt_group(x_prompt, p)
    y_s, conv_s, delta_s, ret_s = _sample_group(x_sample, state_conv_a[0], state_delta[0],
                                                state_ret[0], p)
    return (y_p, y_s, conv_p, delta_p, ret_p, conv_s, delta_s, ret_s)
```

```python
import functools
import math

import jax
import jax.numpy as jnp
from jax import lax
from jax.experimental import pallas as pl
from jax.experimental.pallas import tpu as pltpu

F32 = jnp.float32
BF16 = jnp.bfloat16

EPS = 1e-6
D_MODEL = 1024
H_A, DK_A, DV_A, CONV_W = 4, 128, 128, 4
H_B, DK_B, DV_B = 4, 128, 256
ROPE_BASE = 10000.0
PAST_LEN = 16384
N_KEYS = 128
PEER_HEADS = 8
PEER_TOPK = 16
QA, VA, QB, VB = H_A * DK_A, H_A * DV_A, H_B * DK_B, H_B * DV_B
CONV_CH = 2 * QA + VA

COL_GATE_A, COL_QB, COL_KB, COL_VB = 1536, 2048, 2560, 3072
COL_GATE_B, COL_SEL_A, COL_SEL_B, N_MAIN = 4096, 5120, 6144, 7168
LANES = 128
SUBLANES = 8
CHUNK = 128
VMEM_LIMIT = 56 * 1024 * 1024


def _cparams(*sem):
    return pltpu.CompilerParams(dimension_semantics=sem, vmem_limit_bytes=VMEM_LIMIT)


def _dot(a, b):
    return jnp.dot(a.astype(BF16), b.astype(BF16), preferred_element_type=F32)


def _dot_nt(a, b):
    return lax.dot_general(a.astype(BF16), b.astype(BF16), (((1,), (1,)), ((), ())),
                           preferred_element_type=F32)


def _split2(a):
    hi = a.astype(BF16)
    return hi, (a - hi.astype(F32)).astype(BF16)


def _dot3(a, b):
    ah, al = _split2(a)
    bh, bl = _split2(b)
    if a.ndim == 3:
        dims = (((2,), (1,)), ((0,), (0,)))
        d = lambda x, y: lax.dot_general(x, y, dims, preferred_element_type=F32)
    else:
        d = functools.partial(jnp.dot, preferred_element_type=F32)
    return d(ah, bh) + d(al, bh) + d(ah, bl)


def _sigmoid(x):
    return 1.0 / (1.0 + jnp.exp(-x))


def _silu(x):
    return x * _sigmoid(x)


def _softplus(x):
    return jnp.maximum(x, 0.0) + jnp.log1p(jnp.exp(-jnp.abs(x)))


def _gelu_times2(x):
    return x * (1.0 + lax.erf(x * (1.0 / math.sqrt(2.0))))


def _rope_kernel(cos_ref, sin_ref, *, pos0):
    shape = cos_ref.shape
    half = shape[1] // 2
    lane = lax.broadcasted_iota(jnp.int32, shape, 1)
    row = lax.broadcasted_iota(jnp.int32, shape, 0)
    j = jnp.where(lane >= half, lane - half, lane).astype(F32)
    inv = jnp.exp(j * (-math.log(ROPE_BASE) / half))
    ang = (row + pos0).astype(F32) * inv
    s = jnp.sin(ang)
    cos_ref[...] = jnp.cos(ang)
    sin_ref[...] = jnp.where(lane >= half, s, -s)


def _rope_tables(rows, pos0):
    sds = jax.ShapeDtypeStruct((rows, DK_B), F32)
    return pl.pallas_call(functools.partial(_rope_kernel, pos0=pos0), out_shape=(sds, sds),
                          name="rope_tables")()


def _rope(x, cos2, sin2):
    return x * cos2 + pltpu.roll(x, DK_B // 2, axis=1) * sin2


def _inproj_kernel(x_ref, g_ref, w_ref, wbh_ref, wbl_ref, proj_ref, ba_ref, xn_ref):
    @pl.when(pl.program_id(1) == 0)
    def _():
        x = x_ref[...]
        y = x * lax.rsqrt(jnp.mean(x * x, axis=-1, keepdims=True) + EPS) * g_ref[...]
        yh, yl = _split2(y)
        xn_ref[...] = yh
        d = functools.partial(jnp.dot, preferred_element_type=F32)
        ba_ref[...] = d(yh, wbh_ref[...]) + d(yl, wbh_ref[...]) + d(yh, wbl_ref[...])

    proj_ref[...] = jnp.dot(xn_ref[...], w_ref[...], preferred_element_type=F32)


def _inproj(x2, g, w_main, wb_hi, wb_lo):
    t = x2.shape[0]
    tm = min(t, 1024)
    tn = 1024
    return pl.pallas_call(
        _inproj_kernel,
        grid=(t // tm, N_MAIN // tn),
        in_specs=[pl.BlockSpec((tm, D_MODEL), lambda i, j: (i, 0)),
                  pl.BlockSpec((1, D_MODEL), lambda i, j: (0, 0)),
                  pl.BlockSpec((D_MODEL, tn), lambda i, j: (0, j)),
                  pl.BlockSpec((D_MODEL, LANES), lambda i, j: (0, 0)),
                  pl.BlockSpec((D_MODEL, LANES), lambda i, j: (0, 0))],
        out_specs=(pl.BlockSpec((tm, tn), lambda i, j: (i, j)),
                   pl.BlockSpec((tm, LANES), lambda i, j: (i, 0))),
        out_shape=(jax.ShapeDtypeStruct((t, N_MAIN), F32), jax.ShapeDtypeStruct((t, LANES), F32)),
        scratch_shapes=[pltpu.VMEM((tm, D_MODEL), BF16)],
        compiler_params=_cparams("parallel", "arbitrary"),
        name="rms_inproj",
    )(x2, g, w_main, wb_hi, wb_lo)


def _lane_pick(x, idx):
    lane = lax.broadcasted_iota(jnp.int32, x.shape, 1)
    return jnp.sum(jnp.where(lane == idx, x, 0.0), axis=1, keepdims=True)


def _decay_terms(ba, alog_row, dtb_row):
    beta = _sigmoid(ba)
    logd = -jnp.exp(alog_row) * _softplus(ba + dtb_row)
    return beta, logd


def _unit_lower_inverse(lm, masks):
    n = lm.shape[-1]
    row = lax.broadcasted_iota(jnp.int32, (n, n), 0)
    col = lax.broadcasted_iota(jnp.int32, (n, n), 1)
    t = jnp.where(row == col, 1.0, 0.0) - lm * masks[0]
    for m in masks[1:]:
        t = t - _dot3(_dot3(t, lm * m), t)
    return t


def _doubling_masks(n):
    row = lax.broadcasted_iota(jnp.int32, (n, n), 0)
    col = lax.broadcasted_iota(jnp.int32, (n, n), 1)
    masks = []
    lvl = 0
    while (1 << lvl) < n:
        same = (row >> (lvl + 1)) == (col >> (lvl + 1))
        lower = ((row >> lvl) & 1) == 1
        left = ((col >> lvl) & 1) == 0
        masks.append(jnp.where(same & lower & left, 1.0, 0.0))
        lvl += 1
    return masks


def _delta_kernel(q_ref, k_ref, v_ref, ba_ref, wq_ref, wk_ref, wv_ref, alog_ref, dtb_ref,
                  o_ref, s_ref, pad_ref, qs_ref, ks_ref, vs_ref, bt_ref, ld_ref,
                  u_ref, w_ref, qk_ref, qg_ref, kd_ref, gl_ref):
    h = pl.program_id(1)
    n = q_ref.shape[0]
    c = CHUNK

    def conv_silu(x_ref, w_ref):
        pad_ref[0:8, :] = jnp.zeros((8, LANES), F32)
        pad_ref[8:8 + n, :] = x_ref[...]
        base = 8 - (CONV_W - 1)
        acc = pad_ref[base:base + n, :] * w_ref[0:1, :]
        for i in range(1, CONV_W):
            acc = acc + pad_ref[base + i:base + i + n, :] * w_ref[i:i + 1, :]
        return _silu(acc)

    def l2n(x):
        return x * lax.rsqrt(jnp.sum(x * x, axis=-1, keepdims=True) + EPS)

    qs_ref[...] = l2n(conv_silu(q_ref, wq_ref)) * (DK_A ** -0.5)
    ks_ref[...] = l2n(conv_silu(k_ref, wk_ref))
    vs_ref[...] = conv_silu(v_ref, wv_ref)
    beta, logd = _decay_terms(ba_ref[...], alog_ref[...], dtb_ref[...])
    bt_ref[...] = beta
    ld_ref[...] = logd

    row = lax.broadcasted_iota(jnp.int32, (c, c), 0)
    col = lax.broadcasted_iota(jnp.int32, (c, c), 1)
    tril = row >= col
    strict = row > col
    tril_f = jnp.where(tril, 1.0, 0.0).astype(BF16)
    masks = _doubling_masks(c)

    def chunk_terms(ci):
        rows = pl.ds(pl.multiple_of(ci * c, c), c)
        qc, kc, vc = qs_ref[rows, :], ks_ref[rows, :], vs_ref[rows, :]
        bcol = _lane_pick(bt_ref[rows, :], h)
        ldc = ld_ref[rows, :]
        l1 = ldc.astype(BF16)
        r1 = ldc - l1.astype(F32)
        l2 = r1.astype(BF16)
        l3 = (r1 - l2.astype(F32)).astype(BF16)
        d = functools.partial(jnp.dot, preferred_element_type=F32)
        gcol = _lane_pick(d(tril_f, l1) + d(tril_f, l2) + d(tril_f, l3), H_A + h)
        gmat = jnp.broadcast_to(gcol, (c, c))
        diff = gmat - gmat.T
        decay = jnp.where(tril, jnp.exp(jnp.where(tril, diff, 0.0)), 0.0)
        kb = kc * bcol
        lm = jnp.where(strict, _dot_nt(kb, kc) * decay, 0.0)
        eg = jnp.exp(gcol)
        glast = gcol[c - 1:c, :]
        return (lm, vc * bcol, kb * eg, _dot_nt(qc, kc) * decay, qc * eg,
                kc * jnp.exp(glast - gcol), jnp.broadcast_to(jnp.exp(glast), (SUBLANES, LANES)))

    n_chunks = n // c
    group = math.gcd(n_chunks, 8)

    def group_body(gi, carry):
        terms = [chunk_terms(gi * group + j) for j in range(group)]
        t = _unit_lower_inverse(jnp.stack([tm[0] for tm in terms]), masks)
        for j, (_, vb, kbg, qk, qg, kd, gl) in enumerate(terms):
            ci = gi * group + j
            rows = pl.ds(pl.multiple_of(ci * c, c), c)
            u_ref[rows, :] = _dot(t[j], vb)
            w_ref[rows, :] = _dot(t[j], kbg)
            qk_ref[rows, :] = qk
            qg_ref[rows, :] = qg
            kd_ref[rows, :] = kd
            gl_ref[pl.ds(pl.multiple_of(ci * SUBLANES, SUBLANES), SUBLANES), :] = gl
        return carry

    lax.fori_loop(0, n_chunks // group, group_body, 0)

    def state_body(ci, s):
        rows = pl.ds(pl.multiple_of(ci * c, c), c)
        v_new = u_ref[rows, :] - _dot(w_ref[rows, :], s)
        o_ref[rows, :] = _dot(qg_ref[rows, :], s) + _dot(qk_ref[rows, :], v_new)
        g_last = gl_ref[pl.ds(pl.multiple_of(ci * SUBLANES, SUBLANES), SUBLANES), :][0:1, 0:1]
        return s * g_last + _dot(kd_ref[rows, :].T, v_new)

    s_ref[...] = lax.fori_loop(0, n_chunks, state_body, jnp.zeros((DK_A, DV_A), F32))


def _delta_prompt(proj3, ba3, conv_w, alog_row, dtb_row):
    b, n, _ = proj3.shape
    hb = DK_A // LANES

    def col(off):
        return lambda i, h: (i, 0, off // LANES + h * hb)

    def wcol(off):
        return lambda i, h: (0, off // LANES + h * hb)

    seq = lambda off: pl.BlockSpec((None, n, LANES), col(off))
    wsp = lambda off: pl.BlockSpec((CONV_W, LANES), wcol(off))
    row = pl.BlockSpec((1, LANES), lambda i, h: (0, 0))
    return pl.pallas_call(
        _delta_kernel,
        grid=(b, H_A),
        in_specs=[seq(0), seq(QA), seq(2 * QA),
                  pl.BlockSpec((None, n, LANES), lambda i, h: (i, 0, 0)),
                  wsp(0), wsp(QA), wsp(2 * QA), row, row],
        out_specs=(pl.BlockSpec((None, n, DV_A), lambda i, h: (i, 0, h)),
                   pl.BlockSpec((None, None, DK_A, DV_A), lambda i, h: (i, h, 0, 0))),
        out_shape=(jax.ShapeDtypeStruct((b, n, VA), F32),
                   jax.ShapeDtypeStruct((b, H_A, DK_A, DV_A), F32)),
        scratch_shapes=[pltpu.VMEM((n + 8, LANES), F32)] + [pltpu.VMEM((n, LANES), F32)] * 10
                       + [pltpu.VMEM((n // CHUNK * SUBLANES, LANES), F32)],
        compiler_params=_cparams("parallel", "parallel"),
        name="delta_prompt",
    )(proj3, proj3, proj3, ba3, conv_w, conv_w, conv_w, alog_row, dtb_row)


def _ret_kernel(q_ref, k_ref, v_ref, cos_ref, sin_ref, logg_ref, o_ref, s_ref, qs_ref, ks_ref):
    n = q_ref.shape[0]
    c = CHUNK
    cos2, sin2 = cos_ref[...], sin_ref[...]
    qs_ref[...] = _rope(q_ref[...], cos2, sin2)
    ks_ref[...] = _rope(k_ref[...], cos2, sin2) * (DK_B ** -0.5)

    logg = logg_ref[...]
    row = lax.broadcasted_iota(jnp.int32, (c, c), 0)
    col = lax.broadcasted_iota(jnp.int32, (c, c), 1)
    tril = row >= col
    dmat = jnp.where(tril, jnp.exp(jnp.where(tril, (row - col).astype(F32) * logg, 0.0)), 0.0)
    idx = lax.broadcasted_iota(jnp.int32, (c, LANES), 0).astype(F32)
    q_dec = jnp.exp((idx + 1.0) * logg)
    k_dec = jnp.exp((c - 1.0 - idx) * logg)
    g_c = jnp.exp(c * logg[:, 0:1])

    n_chunks = n // c
    group = math.gcd(n_chunks, 4)

    def body(gi, s):
        for j in range(group):
            rows = pl.ds(pl.multiple_of((gi * group + j) * c, c), c)
            qc, kc, vc = qs_ref[rows, :], ks_ref[rows, :], v_ref[rows, :]
            inner = _dot(_dot_nt(qc, kc) * dmat, vc)
            o_ref[rows, :] = inner + _dot(qc * q_dec, s)
            s = s * g_c + _dot((kc * k_dec).T, vc)
        return s

    s_ref[...] = lax.fori_loop(0, n_chunks // group, body, jnp.zeros((DK_B, DV_B), F32))


def _ret_prompt(proj3, cos2, sin2, logg_tab):
    b, n, _ = proj3.shape
    return pl.pallas_call(
        _ret_kernel,
        grid=(b, H_B),
        in_specs=[pl.BlockSpec((None, n, DK_B), lambda i, h: (i, 0, COL_QB // DK_B + h)),
                  pl.BlockSpec((None, n, DK_B), lambda i, h: (i, 0, COL_KB // DK_B + h)),
                  pl.BlockSpec((None, n, DV_B), lambda i, h: (i, 0, COL_VB // DV_B + h)),
                  pl.BlockSpec((n, DK_B), lambda i, h: (0, 0)),
                  pl.BlockSpec((n, DK_B), lambda i, h: (0, 0)),
                  pl.BlockSpec((None, 1, LANES), lambda i, h: (h, 0, 0))],
        out_specs=(pl.BlockSpec((None, n, DV_B), lambda i, h: (i, 0, h)),
                   pl.BlockSpec((None, None, DK_B, DV_B), lambda i, h: (i, h, 0, 0))),
        out_shape=(jax.ShapeDtypeStruct((b, n, VB), F32),
                   jax.ShapeDtypeStruct((b, H_B, DK_B, DV_B), F32)),
        scratch_shapes=[pltpu.VMEM((n, DK_B), F32)] * 2,
        compiler_params=_cparams("parallel", "parallel"),
        name="retention_prompt",
    )(proj3, proj3, proj3, cos2, sin2, logg_tab)


def _sample_prep_kernel(proj_ref, ba_ref, cb_ref, cw_ref, alog_ref, dtb_ref, cos_ref, sin_ref,
                        cnew_ref, va_ref, qat_ref, kat_ref, qbt_ref, kbt_ref, bg_ref):
    x = proj_ref[:, 0:CONV_CH]
    acc = cb_ref[0] * cw_ref[0:1, :]
    for i in range(1, CONV_W - 1):
        acc = acc + cb_ref[i] * cw_ref[i:i + 1, :]
    acc = acc + x * cw_ref[CONV_W - 1:CONV_W, :]
    qkv = _silu(acc)
    for i in range(CONV_W - 2):
        cnew_ref[i] = cb_ref[i + 1]
    cnew_ref[CONV_W - 2] = x

    def l2n(v):
        return v * lax.rsqrt(jnp.sum(v * v, axis=-1, keepdims=True) + EPS)

    cos2, sin2 = cos_ref[0:1, :], sin_ref[0:1, :]
    for h in range(H_A):
        hs = slice(h * DK_A, (h + 1) * DK_A)
        qat_ref[hs, :] = (l2n(qkv[:, h * DK_A:(h + 1) * DK_A]) * (DK_A ** -0.5)).T
        kat_ref[hs, :] = l2n(qkv[:, QA + h * DK_A:QA + (h + 1) * DK_A]).T
    va_ref[...] = qkv[:, 2 * QA:]
    for h in range(H_B):
        hs = slice(h * DK_B, (h + 1) * DK_B)
        qbt_ref[hs, :] = _rope(proj_ref[:, COL_QB + h * DK_B:COL_QB + (h + 1) * DK_B], cos2, sin2).T
        kbt_ref[hs, :] = (_rope(proj_ref[:, COL_KB + h * DK_B:COL_KB + (h + 1) * DK_B], cos2, sin2)
                          * (DK_B ** -0.5)).T
    beta, logd = _decay_terms(ba_ref[...], alog_ref[...], dtb_ref[...])
    lane = lax.broadcasted_iota(jnp.int32, beta.shape, 1)
    bg_ref[...] = jnp.where(lane < H_A, beta, jnp.exp(logd))


def _sample_prep(proj, ba, cb3, conv_w, alog_row, dtb_row, cos2, sin2):
    n = proj.shape[0]
    sd = lambda *s: jax.ShapeDtypeStruct(s, F32)
    return pl.pallas_call(
        _sample_prep_kernel,
        out_shape=(sd(CONV_W - 1, n, CONV_CH), sd(n, VA), sd(QA, n), sd(QA, n), sd(QB, n), sd(QB, n),
                   sd(n, LANES)),
        compiler_params=pltpu.CompilerParams(vmem_limit_bytes=VMEM_LIMIT),
        name="sample_prep",
    )(proj, ba, cb3, conv_w, alog_row, dtb_row, cos2, sin2)


SAMPLE_ROWS = 8


def _sample_state_kernel(sd_ref, sr_ref, va_ref, vb_ref, bg_ref, qat_ref, kat_ref, qbt_ref, kbt_ref,
                         logg_ref, sdn_ref, srn_ref, oa_ref, ob_ref):
    base = pl.program_id(0) * SAMPLE_ROWS
    nseq = qat_ref.shape[1]
    lane = lax.broadcasted_iota(jnp.int32, (DK_A, nseq), 1)

    def column(t_ref, h, seq):
        blk = t_ref[h * DK_A:(h + 1) * DK_A, :]
        return jnp.sum(jnp.where(lane == seq, blk, 0.0), axis=1, keepdims=True)

    for j in range(SAMPLE_ROWS):
        seq = base + j
        bg_row = bg_ref[j:j + 1, :]
        for h in range(H_A):
            kcol, qcol = column(kat_ref, h, seq), column(qat_ref, h, seq)
            beta = _lane_pick(bg_row, h)
            eg = _lane_pick(bg_row, H_A + h)
            s0 = sd_ref[j, h]
            v = va_ref[j:j + 1, h * DV_A:(h + 1) * DV_A]
            ks = jnp.sum(kcol * s0, axis=0, keepdims=True)
            v_new = beta * v - (beta * eg) * ks
            s1 = s0 * eg + kcol * v_new
            sdn_ref[j, h] = s1
            oa_ref[j:j + 1, h * DV_A:(h + 1) * DV_A] = jnp.sum(qcol * s1, axis=0, keepdims=True)
        for h in range(H_B):
            kcol, qcol = column(kbt_ref, h, seq), column(qbt_ref, h, seq)
            gamma = jnp.exp(logg_ref[h][:, 0:1])
            v = vb_ref[j:j + 1, h * DV_B:(h + 1) * DV_B]
            s1 = sr_ref[j, h] * gamma + kcol * v
            srn_ref[j, h] = s1
            ob_ref[j:j + 1, h * DV_B:(h + 1) * DV_B] = jnp.sum(qcol * s1, axis=0, keepdims=True)


def _sample_state(sd, sr, va, proj, bg, qat, kat, qbt, kbt, logg_tab):
    n = sd.shape[0]
    r = SAMPLE_ROWS
    full = lambda a: pl.BlockSpec(a.shape, lambda i: (0,) * a.ndim)
    return pl.pallas_call(
        _sample_state_kernel,
        grid=(n // r,),
        in_specs=[pl.BlockSpec((r, H_A, DK_A, DV_A), lambda i: (i, 0, 0, 0)),
                  pl.BlockSpec((r, H_B, DK_B, DV_B), lambda i: (i, 0, 0, 0)),
                  pl.BlockSpec((r, VA), lambda i: (i, 0)),
                  pl.BlockSpec((r, VB), lambda i: (i, COL_VB // VB)),
                  pl.BlockSpec((r, LANES), lambda i: (i, 0)),
                  full(qat), full(kat), full(qbt), full(kbt), full(logg_tab)],
        out_specs=(pl.BlockSpec((r, H_A, DK_A, DV_A), lambda i: (i, 0, 0, 0)),
                   pl.BlockSpec((r, H_B, DK_B, DV_B), lambda i: (i, 0, 0, 0)),
                   pl.BlockSpec((r, VA), lambda i: (i, 0)),
                   pl.BlockSpec((r, VB), lambda i: (i, 0))),
        out_shape=(jax.ShapeDtypeStruct(sd.shape, F32), jax.ShapeDtypeStruct(sr.shape, F32),
                   jax.ShapeDtypeStruct((n, VA), F32), jax.ShapeDtypeStruct((n, VB), F32)),
        compiler_params=_cparams("parallel"),
        name="sample_state",
    )(sd, sr, va, proj, bg, qat, kat, qbt, kbt, logg_tab)


def _postmix_kernel(oa_ref, ob_ref, ga_ref, gb_ref, sa_ref, sb_ref, x_ref, gna_ref, gnb_ref,
                    wa_ref, wb_ref, wo_ref, nf_ref, wq_ref, keys_ref, h_ref, hnt_ref, st_ref):
    def gated(o_ref, g_ref, gn_ref, heads, dv):
        parts = []
        for h in range(heads):
            o = o_ref[:, h * dv:(h + 1) * dv]
            y = o * lax.rsqrt(jnp.mean(o * o, axis=-1, keepdims=True) + EPS) * gn_ref[...]
            parts.append(y * _silu(g_ref[:, h * dv:(h + 1) * dv]))
        return jnp.concatenate(parts, axis=1)

    br_a = _dot(gated(oa_ref, ga_ref, gna_ref, H_A, DV_A), wa_ref[...])
    br_b = _dot(gated(ob_ref, gb_ref, gnb_ref, H_B, DV_B), wb_ref[...])
    merged = _sigmoid(sa_ref[...]) * br_a + _sigmoid(sb_ref[...]) * br_b
    hres = x_ref[...] + _dot(merged, wo_ref[...])
    h_ref[...] = hres
    hn = hres * lax.rsqrt(jnp.mean(hres * hres, axis=-1, keepdims=True) + EPS) * nf_ref[...]
    hnt_ref[...] = hn.T.astype(BF16)
    q = _dot(hn, wq_ref[...])
    for hp in range(2 * PEER_HEADS):
        st_ref[hp] = _dot_nt(keys_ref[hp], q[:, hp * N_KEYS:(hp + 1) * N_KEYS])


def _postmix(oa, ob, proj, x2, gn_a, gn_b, w_a, w_b, w_o, norm_ffn, w_q, keys):
    t = x2.shape[0]
    tm = min(t, 256)
    full = lambda a: pl.BlockSpec(a.shape, lambda i: (0,) * a.ndim)
    pcol = lambda width, off: pl.BlockSpec((tm, width), lambda i: (i, off // width))
    return pl.pallas_call(
        _postmix_kernel,
        grid=(t // tm,),
        in_specs=[pl.BlockSpec((tm, VA), lambda i: (i, 0)), pl.BlockSpec((tm, VB), lambda i: (i, 0)),
                  pcol(VA, COL_GATE_A), pcol(VB, COL_GATE_B), pcol(D_MODEL, COL_SEL_A),
                  pcol(D_MODEL, COL_SEL_B), pl.BlockSpec((tm, D_MODEL), lambda i: (i, 0)),
                  full(gn_a), full(gn_b), full(w_a), full(w_b), full(w_o), full(norm_ffn), full(w_q),
                  full(keys)],
        out_specs=(pl.BlockSpec((tm, D_MODEL), lambda i: (i, 0)),
                   pl.BlockSpec((D_MODEL, tm), lambda i: (0, i)),
                   pl.BlockSpec((2 * PEER_HEADS, N_KEYS, tm), lambda i: (0, 0, i))),
        out_shape=(jax.ShapeDtypeStruct((t, D_MODEL), F32), jax.ShapeDtypeStruct((D_MODEL, t), BF16),
                   jax.ShapeDtypeStruct((2 * PEER_HEADS, N_KEYS, t), F32)),
        compiler_params=_cparams("parallel"),
        name="postmix",
    )(oa, ob, proj, proj, proj, proj, x2, gn_a, gn_b, w_a, w_b, w_o, norm_ffn, w_q, keys)


A_PER_CHUNK = 2 * SUBLANES
E_CHUNK = A_PER_CHUNK * N_KEYS
PEER_TOKENS = 256

def _batcher_pairs(n):
    pairs = []
    p = 1
    while p < n:
        k = p
        while k >= 1:
            for j in range(k % p, n - k, 2 * k):
                for i in range(min(k, n - j - k)):
                    if (i + j) // (2 * p) == (i + j + k) // (2 * p):
                        pairs.append((i + j, i + j + k))
            k //= 2
        p *= 2
    return pairs


_SORT_TOPK = _batcher_pairs(PEER_TOPK)


def _vmax(x, y):
    if x is None:
        return y
    return x if y is None else jnp.maximum(x, y)


def _exchange(x, y):
    if x is None or y is None:
        return _vmax(x, y), None
    return jnp.maximum(x, y), jnp.minimum(x, y)


def _sort_desc(v):
    v = list(v)
    for i, j in _SORT_TOPK:
        v[i], v[j] = _exchange(v[i], v[j])
    return v


def _merge_top(v, w):
    n = len(v)
    c = [_vmax(v[i], w[n - 1 - i]) for i in range(n)]
    d = n // 2
    while d >= 1:
        for i in range(n):
            if (i & d) == 0:
                c[i], c[i + d] = _exchange(c[i], c[i + d])
        d //= 2
    return c


def _top_scores(x):
    v = _sort_desc([x[i * SUBLANES:(i + 1) * SUBLANES, :] for i in range(N_KEYS // SUBLANES)])
    shift = SUBLANES // 2
    while shift >= 1:
        v = _merge_top(v, [pltpu.roll(t, shift, axis=0) for t in v])
        shift //= 2
    return v


def _peer_select(st_ref, p1_ref, p2_ref, th_ref, tb):
    sub = lax.broadcasted_iota(jnp.int32, (SUBLANES, LANES), 0)
    for tg in range(tb // LANES):
        lanes = slice(tg * LANES, (tg + 1) * LANES)
        top = [[None] * PEER_TOPK for _ in range(2)]
        for h in range(PEER_HEADS):
            for p in range(2):
                v = _top_scores(st_ref[2 * h + p, :, lanes])
                for r in range(PEER_TOPK):
                    top[p][r] = v[r] if h == 0 else jnp.where(sub == h, v[r], top[p][r])
        sums = [[top[0][r1] + top[1][r2] if (r1 + 1) * (r2 + 1) <= PEER_TOPK else None
                 for r2 in range(PEER_TOPK)] for r1 in range(PEER_TOPK)]
        best = sums[0]
        for r1 in range(1, PEER_TOPK):
            best = _merge_top(best, sums[r1])
        z = jnp.ones((SUBLANES, LANES), F32)
        for r in range(1, PEER_TOPK):
            z = z + jnp.exp(best[r] - best[0])
        tau = best[PEER_TOPK - 1]
        theta = []
        for r1 in range(PEER_TOPK):
            th = None
            for r2 in range(PEER_TOPK):
                if sums[r1][r2] is not None:
                    cand = jnp.where(sums[r1][r2] >= tau, top[1][r2], jnp.inf)
                    th = cand if th is None else jnp.minimum(th, cand)
            theta.append(th)
        for h in range(PEER_HEADS):
            row = slice(h, h + 1)
            s1 = st_ref[2 * h, :, lanes]
            th_a = jnp.full(s1.shape, jnp.inf, F32)
            for r1 in range(PEER_TOPK):
                th_a = jnp.where(s1 == top[0][r1][row, :], theta[r1][row, :], th_a)
            th_ref[h, :, lanes] = th_a
            p1_ref[h, :, lanes] = jnp.exp(s1 - top[0][0][row, :]) / z[row, :] * 0.5
            p2_ref[h, :, lanes] = jnp.exp(st_ref[2 * h + 1, :, lanes] - top[1][0][row, :])


def _peer_coefficients(chunk, st_ref, p1_ref, p2_ref, th_ref, act_ref, coef_ref, tb):
    a0 = pl.multiple_of(chunk * A_PER_CHUNK, A_PER_CHUNK)
    for tg in range(tb // LANES):
        lanes = slice(tg * LANES, (tg + 1) * LANES)
        tht = [th_ref[h, pl.ds(a0, A_PER_CHUNK), lanes] for h in range(PEER_HEADS)]
        p1t = [p1_ref[h, pl.ds(a0, A_PER_CHUNK), lanes] for h in range(PEER_HEADS)]
        for r in range(A_PER_CHUNK):
            rows = slice(r * N_KEYS, (r + 1) * N_KEYS)
            gate = jnp.zeros((N_KEYS, LANES), F32)
            for h in range(PEER_HEADS):
                keep = st_ref[2 * h + 1, :, lanes] >= tht[h][r:r + 1, :]
                gate = gate + jnp.where(keep, p2_ref[h, :, lanes], 0.0) * p1t[h][r:r + 1, :]
            coef_ref[rows, lanes] = (gate * _gelu_times2(act_ref[rows, lanes])).astype(BF16)


def _peer_kernel(hnt_ref, st_ref, u0_ref, uo_ref, ue_ref, vte_ref, vto_ref, ft_ref,
                 p1_ref, p2_ref, th_ref, acta_ref, actb_ref, coefa_ref, coefb_ref, acc_ref):
    s = pl.program_id(1)
    tb = hnt_ref.shape[1]
    dot = functools.partial(jnp.dot, preferred_element_type=F32)

    @pl.when(s == 0)
    def _():
        _peer_select(st_ref, p1_ref, p2_ref, th_ref, tb)
        acta_ref[...] = dot(u0_ref[...], hnt_ref[...])
        acc_ref[...] = jnp.zeros_like(acc_ref)

    actb_ref[...] = dot(uo_ref[...], hnt_ref[...])
    _peer_coefficients(2 * s, st_ref, p1_ref, p2_ref, th_ref, acta_ref, coefa_ref, tb)
    acc_ref[...] += dot(vte_ref[...], coefa_ref[...])
    acta_ref[...] = dot(ue_ref[...], hnt_ref[...])
    _peer_coefficients(2 * s + 1, st_ref, p1_ref, p2_ref, th_ref, actb_ref, coefb_ref, tb)
    acc_ref[...] += dot(vto_ref[...], coefb_ref[...])

    @pl.when(s == pl.num_programs(1) - 1)
    def _():
        ft_ref[...] = acc_ref[...]


def _peer(hnt, st, u_bf, vt_bf):
    t = hnt.shape[1]
    tb = min(t, PEER_TOKENS)
    n_chunks = u_bf.shape[0] // E_CHUNK
    steps = n_chunks // 2
    ublk = lambda fn: pl.BlockSpec((E_CHUNK, D_MODEL), fn)
    vblk = lambda fn: pl.BlockSpec((None, D_MODEL, E_CHUNK), fn)
    return pl.pallas_call(
        _peer_kernel,
        grid=(t // tb, steps),
        in_specs=[pl.BlockSpec((D_MODEL, tb), lambda i, s: (0, i)),
                  pl.BlockSpec((2 * PEER_HEADS, N_KEYS, tb), lambda i, s: (0, 0, i)),
                  ublk(lambda i, s: (0, 0)),
                  ublk(lambda i, s: (2 * s + 1, 0)),
                  ublk(lambda i, s: (jnp.minimum(2 * s + 2, n_chunks - 1), 0)),
                  vblk(lambda i, s: (2 * s, 0, 0)),
                  vblk(lambda i, s: (2 * s + 1, 0, 0))],
        out_specs=pl.BlockSpec((D_MODEL, tb), lambda i, s: (0, i)),
        out_shape=jax.ShapeDtypeStruct((D_MODEL, t), F32),
        scratch_shapes=[pltpu.VMEM((PEER_HEADS, N_KEYS, tb), F32),
                        pltpu.VMEM((PEER_HEADS, N_KEYS, tb), F32),
                        pltpu.VMEM((PEER_HEADS, N_KEYS, tb), F32),
                        pltpu.VMEM((E_CHUNK, tb), F32),
                        pltpu.VMEM((E_CHUNK, tb), F32),
                        pltpu.VMEM((E_CHUNK, tb), BF16),
                        pltpu.VMEM((E_CHUNK, tb), BF16),
                        pltpu.VMEM((D_MODEL, tb), F32)],
        compiler_params=_cparams("parallel", "arbitrary"),
        name="peer_dense",
    )(hnt, st, u_bf, u_bf, u_bf, vt_bf, vt_bf)


def _final_kernel(h_ref, ft_ref, g_ref, y_ref):
    hres = h_ref[...] + ft_ref[...].T
    y_ref[...] = hres * lax.rsqrt(jnp.mean(hres * hres, axis=-1, keepdims=True) + EPS) * g_ref[...]


def _final(h, ft, g):
    t = h.shape[0]
    tm = min(t, 512)
    return pl.pallas_call(
        _final_kernel,
        grid=(t // tm,),
        in_specs=[pl.BlockSpec((tm, D_MODEL), lambda i: (i, 0)),
                  pl.BlockSpec((D_MODEL, tm), lambda i: (0, i)),
                  pl.BlockSpec((1, D_MODEL), lambda i: (0, 0))],
        out_specs=pl.BlockSpec((tm, D_MODEL), lambda i: (i, 0)),
        out_shape=jax.ShapeDtypeStruct((t, D_MODEL), F32),
        compiler_params=_cparams("parallel"),
        name="final_norm",
    )(h, ft, g)


def _pack_weights(norm_mix, w_in, conv_w, a_log, dt_bias, gn_a, gn_b, w_br_a, w_br_b, w_out,
                  norm_ffn, peer_wq, peer_keys, peer_u, peer_v, norm_final):
    w = w_in[0]
    n_small = 2 * H_A
    c0 = CONV_CH + VA
    w_main = jnp.concatenate([w[:, :c0], w[:, c0 + n_small:]], axis=1).astype(BF16)
    w_ba = jnp.pad(w[:, c0:c0 + n_small], ((0, 0), (0, LANES - n_small)))
    wb_hi = w_ba.astype(BF16)
    wb_lo = (w_ba - wb_hi.astype(F32)).astype(BF16)
    pad_row = lambda v: jnp.pad(v.reshape(1, H_A), ((0, 0), (H_A, LANES - 2 * H_A)))
    logg = jnp.log1p(-(2.0 ** (-5.0 - jnp.arange(H_B, dtype=F32))))
    return dict(
        norm_mix=norm_mix[0].reshape(1, D_MODEL), w_main=w_main, wb_hi=wb_hi, wb_lo=wb_lo,
        conv_w=conv_w[0], alog_row=pad_row(a_log[0]), dtb_row=pad_row(dt_bias[0]),
        logg_tab=jnp.broadcast_to(logg[:, None, None], (H_B, 1, LANES)),
        gn_a=gn_a[0].reshape(1, DV_A), gn_b=gn_b[0].reshape(1, DV_B),
        w_a=w_br_a[0].astype(BF16), w_b=w_br_b[0].astype(BF16), w_o=w_out[0].astype(BF16),
        norm_ffn=norm_ffn[0].reshape(1, D_MODEL), w_q=peer_wq[0].astype(BF16),
        keys=peer_keys[0].reshape(2 * PEER_HEADS, N_KEYS, N_KEYS).astype(BF16),
        u_bf=peer_u[0].astype(BF16),
        vt_bf=jnp.transpose(peer_v[0].astype(BF16).reshape(-1, E_CHUNK, D_MODEL), (0, 2, 1)),
        norm_final=norm_final.reshape(1, D_MODEL))


def _channel_mix(oa, ob, proj, x2, p):
    h, hnt, st = _postmix(oa, ob, proj, x2, p["gn_a"], p["gn_b"], p["w_a"], p["w_b"], p["w_o"],
                          p["norm_ffn"], p["w_q"], p["keys"])
    ft = _peer(hnt, st, p["u_bf"], p["vt_bf"])
    return _final(h, ft, p["norm_final"])


def _prompt_group(x, p):
    b, n, _ = x.shape
    x2 = x.reshape(b * n, D_MODEL)
    proj, ba = _inproj(x2, p["norm_mix"], p["w_main"], p["wb_hi"], p["wb_lo"])
    proj3 = proj.reshape(b, n, N_MAIN)
    oa, s_delta = _delta_prompt(proj3, ba.reshape(b, n, LANES), p["conv_w"], p["alog_row"], p["dtb_row"])
    cos2, sin2 = _rope_tables(n, 0)
    ob, s_ret = _ret_prompt(proj3, cos2, sin2, p["logg_tab"])
    y = _channel_mix(oa.reshape(b * n, VA), ob.reshape(b * n, VB), proj, x2, p)
    conv_new = proj3[:, n - (CONV_W - 1):, :CONV_CH]
    return y.reshape(b, n, D_MODEL), conv_new[None], s_delta[None], s_ret[None]


def _sample_group(x, conv_buf, s_delta, s_ret, p):
    n = x.shape[0]
    x2 = x.reshape(n, D_MODEL)
    proj, ba = _inproj(x2, p["norm_mix"], p["w_main"], p["wb_hi"], p["wb_lo"])
    cos2, sin2 = _rope_tables(8, PAST_LEN)
    cb3 = jnp.transpose(conv_buf, (1, 0, 2))
    cnew, va, qat, kat, qbt, kbt, bg = _sample_prep(proj, ba, cb3, p["conv_w"], p["alog_row"],
                                                    p["dtb_row"], cos2, sin2)
    sd_new, sr_new, oa, ob = _sample_state(s_delta, s_ret, va, proj, bg, qat, kat, qbt, kbt,
                                           p["logg_tab"])
    y = _channel_mix(oa, ob, proj, x2, p)
    return (y.reshape(n, 1, D_MODEL), jnp.transpose(cnew, (1, 0, 2))[None], sd_new[None], sr_new[None])


def kernel(x_prompt, x_sample, state_conv_a, state_delta, state_ret, norm_mix, w_in, conv_w, a_log, dt_bias, gn_a, gn_b, w_br_a, w_br_b, w_out, norm_ffn, peer_wq, peer_keys, peer_u, peer_v, norm_final):
    assert w_in.shape[0] == 1 and x_sample.shape[1] == 1
    p = _pack_weights(norm_mix, w_in, conv_w, a_log, dt_bias, gn_a, gn_b, w_br_a, w_br_b, w_out,
                      norm_ffn, peer_wq, peer_keys, peer_u, peer_v, norm_final)
    y_p, conv_p, delta_p, ret_p = _prompt_group(x_prompt, p)
    y_s, conv_s, delta_s, ret_s = _sample_group(x_sample, state_conv_a[0], state_delta[0],
                                                state_ret[0], p)
    return (y_p, y_s, conv_p, delta_p, ret_p, conv_s, delta_s, ret_s)
```

```python
import functools
import math

import jax
import jax.numpy as jnp
from jax import lax
from jax.experimental import pallas as pl
from jax.experimental.pallas import tpu as pltpu

F32 = jnp.float32
BF16 = jnp.bfloat16

EPS = 1e-6
D_MODEL = 1024
H_A, DK_A, DV_A, CONV_W = 4, 128, 128, 4
H_B, DK_B, DV_B = 4, 128, 256
ROPE_BASE = 10000.0
PAST_LEN = 16384
N_KEYS = 128
PEER_HEADS = 8
PEER_TOPK = 16
QA, VA, QB, VB = H_A * DK_A, H_A * DV_A, H_B * DK_B, H_B * DV_B
CONV_CH = 2 * QA + VA

COL_GATE_A, COL_QB, COL_KB, COL_VB = 1536, 2048, 2560, 3072
COL_GATE_B, COL_SEL_A, COL_SEL_B, N_MAIN = 4096, 5120, 6144, 7168
LANES = 128
SUBLANES = 8
CHUNK = 128
VMEM_LIMIT = 56 * 1024 * 1024


def _cparams(*sem):
    return pltpu.CompilerParams(dimension_semantics=sem, vmem_limit_bytes=VMEM_LIMIT)


def _dot(a, b):
    return jnp.dot(a.astype(BF16), b.astype(BF16), preferred_element_type=F32)


def _dot_nt(a, b):
    return lax.dot_general(a.astype(BF16), b.astype(BF16), (((1,), (1,)), ((), ())),
                           preferred_element_type=F32)


def _split2(a):
    hi = a.astype(BF16)
    return hi, (a - hi.astype(F32)).astype(BF16)


def _dot3(a, b):
    ah, al = _split2(a)
    bh, bl = _split2(b)
    if a.ndim == 3:
        dims = (((2,), (1,)), ((0,), (0,)))
        d = lambda x, y: lax.dot_general(x, y, dims, preferred_element_type=F32)
    else:
        d = functools.partial(jnp.dot, preferred_element_type=F32)
    return d(ah, bh) + d(al, bh) + d(ah, bl)


def _sigmoid(x):
    return 1.0 / (1.0 + jnp.exp(-x))


def _silu(x):
    return x * _sigmoid(x)


def _softplus(x):
    return jnp.maximum(x, 0.0) + jnp.log1p(jnp.exp(-jnp.abs(x)))


def _gelu_times2(x):
    return x * (1.0 + lax.erf(x * (1.0 / math.sqrt(2.0))))


def _rope_kernel(cos_ref, sin_ref, *, pos0):
    shape = cos_ref.shape
    half = shape[1] // 2
    lane = lax.broadcasted_iota(jnp.int32, shape, 1)
    row = lax.broadcasted_iota(jnp.int32, shape, 0)
    j = jnp.where(lane >= half, lane - half, lane).astype(F32)
    inv = jnp.exp(j * (-math.log(ROPE_BASE) / half))
    ang = (row + pos0).astype(F32) * inv
    s = jnp.sin(ang)
    cos_ref[...] = jnp.cos(ang)
    sin_ref[...] = jnp.where(lane >= half, s, -s)


def _rope_tables(rows, pos0):
    sds = jax.ShapeDtypeStruct((rows, DK_B), F32)
    return pl.pallas_call(functools.partial(_rope_kernel, pos0=pos0), out_shape=(sds, sds),
                          name="rope_tables")()


def _rope(x, cos2, sin2):
    return x * cos2 + pltpu.roll(x, DK_B // 2, axis=1) * sin2


def _inproj_kernel(x_ref, g_ref, w_ref, wbh_ref, wbl_ref, proj_ref, ba_ref, xn_ref):
    @pl.when(pl.program_id(1) == 0)
    def _():
        x = x_ref[...]
        y = x * lax.rsqrt(jnp.mean(x * x, axis=-1, keepdims=True) + EPS) * g_ref[...]
        yh, yl = _split2(y)
        xn_ref[...] = yh
        d = functools.partial(jnp.dot, preferred_element_type=F32)
        ba_ref[...] = d(yh, wbh_ref[...]) + d(yl, wbh_ref[...]) + d(yh, wbl_ref[...])

    proj_ref[...] = jnp.dot(xn_ref[...], w_ref[...], preferred_element_type=F32)


def _inproj(x2, g, w_main, wb_hi, wb_lo):
    t = x2.shape[0]
    tm = min(t, 1024)
    tn = 1024
    return pl.pallas_call(
        _inproj_kernel,
        grid=(t // tm, N_MAIN // tn),
        in_specs=[pl.BlockSpec((tm, D_MODEL), lambda i, j: (i, 0)),
                  pl.BlockSpec((1, D_MODEL), lambda i, j: (0, 0)),
                  pl.BlockSpec((D_MODEL, tn), lambda i, j: (0, j)),
                  pl.BlockSpec((D_MODEL, LANES), lambda i, j: (0, 0)),
                  pl.BlockSpec((D_MODEL, LANES), lambda i, j: (0, 0))],
        out_specs=(pl.BlockSpec((tm, tn), lambda i, j: (i, j)),
                   pl.BlockSpec((tm, LANES), lambda i, j: (i, 0))),
        out_shape=(jax.ShapeDtypeStruct((t, N_MAIN), F32), jax.ShapeDtypeStruct((t, LANES), F32)),
        scratch_shapes=[pltpu.VMEM((tm, D_MODEL), BF16)],
        compiler_params=_cparams("parallel", "arbitrary"),
        name="rms_inproj",
    )(x2, g, w_main, wb_hi, wb_lo)


def _lane_pick(x, idx):
    lane = lax.broadcasted_iota(jnp.int32, x.shape, 1)
    return jnp.sum(jnp.where(lane == idx, x, 0.0), axis=1, keepdims=True)


def _decay_terms(ba, alog_row, dtb_row):
    beta = _sigmoid(ba)
    logd = -jnp.exp(alog_row) * _softplus(ba + dtb_row)
    return beta, logd


def _unit_lower_inverse(lm, masks):
    n = lm.shape[-1]
    row = lax.broadcasted_iota(jnp.int32, (n, n), 0)
    col = lax.broadcasted_iota(jnp.int32, (n, n), 1)
    t = jnp.where(row == col, 1.0, 0.0) - lm * masks[0]
    for m in masks[1:]:
        t = t - _dot3(_dot3(t, lm * m), t)
    return t


def _doubling_masks(n):
    row = lax.broadcasted_iota(jnp.int32, (n, n), 0)
    col = lax.broadcasted_iota(jnp.int32, (n, n), 1)
    masks = []
    lvl = 0
    while (1 << lvl) < n:
        same = (row >> (lvl + 1)) == (col >> (lvl + 1))
        lower = ((row >> lvl) & 1) == 1
        left = ((col >> lvl) & 1) == 0
        masks.append(jnp.where(same & lower & left, 1.0, 0.0))
        lvl += 1
    return masks


def _delta_kernel(q_ref, k_ref, v_ref, ba_ref, wq_ref, wk_ref, wv_ref, alog_ref, dtb_ref,
                  o_ref, s_ref, pad_ref, qs_ref, ks_ref, vs_ref, bt_ref, ld_ref,
                  u_ref, w_ref, qk_ref, qg_ref, kd_ref, gl_ref):
    h = pl.program_id(1)
    n = q_ref.shape[0]
    c = CHUNK

    def conv_silu(x_ref, w_ref):
        pad_ref[0:8, :] = jnp.zeros((8, LANES), F32)
        pad_ref[8:8 + n, :] = x_ref[...]
        base = 8 - (CONV_W - 1)
        acc = pad_ref[base:base + n, :] * w_ref[0:1, :]
        for i in range(1, CONV_W):
            acc = acc + pad_ref[base + i:base + i + n, :] * w_ref[i:i + 1, :]
        return _silu(acc)

    def l2n(x):
        return x * lax.rsqrt(jnp.sum(x * x, axis=-1, keepdims=True) + EPS)

    qs_ref[...] = l2n(conv_silu(q_ref, wq_ref)) * (DK_A ** -0.5)
    ks_ref[...] = l2n(conv_silu(k_ref, wk_ref))
    vs_ref[...] = conv_silu(v_ref, wv_ref)
    beta, logd = _decay_terms(ba_ref[...], alog_ref[...], dtb_ref[...])
    bt_ref[...] = beta
    ld_ref[...] = logd

    row = lax.broadcasted_iota(jnp.int32, (c, c), 0)
    col = lax.broadcasted_iota(jnp.int32, (c, c), 1)
    tril = row >= col
    strict = row > col
    tril_f = jnp.where(tril, 1.0, 0.0).astype(BF16)
    masks = _doubling_masks(c)

    def chunk_terms(ci):
        rows = pl.ds(pl.multiple_of(ci * c, c), c)
        qc, kc, vc = qs_ref[rows, :], ks_ref[rows, :], vs_ref[rows, :]
        bcol = _lane_pick(bt_ref[rows, :], h)
        ldc = ld_ref[rows, :]
        l1 = ldc.astype(BF16)
        r1 = ldc - l1.astype(F32)
        l2 = r1.astype(BF16)
        l3 = (r1 - l2.astype(F32)).astype(BF16)
        d = functools.partial(jnp.dot, preferred_element_type=F32)
        gcol = _lane_pick(d(tril_f, l1) + d(tril_f, l2) + d(tril_f, l3), H_A + h)
        gmat = jnp.broadcast_to(gcol, (c, c))
        diff = gmat - gmat.T
        decay = jnp.where(tril, jnp.exp(jnp.where(tril, diff, 0.0)), 0.0)
        kb = kc * bcol
        lm = jnp.where(strict, _dot_nt(kb, kc) * decay, 0.0)
        eg = jnp.exp(gcol)
        glast = gcol[c - 1:c, :]
        return (lm, vc * bcol, kb * eg, _dot_nt(qc, kc) * decay, qc * eg,
                kc * jnp.exp(glast - gcol), jnp.broadcast_to(jnp.exp(glast), (SUBLANES, LANES)))

    n_chunks = n // c
    group = math.gcd(n_chunks, 8)

    def group_body(gi, carry):
        terms = [chunk_terms(gi * group + j) for j in range(group)]
        t = _unit_lower_inverse(jnp.stack([tm[0] for tm in terms]), masks)
        for j, (_, vb, kbg, qk, qg, kd, gl) in enumerate(terms):
            ci = gi * group + j
            rows = pl.ds(pl.multiple_of(ci * c, c), c)
            u_ref[rows, :] = _dot(t[j], vb)
            w_ref[rows, :] = _dot(t[j], kbg)
            qk_ref[rows, :] = qk
            qg_ref[rows, :] = qg
            kd_ref[rows, :] = kd
            gl_ref[pl.ds(pl.multiple_of(ci * SUBLANES, SUBLANES), SUBLANES), :] = gl
        return carry

    lax.fori_loop(0, n_chunks // group, group_body, 0)

    def state_body(ci, s):
        rows = pl.ds(pl.multiple_of(ci * c, c), c)
        v_new = u_ref[rows, :] - _dot(w_ref[rows, :], s)
        o_ref[rows, :] = _dot(qg_ref[rows, :], s) + _dot(qk_ref[rows, :], v_new)
        g_last = gl_ref[pl.ds(pl.multiple_of(ci * SUBLANES, SUBLANES), SUBLANES), :][0:1, 0:1]
        return s * g_last + _dot(kd_ref[rows, :].T, v_new)

    s_ref[...] = lax.fori_loop(0, n_chunks, state_body, jnp.zeros((DK_A, DV_A), F32))


def _delta_prompt(proj3, ba3, conv_w, alog_row, dtb_row):
    b, n, _ = proj3.shape
    hb = DK_A // LANES

    def col(off):
        return lambda i, h: (i, 0, off // LANES + h * hb)

    def wcol(off):
        return lambda i, h: (0, off // LANES + h * hb)

    seq = lambda off: pl.BlockSpec((None, n, LANES), col(off))
    wsp = lambda off: pl.BlockSpec((CONV_W, LANES), wcol(off))
    row = pl.BlockSpec((1, LANES), lambda i, h: (0, 0))
    return pl.pallas_call(
        _delta_kernel,
        grid=(b, H_A),
        in_specs=[seq(0), seq(QA), seq(2 * QA),
                  pl.BlockSpec((None, n, LANES), lambda i, h: (i, 0, 0)),
                  wsp(0), wsp(QA), wsp(2 * QA), row, row],
        out_specs=(pl.BlockSpec((None, n, DV_A), lambda i, h: (i, 0, h)),
                   pl.BlockSpec((None, None, DK_A, DV_A), lambda i, h: (i, h, 0, 0))),
        out_shape=(jax.ShapeDtypeStruct((b, n, VA), F32),
                   jax.ShapeDtypeStruct((b, H_A, DK_A, DV_A), F32)),
        scratch_shapes=[pltpu.VMEM((n + 8, LANES), F32)] + [pltpu.VMEM((n, LANES), F32)] * 10
                       + [pltpu.VMEM((n // CHUNK * SUBLANES, LANES), F32)],
        compiler_params=_cparams("parallel", "parallel"),
        name="delta_prompt",
    )(proj3, proj3, proj3, ba3, conv_w, conv_w, conv_w, alog_row, dtb_row)


def _ret_kernel(q_ref, k_ref, v_ref, cos_ref, sin_ref, logg_ref, o_ref, s_ref, qs_ref, ks_ref):
    n = q_ref.shape[0]
    c = CHUNK
    cos2, sin2 = cos_ref[...], sin_ref[...]
    qs_ref[...] = _rope(q_ref[...], cos2, sin2)
    ks_ref[...] = _rope(k_ref[...], cos2, sin2) * (DK_B ** -0.5)

    logg = logg_ref[...]
    row = lax.broadcasted_iota(jnp.int32, (c, c), 0)
    col = lax.broadcasted_iota(jnp.int32, (c, c), 1)
    tril = row >= col
    dmat = jnp.where(tril, jnp.exp(jnp.where(tril, (row - col).astype(F32) * logg, 0.0)), 0.0)
    idx = lax.broadcasted_iota(jnp.int32, (c, LANES), 0).astype(F32)
    q_dec = jnp.exp((idx + 1.0) * logg)
    k_dec = jnp.exp((c - 1.0 - idx) * logg)
    g_c = jnp.exp(c * logg[:, 0:1])

    n_chunks = n // c
    group = math.gcd(n_chunks, 4)

    def body(gi, s):
        for j in range(group):
            rows = pl.ds(pl.multiple_of((gi * group + j) * c, c), c)
            qc, kc, vc = qs_ref[rows, :], ks_ref[rows, :], v_ref[rows, :]
            inner = _dot(_dot_nt(qc, kc) * dmat, vc)
            o_ref[rows, :] = inner + _dot(qc * q_dec, s)
            s = s * g_c + _dot((kc * k_dec).T, vc)
        return s

    s_ref[...] = lax.fori_loop(0, n_chunks // group, body, jnp.zeros((DK_B, DV_B), F32))


def _ret_prompt(proj3, cos2, sin2, logg_tab):
    b, n, _ = proj3.shape
    return pl.pallas_call(
        _ret_kernel,
        grid=(b, H_B),
        in_specs=[pl.BlockSpec((None, n, DK_B), lambda i, h: (i, 0, COL_QB // DK_B + h)),
                  pl.BlockSpec((None, n, DK_B), lambda i, h: (i, 0, COL_KB // DK_B + h)),
                  pl.BlockSpec((None, n, DV_B), lambda i, h: (i, 0, COL_VB // DV_B + h)),
                  pl.BlockSpec((n, DK_B), lambda i, h: (0, 0)),
                  pl.BlockSpec((n, DK_B), lambda i, h: (0, 0)),
                  pl.BlockSpec((None, 1, LANES), lambda i, h: (h, 0, 0))],
        out_specs=(pl.BlockSpec((None, n, DV_B), lambda i, h: (i, 0, h)),
                   pl.BlockSpec((None, None, DK_B, DV_B), lambda i, h: (i, h, 0, 0))),
        out_shape=(jax.ShapeDtypeStruct((b, n, VB), F32),
                   jax.ShapeDtypeStruct((b, H_B, DK_B, DV_B), F32)),
        scratch_shapes=[pltpu.VMEM((n, DK_B), F32)] * 2,
        compiler_params=_cparams("parallel", "parallel"),
        name="retention_prompt",
    )(proj3, proj3, proj3, cos2, sin2, logg_tab)


def _sample_prep_kernel(proj_ref, ba_ref, cb_ref, cw_ref, alog_ref, dtb_ref, cos_ref, sin_ref,
                        cnew_ref, va_ref, qat_ref, kat_ref, qbt_ref, kbt_ref, bg_ref):
    x = proj_ref[:, 0:CONV_CH]
    acc = cb_ref[0] * cw_ref[0:1, :]
    for i in range(1, CONV_W - 1):
        acc = acc + cb_ref[i] * cw_ref[i:i + 1, :]
    acc = acc + x * cw_ref[CONV_W - 1:CONV_W, :]
    qkv = _silu(acc)
    for i in range(CONV_W - 2):
        cnew_ref[i] = cb_ref[i + 1]
    cnew_ref[CONV_W - 2] = x

    def l2n(v):
        return v * lax.rsqrt(jnp.sum(v * v, axis=-1, keepdims=True) + EPS)

    cos2, sin2 = cos_ref[0:1, :], sin_ref[0:1, :]
    for h in range(H_A):
        hs = slice(h * DK_A, (h + 1) * DK_A)
        qat_ref[hs, :] = (l2n(qkv[:, h * DK_A:(h + 1) * DK_A]) * (DK_A ** -0.5)).T
        kat_ref[hs, :] = l2n(qkv[:, QA + h * DK_A:QA + (h + 1) * DK_A]).T
    va_ref[...] = qkv[:, 2 * QA:]
    for h in range(H_B):
        hs = slice(h * DK_B, (h + 1) * DK_B)
        qbt_ref[hs, :] = _rope(proj_ref[:, COL_QB + h * DK_B:COL_QB + (h + 1) * DK_B], cos2, sin2).T
        kbt_ref[hs, :] = (_rope(proj_ref[:, COL_KB + h * DK_B:COL_KB + (h + 1) * DK_B], cos2, sin2)
                          * (DK_B ** -0.5)).T
    beta, logd = _decay_terms(ba_ref[...], alog_ref[...], dtb_ref[...])
    lane = lax.broadcasted_iota(jnp.int32, beta.shape, 1)
    bg_ref[...] = jnp.where(lane < H_A, beta, jnp.exp(logd))


def _sample_prep(proj, ba, cb3, conv_w, alog_row, dtb_row, cos2, sin2):
    n = proj.shape[0]
    sd = lambda *s: jax.ShapeDtypeStruct(s, F32)
    return pl.pallas_call(
        _sample_prep_kernel,
        out_shape=(sd(CONV_W - 1, n, CONV_CH), sd(n, VA), sd(QA, n), sd(QA, n), sd(QB, n), sd(QB, n),
                   sd(n, LANES)),
        compiler_params=pltpu.CompilerParams(vmem_limit_bytes=VMEM_LIMIT),
        name="sample_prep",
    )(proj, ba, cb3, conv_w, alog_row, dtb_row, cos2, sin2)


SAMPLE_ROWS = 8


def _sample_state_kernel(sd_ref, sr_ref, va_ref, vb_ref, bg_ref, qat_ref, kat_ref, qbt_ref, kbt_ref,
                         logg_ref, sdn_ref, srn_ref, oa_ref, ob_ref):
    base = pl.program_id(0) * SAMPLE_ROWS
    nseq = qat_ref.shape[1]
    lane = lax.broadcasted_iota(jnp.int32, (DK_A, nseq), 1)

    def column(t_ref, h, seq):
        blk = t_ref[h * DK_A:(h + 1) * DK_A, :]
        return jnp.sum(jnp.where(lane == seq, blk, 0.0), axis=1, keepdims=True)

    for j in range(SAMPLE_ROWS):
        seq = base + j
        bg_row = bg_ref[j:j + 1, :]
        for h in range(H_A):
            kcol, qcol = column(kat_ref, h, seq), column(qat_ref, h, seq)
            beta = _lane_pick(bg_row, h)
            eg = _lane_pick(bg_row, H_A + h)
            s0 = sd_ref[j, h]
            v = va_ref[j:j + 1, h * DV_A:(h + 1) * DV_A]
            ks = jnp.sum(kcol * s0, axis=0, keepdims=True)
            v_new = beta * v - (beta * eg) * ks
            s1 = s0 * eg + kcol * v_new
            sdn_ref[j, h] = s1
            oa_ref[j:j + 1, h * DV_A:(h + 1) * DV_A] = jnp.sum(qcol * s1, axis=0, keepdims=True)
        for h in range(H_B):
            kcol, qcol = column(kbt_ref, h, seq), column(qbt_ref, h, seq)
            gamma = jnp.exp(logg_ref[h][:, 0:1])
            v = vb_ref[j:j + 1, h * DV_B:(h + 1) * DV_B]
            s1 = sr_ref[j, h] * gamma + kcol * v
            srn_ref[j, h] = s1
            ob_ref[j:j + 1, h * DV_B:(h + 1) * DV_B] = jnp.sum(qcol * s1, axis=0, keepdims=True)


def _sample_state(sd, sr, va, proj, bg, qat, kat, qbt, kbt, logg_tab):
    n = sd.shape[0]
    r = SAMPLE_ROWS
    full = lambda a: pl.BlockSpec(a.shape, lambda i: (0,) * a.ndim)
    return pl.pallas_call(
        _sample_state_kernel,
        grid=(n // r,),
        in_specs=[pl.BlockSpec((r, H_A, DK_A, DV_A), lambda i: (i, 0, 0, 0)),
                  pl.BlockSpec((r, H_B, DK_B, DV_B), lambda i: (i, 0, 0, 0)),
                  pl.BlockSpec((r, VA), lambda i: (i, 0)),
                  pl.BlockSpec((r, VB), lambda i: (i, COL_VB // VB)),
                  pl.BlockSpec((r, LANES), lambda i: (i, 0)),
                  full(qat), full(kat), full(qbt), full(kbt), full(logg_tab)],
        out_specs=(pl.BlockSpec((r, H_A, DK_A, DV_A), lambda i: (i, 0, 0, 0)),
                   pl.BlockSpec((r, H_B, DK_B, DV_B), lambda i: (i, 0, 0, 0)),
                   pl.BlockSpec((r, VA), lambda i: (i, 0)),
                   pl.BlockSpec((r, VB), lambda i: (i, 0))),
        out_shape=(jax.ShapeDtypeStruct(sd.shape, F32), jax.ShapeDtypeStruct(sr.shape, F32),
                   jax.ShapeDtypeStruct((n, VA), F32), jax.ShapeDtypeStruct((n, VB), F32)),
        compiler_params=_cparams("parallel"),
        name="sample_state",
    )(sd, sr, va, proj, bg, qat, kat, qbt, kbt, logg_tab)


def _postmix_kernel(oa_ref, ob_ref, ga_ref, gb_ref, sa_ref, sb_ref, x_ref, gna_ref, gnb_ref,
                    wa_ref, wb_ref, wo_ref, nf_ref, wq_ref, keys_ref, h_ref, hnt_ref, st_ref):
    def gated(o_ref, g_ref, gn_ref, heads, dv):
        parts = []
        for h in range(heads):
            o = o_ref[:, h * dv:(h + 1) * dv]
            y = o * lax.rsqrt(jnp.mean(o * o, axis=-1, keepdims=True) + EPS) * gn_ref[...]
            parts.append(y * _silu(g_ref[:, h * dv:(h + 1) * dv]))
        return jnp.concatenate(parts, axis=1)

    br_a = _dot(gated(oa_ref, ga_ref, gna_ref, H_A, DV_A), wa_ref[...])
    br_b = _dot(gated(ob_ref, gb_ref, gnb_ref, H_B, DV_B), wb_ref[...])
    merged = _sigmoid(sa_ref[...]) * br_a + _sigmoid(sb_ref[...]) * br_b
    hres = x_ref[...] + _dot(merged, wo_ref[...])
    h_ref[...] = hres
    hn = hres * lax.rsqrt(jnp.mean(hres * hres, axis=-1, keepdims=True) + EPS) * nf_ref[...]
    hnt_ref[...] = hn.T.astype(BF16)
    q = _dot(hn, wq_ref[...])
    for hp in range(2 * PEER_HEADS):
        st_ref[hp] = _dot_nt(keys_ref[hp], q[:, hp * N_KEYS:(hp + 1) * N_KEYS])


def _postmix(oa, ob, proj, x2, gn_a, gn_b, w_a, w_b, w_o, norm_ffn, w_q, keys):
    t = x2.shape[0]
    tm = min(t, 256)
    full = lambda a: pl.BlockSpec(a.shape, lambda i: (0,) * a.ndim)
    pcol = lambda width, off: pl.BlockSpec((tm, width), lambda i: (i, off // width))
    return pl.pallas_call(
        _postmix_kernel,
        grid=(t // tm,),
        in_specs=[pl.BlockSpec((tm, VA), lambda i: (i, 0)), pl.BlockSpec((tm, VB), lambda i: (i, 0)),
                  pcol(VA, COL_GATE_A), pcol(VB, COL_GATE_B), pcol(D_MODEL, COL_SEL_A),
                  pcol(D_MODEL, COL_SEL_B), pl.BlockSpec((tm, D_MODEL), lambda i: (i, 0)),
                  full(gn_a), full(gn_b), full(w_a), full(w_b), full(w_o), full(norm_ffn), full(w_q),
                  full(keys)],
        out_specs=(pl.BlockSpec((tm, D_MODEL), lambda i: (i, 0)),
                   pl.BlockSpec((D_MODEL, tm), lambda i: (0, i)),
                   pl.BlockSpec((2 * PEER_HEADS, N_KEYS, tm), lambda i: (0, 0, i))),
        out_shape=(jax.ShapeDtypeStruct((t, D_MODEL), F32), jax.ShapeDtypeStruct((D_MODEL, t), BF16),
                   jax.ShapeDtypeStruct((2 * PEER_HEADS, N_KEYS, t), F32)),
        compiler_params=_cparams("parallel"),
        name="postmix",
    )(oa, ob, proj, proj, proj, proj, x2, gn_a, gn_b, w_a, w_b, w_o, norm_ffn, w_q, keys)


A_PER_CHUNK = 2 * SUBLANES
E_CHUNK = A_PER_CHUNK * N_KEYS
PEER_TOKENS = 256

def _batcher_pairs(n):
    pairs = []
    p = 1
    while p < n:
        k = p
        while k >= 1:
            for j in range(k % p, n - k, 2 * k):
                for i in range(min(k, n - j - k)):
                    if (i + j) // (2 * p) == (i + j + k) // (2 * p):
                        pairs.append((i + j, i + j + k))
            k //= 2
        p *= 2
    return pairs


_SORT_TOPK = _batcher_pairs(PEER_TOPK)


def _vmax(x, y):
    if x is None:
        return y
    return x if y is None else jnp.maximum(x, y)


def _exchange(x, y):
    if x is None or y is None:
        return _vmax(x, y), None
    return jnp.maximum(x, y), jnp.minimum(x, y)


def _sort_desc(v):
    v = list(v)
    for i, j in _SORT_TOPK:
        v[i], v[j] = _exchange(v[i], v[j])
    return v


def _merge_top(v, w):
    n = len(v)
    c = [_vmax(v[i], w[n - 1 - i]) for i in range(n)]
    d = n // 2
    while d >= 1:
        for i in range(n):
            if (i & d) == 0:
                c[i], c[i + d] = _exchange(c[i], c[i + d])
        d //= 2
    return c


def _top_scores(x):
    v = _sort_desc([x[i * SUBLANES:(i + 1) * SUBLANES, :] for i in range(N_KEYS // SUBLANES)])
    shift = SUBLANES // 2
    while shift >= 1:
        v = _merge_top(v, [pltpu.roll(t, shift, axis=0) for t in v])
        shift //= 2
    return v


def _peer_select(st_ref, p1_ref, p2_ref, th_ref, tb):
    sub = lax.broadcasted_iota(jnp.int32, (SUBLANES, LANES), 0)
    for tg in range(tb // LANES):
        lanes = slice(tg * LANES, (tg + 1) * LANES)
        top = [[None] * PEER_TOPK for _ in range(2)]
        for h in range(PEER_HEADS):
            for p in range(2):
                v = _top_scores(st_ref[2 * h + p, :, lanes])
                for r in range(PEER_TOPK):
                    top[p][r] = v[r] if h == 0 else jnp.where(sub == h, v[r], top[p][r])
        sums = [[top[0][r1] + top[1][r2] if (r1 + 1) * (r2 + 1) <= PEER_TOPK else None
                 for r2 in range(PEER_TOPK)] for r1 in range(PEER_TOPK)]
        best = sums[0]
        for r1 in range(1, PEER_TOPK):
            best = _merge_top(best, sums[r1])
        z = jnp.ones((SUBLANES, LANES), F32)
        for r in range(1, PEER_TOPK):
            z = z + jnp.exp(best[r] - best[0])
        tau = best[PEER_TOPK - 1]
        theta = []
        for r1 in range(PEER_TOPK):
            th = None
            for r2 in range(PEER_TOPK):
                if sums[r1][r2] is not None:
                    cand = jnp.where(sums[r1][r2] >= tau, top[1][r2], jnp.inf)
                    th = cand if th is None else jnp.minimum(th, cand)
            theta.append(th)
        for h in range(PEER_HEADS):
            row = slice(h, h + 1)
            s1 = st_ref[2 * h, :, lanes]
            th_a = jnp.full(s1.shape, jnp.inf, F32)
            for r1 in range(PEER_TOPK):
                th_a = jnp.where(s1 == top[0][r1][row, :], theta[r1][row, :], th_a)
            th_ref[h, :, lanes] = th_a
            p1_ref[h, :, lanes] = jnp.exp(s1 - top[0][0][row, :]) / z[row, :] * 0.5
            p2_ref[h, :, lanes] = jnp.exp(st_ref[2 * h + 1, :, lanes] - top[1][0][row, :])


def _peer_coefficients(chunk, st_ref, p1_ref, p2_ref, th_ref, act_ref, coef_ref, tb):
    a0 = pl.multiple_of(chunk * A_PER_CHUNK, A_PER_CHUNK)
    for tg in range(tb // LANES):
        lanes = slice(tg * LANES, (tg + 1) * LANES)
        tht = [th_ref[h, pl.ds(a0, A_PER_CHUNK), lanes] for h in range(PEER_HEADS)]
        p1t = [p1_ref[h, pl.ds(a0, A_PER_CHUNK), lanes] for h in range(PEER_HEADS)]
        for r in range(A_PER_CHUNK):
            rows = slice(r * N_KEYS, (r + 1) * N_KEYS)
            gate = jnp.zeros((N_KEYS, LANES), F32)
            for h in range(PEER_HEADS):
                keep = st_ref[2 * h + 1, :, lanes] >= tht[h][r:r + 1, :]
                gate = gate + jnp.where(keep, p2_ref[h, :, lanes], 0.0) * p1t[h][r:r + 1, :]
            coef_ref[rows, lanes] = (gate * _gelu_times2(act_ref[rows, lanes])).astype(BF16)


def _peer_kernel(hnt_ref, hntn_ref, st_ref, u0_ref, uo_ref, ue_ref, vte_ref, vto_ref, ft_ref,
                 p1_ref, p2_ref, th_ref, acta_ref, actb_ref, coefa_ref, coefb_ref):
    i = pl.program_id(0)
    s = pl.program_id(1)
    last = pl.num_programs(1) - 1
    tb = hnt_ref.shape[1]
    dot = functools.partial(jnp.dot, preferred_element_type=F32)

    @pl.when(s == 0)
    def _():
        _peer_select(st_ref, p1_ref, p2_ref, th_ref, tb)
        ft_ref[...] = jnp.zeros_like(ft_ref)

    @pl.when((s == 0) & (i == 0))
    def _():
        acta_ref[...] = dot(u0_ref[...], hnt_ref[...])

    actb_ref[...] = dot(uo_ref[...], hnt_ref[...])
    _peer_coefficients(2 * s, st_ref, p1_ref, p2_ref, th_ref, acta_ref, coefa_ref, tb)
    ft_ref[...] += dot(vte_ref[...], coefa_ref[...])
    acta_ref[...] = dot(ue_ref[...], jnp.where(s == last, hntn_ref[...], hnt_ref[...]))
    _peer_coefficients(2 * s + 1, st_ref, p1_ref, p2_ref, th_ref, actb_ref, coefb_ref, tb)
    ft_ref[...] += dot(vto_ref[...], coefb_ref[...])


def _peer(hnt, st, u_bf, vt_bf):
    t = hnt.shape[1]
    tb = min(t, PEER_TOKENS)
    n_chunks = u_bf.shape[0] // E_CHUNK
    n_blocks = t // tb
    steps = n_chunks // 2
    ublk = lambda fn: pl.BlockSpec((E_CHUNK, D_MODEL), fn)
    vblk = lambda fn: pl.BlockSpec((None, D_MODEL, E_CHUNK), fn)
    return pl.pallas_call(
        _peer_kernel,
        grid=(n_blocks, steps),
        in_specs=[pl.BlockSpec((D_MODEL, tb), lambda i, s: (0, i)),
                  pl.BlockSpec((D_MODEL, tb), lambda i, s: (0, jnp.minimum(i + 1, n_blocks - 1))),
                  pl.BlockSpec((2 * PEER_HEADS, N_KEYS, tb), lambda i, s: (0, 0, i)),
                  ublk(lambda i, s: (0, 0)),
                  ublk(lambda i, s: (2 * s + 1, 0)),
                  ublk(lambda i, s: ((2 * s + 2) % n_chunks, 0)),
                  vblk(lambda i, s: (2 * s, 0, 0)),
                  vblk(lambda i, s: (2 * s + 1, 0, 0))],
        out_specs=pl.BlockSpec((D_MODEL, tb), lambda i, s: (0, i)),
        out_shape=jax.ShapeDtypeStruct((D_MODEL, t), F32),
        scratch_shapes=[pltpu.VMEM((PEER_HEADS, N_KEYS, tb), F32),
                        pltpu.VMEM((PEER_HEADS, N_KEYS, tb), F32),
                        pltpu.VMEM((PEER_HEADS, N_KEYS, tb), F32),
                        pltpu.VMEM((E_CHUNK, tb), F32),
                        pltpu.VMEM((E_CHUNK, tb), F32),
                        pltpu.VMEM((E_CHUNK, tb), BF16),
                        pltpu.VMEM((E_CHUNK, tb), BF16)],
        compiler_params=_cparams("arbitrary", "arbitrary"),
        name="peer_dense",
    )(hnt, hnt, st, u_bf, u_bf, u_bf, vt_bf, vt_bf)


def _final_kernel(h_ref, ft_ref, g_ref, y_ref):
    hres = h_ref[...] + ft_ref[...].T
    y_ref[...] = hres * lax.rsqrt(jnp.mean(hres * hres, axis=-1, keepdims=True) + EPS) * g_ref[...]


def _final(h, ft, g):
    t = h.shape[0]
    tm = min(t, 512)
    return pl.pallas_call(
        _final_kernel,
        grid=(t // tm,),
        in_specs=[pl.BlockSpec((tm, D_MODEL), lambda i: (i, 0)),
                  pl.BlockSpec((D_MODEL, tm), lambda i: (0, i)),
                  pl.BlockSpec((1, D_MODEL), lambda i: (0, 0))],
        out_specs=pl.BlockSpec((tm, D_MODEL), lambda i: (i, 0)),
        out_shape=jax.ShapeDtypeStruct((t, D_MODEL), F32),
        compiler_params=_cparams("parallel"),
        name="final_norm",
    )(h, ft, g)


def _pack_weights(norm_mix, w_in, conv_w, a_log, dt_bias, gn_a, gn_b, w_br_a, w_br_b, w_out,
                  norm_ffn, peer_wq, peer_keys, peer_u, peer_v, norm_final):
    w = w_in[0]
    n_small = 2 * H_A
    c0 = CONV_CH + VA
    w_main = jnp.concatenate([w[:, :c0], w[:, c0 + n_small:]], axis=1).astype(BF16)
    w_ba = jnp.pad(w[:, c0:c0 + n_small], ((0, 0), (0, LANES - n_small)))
    wb_hi = w_ba.astype(BF16)
    wb_lo = (w_ba - wb_hi.astype(F32)).astype(BF16)
    pad_row = lambda v: jnp.pad(v.reshape(1, H_A), ((0, 0), (H_A, LANES - 2 * H_A)))
    logg = jnp.log1p(-(2.0 ** (-5.0 - jnp.arange(H_B, dtype=F32))))
    return dict(
        norm_mix=norm_mix[0].reshape(1, D_MODEL), w_main=w_main, wb_hi=wb_hi, wb_lo=wb_lo,
        conv_w=conv_w[0], alog_row=pad_row(a_log[0]), dtb_row=pad_row(dt_bias[0]),
        logg_tab=jnp.broadcast_to(logg[:, None, None], (H_B, 1, LANES)),
        gn_a=gn_a[0].reshape(1, DV_A), gn_b=gn_b[0].reshape(1, DV_B),
        w_a=w_br_a[0].astype(BF16), w_b=w_br_b[0].astype(BF16), w_o=w_out[0].astype(BF16),
        norm_ffn=norm_ffn[0].reshape(1, D_MODEL), w_q=peer_wq[0].astype(BF16),
        keys=peer_keys[0].reshape(2 * PEER_HEADS, N_KEYS, N_KEYS).astype(BF16),
        u_bf=peer_u[0].astype(BF16),
        vt_bf=jnp.transpose(peer_v[0].astype(BF16).reshape(-1, E_CHUNK, D_MODEL), (0, 2, 1)),
        norm_final=norm_final.reshape(1, D_MODEL))


def _channel_mix(oa, ob, proj, x2, p):
    h, hnt, st = _postmix(oa, ob, proj, x2, p["gn_a"], p["gn_b"], p["w_a"], p["w_b"], p["w_o"],
                          p["norm_ffn"], p["w_q"], p["keys"])
    ft = _peer(hnt, st, p["u_bf"], p["vt_bf"])
    return _final(h, ft, p["norm_final"])


def _prompt_group(x, p):
    b, n, _ = x.shape
    x2 = x.reshape(b * n, D_MODEL)
    proj, ba = _inproj(x2, p["norm_mix"], p["w_main"], p["wb_hi"], p["wb_lo"])
    proj3 = proj.reshape(b, n, N_MAIN)
    oa, s_delta = _delta_prompt(proj3, ba.reshape(b, n, LANES), p["conv_w"], p["alog_row"], p["dtb_row"])
    cos2, sin2 = _rope_tables(n, 0)
    ob, s_ret = _ret_prompt(proj3, cos2, sin2, p["logg_tab"])
    y = _channel_mix(oa.reshape(b * n, VA), ob.reshape(b * n, VB), proj, x2, p)
    conv_new = proj3[:, n - (CONV_W - 1):, :CONV_CH]
    return y.reshape(b, n, D_MODEL), conv_new[None], s_delta[None], s_ret[None]


def _sample_group(x, conv_buf, s_delta, s_ret, p):
    n = x.shape[0]
    x2 = x.reshape(n, D_MODEL)
    proj, ba = _inproj(x2, p["norm_mix"], p["w_main"], p["wb_hi"], p["wb_lo"])
    cos2, sin2 = _rope_tables(8, PAST_LEN)
    cb3 = jnp.transpose(conv_buf, (1, 0, 2))
    cnew, va, qat, kat, qbt, kbt, bg = _sample_prep(proj, ba, cb3, p["conv_w"], p["alog_row"],
                                                    p["dtb_row"], cos2, sin2)
    sd_new, sr_new, oa, ob = _sample_state(s_delta, s_ret, va, proj, bg, qat, kat, qbt, kbt,
                                           p["logg_tab"])
    y = _channel_mix(oa, ob, proj, x2, p)
    return (y.reshape(n, 1, D_MODEL), jnp.transpose(cnew, (1, 0, 2))[None], sd_new[None], sr_new[None])


def kernel(x_prompt, x_sample, state_conv_a, state_delta, state_ret, norm_mix, w_in, conv_w, a_log, dt_bias, gn_a, gn_b, w_br_a, w_br_b, w_out, norm_ffn, peer_wq, peer_keys, peer_u, peer_v, norm_final):
    assert w_in.shape[0] == 1 and x_sample.shape[1] == 1
    p = _pack_weights(norm_mix, w_in, conv_w, a_log, dt_bias, gn_a, gn_b, w_br_a, w_br_b, w_out,
                      norm_ffn, peer_wq, peer_keys, peer_u, peer_v, norm_final)
    y_p, conv_p, delta_p, ret_p = _prompt_group(x_prompt, p)
    y_s, conv_s, delta_s, ret_s = _sample_group(x_sample, state_conv_a[0], state_delta[0],
                                                state_ret[0], p)
    return (y_p, y_s, conv_p, delta_p, ret_p, conv_s, delta_s, ret_s)
```

```python
import functools
import math

import jax
import jax.numpy as jnp
from jax import lax
from jax.experimental import pallas as pl
from jax.experimental.pallas import tpu as pltpu

F32 = jnp.float32
BF16 = jnp.bfloat16

EPS = 1e-6
D_MODEL = 1024
H_A, DK_A, DV_A, CONV_W = 4, 128, 128, 4
H_B, DK_B, DV_B = 4, 128, 256
ROPE_BASE = 10000.0
PAST_LEN = 16384
N_KEYS = 128
PEER_HEADS = 8
PEER_TOPK = 16
QA, VA, QB, VB = H_A * DK_A, H_A * DV_A, H_B * DK_B, H_B * DV_B
CONV_CH = 2 * QA + VA

COL_GATE_A, COL_QB, COL_KB, COL_VB = 1536, 2048, 2560, 3072
COL_GATE_B, COL_SEL_A, COL_SEL_B, N_MAIN = 4096, 5120, 6144, 7168
LANES = 128
SUBLANES = 8
CHUNK = 128
VMEM_LIMIT = 56 * 1024 * 1024


def _cparams(*sem):
    return pltpu.CompilerParams(dimension_semantics=sem, vmem_limit_bytes=VMEM_LIMIT)


def _dot(a, b):
    return jnp.dot(a.astype(BF16), b.astype(BF16), preferred_element_type=F32)


def _dot_nt(a, b):
    return lax.dot_general(a.astype(BF16), b.astype(BF16), (((1,), (1,)), ((), ())),
                           preferred_element_type=F32)


def _split2(a):
    hi = a.astype(BF16)
    return hi, (a - hi.astype(F32)).astype(BF16)


def _dot3(a, b):
    ah, al = _split2(a)
    bh, bl = _split2(b)
    if a.ndim == 3:
        dims = (((2,), (1,)), ((0,), (0,)))
        d = lambda x, y: lax.dot_general(x, y, dims, preferred_element_type=F32)
    else:
        d = functools.partial(jnp.dot, preferred_element_type=F32)
    return d(ah, bh) + d(al, bh) + d(ah, bl)


def _sigmoid(x):
    return 1.0 / (1.0 + jnp.exp(-x))


def _silu(x):
    return x * _sigmoid(x)


def _softplus(x):
    return jnp.maximum(x, 0.0) + jnp.log1p(jnp.exp(-jnp.abs(x)))


def _gelu_times2(x):
    return x * (1.0 + lax.erf(x * (1.0 / math.sqrt(2.0))))


def _rope_kernel(cos_ref, sin_ref, *, pos0):
    shape = cos_ref.shape
    half = shape[1] // 2
    lane = lax.broadcasted_iota(jnp.int32, shape, 1)
    row = lax.broadcasted_iota(jnp.int32, shape, 0)
    j = jnp.where(lane >= half, lane - half, lane).astype(F32)
    inv = jnp.exp(j * (-math.log(ROPE_BASE) / half))
    ang = (row + pos0).astype(F32) * inv
    s = jnp.sin(ang)
    cos_ref[...] = jnp.cos(ang)
    sin_ref[...] = jnp.where(lane >= half, s, -s)


def _rope_tables(rows, pos0):
    sds = jax.ShapeDtypeStruct((rows, DK_B), F32)
    return pl.pallas_call(functools.partial(_rope_kernel, pos0=pos0), out_shape=(sds, sds),
                          name="rope_tables")()


def _rope(x, cos2, sin2):
    return x * cos2 + pltpu.roll(x, DK_B // 2, axis=1) * sin2


def _inproj_kernel(x_ref, g_ref, w_ref, wbh_ref, wbl_ref, proj_ref, ba_ref, xn_ref):
    @pl.when(pl.program_id(1) == 0)
    def _():
        x = x_ref[...]
        y = x * lax.rsqrt(jnp.mean(x * x, axis=-1, keepdims=True) + EPS) * g_ref[...]
        yh, yl = _split2(y)
        xn_ref[...] = yh
        d = functools.partial(jnp.dot, preferred_element_type=F32)
        ba_ref[...] = d(yh, wbh_ref[...]) + d(yl, wbh_ref[...]) + d(yh, wbl_ref[...])

    proj_ref[...] = jnp.dot(xn_ref[...], w_ref[...], preferred_element_type=F32)


def _inproj(x2, g, w_main, wb_hi, wb_lo):
    t = x2.shape[0]
    tm = min(t, 1024)
    tn = 1024
    return pl.pallas_call(
        _inproj_kernel,
        grid=(t // tm, N_MAIN // tn),
        in_specs=[pl.BlockSpec((tm, D_MODEL), lambda i, j: (i, 0)),
                  pl.BlockSpec((1, D_MODEL), lambda i, j: (0, 0)),
                  pl.BlockSpec((D_MODEL, tn), lambda i, j: (0, j)),
                  pl.BlockSpec((D_MODEL, LANES), lambda i, j: (0, 0)),
                  pl.BlockSpec((D_MODEL, LANES), lambda i, j: (0, 0))],
        out_specs=(pl.BlockSpec((tm, tn), lambda i, j: (i, j)),
                   pl.BlockSpec((tm, LANES), lambda i, j: (i, 0))),
        out_shape=(jax.ShapeDtypeStruct((t, N_MAIN), F32), jax.ShapeDtypeStruct((t, LANES), F32)),
        scratch_shapes=[pltpu.VMEM((tm, D_MODEL), BF16)],
        compiler_params=_cparams("parallel", "arbitrary"),
        name="rms_inproj",
    )(x2, g, w_main, wb_hi, wb_lo)


def _lane_pick(x, idx):
    lane = lax.broadcasted_iota(jnp.int32, x.shape, 1)
    return jnp.sum(jnp.where(lane == idx, x, 0.0), axis=1, keepdims=True)


def _decay_terms(ba, alog_row, dtb_row):
    beta = _sigmoid(ba)
    logd = -jnp.exp(alog_row) * _softplus(ba + dtb_row)
    return beta, logd


def _unit_lower_inverse(lm, masks):
    n = lm.shape[-1]
    row = lax.broadcasted_iota(jnp.int32, (n, n), 0)
    col = lax.broadcasted_iota(jnp.int32, (n, n), 1)
    t = jnp.where(row == col, 1.0, 0.0) - lm * masks[0]
    for m in masks[1:]:
        t = t - _dot3(_dot3(t, lm * m), t)
    return t


def _doubling_masks(n):
    row = lax.broadcasted_iota(jnp.int32, (n, n), 0)
    col = lax.broadcasted_iota(jnp.int32, (n, n), 1)
    masks = []
    lvl = 0
    while (1 << lvl) < n:
        same = (row >> (lvl + 1)) == (col >> (lvl + 1))
        lower = ((row >> lvl) & 1) == 1
        left = ((col >> lvl) & 1) == 0
        masks.append(jnp.where(same & lower & left, 1.0, 0.0))
        lvl += 1
    return masks


def _delta_kernel(q_ref, k_ref, v_ref, ba_ref, wq_ref, wk_ref, wv_ref, alog_ref, dtb_ref,
                  o_ref, s_ref, pad_ref, qs_ref, ks_ref, vs_ref, bt_ref, ld_ref,
                  u_ref, w_ref, qk_ref, qg_ref, kd_ref, gl_ref):
    h = pl.program_id(1)
    n = q_ref.shape[0]
    c = CHUNK

    def conv_silu(x_ref, w_ref):
        pad_ref[0:8, :] = jnp.zeros((8, LANES), F32)
        pad_ref[8:8 + n, :] = x_ref[...]
        base = 8 - (CONV_W - 1)
        acc = pad_ref[base:base + n, :] * w_ref[0:1, :]
        for i in range(1, CONV_W):
            acc = acc + pad_ref[base + i:base + i + n, :] * w_ref[i:i + 1, :]
        return _silu(acc)

    def l2n(x):
        return x * lax.rsqrt(jnp.sum(x * x, axis=-1, keepdims=True) + EPS)

    qs_ref[...] = l2n(conv_silu(q_ref, wq_ref)) * (DK_A ** -0.5)
    ks_ref[...] = l2n(conv_silu(k_ref, wk_ref))
    vs_ref[...] = conv_silu(v_ref, wv_ref)
    beta, logd = _decay_terms(ba_ref[...], alog_ref[...], dtb_ref[...])
    bt_ref[...] = beta
    ld_ref[...] = logd

    row = lax.broadcasted_iota(jnp.int32, (c, c), 0)
    col = lax.broadcasted_iota(jnp.int32, (c, c), 1)
    tril = row >= col
    strict = row > col
    tril_f = jnp.where(tril, 1.0, 0.0).astype(BF16)
    masks = _doubling_masks(c)

    def chunk_terms(ci):
        rows = pl.ds(pl.multiple_of(ci * c, c), c)
        qc, kc, vc = qs_ref[rows, :], ks_ref[rows, :], vs_ref[rows, :]
        bcol = _lane_pick(bt_ref[rows, :], h)
        ldc = ld_ref[rows, :]
        l1 = ldc.astype(BF16)
        r1 = ldc - l1.astype(F32)
        l2 = r1.astype(BF16)
        l3 = (r1 - l2.astype(F32)).astype(BF16)
        d = functools.partial(jnp.dot, preferred_element_type=F32)
        gcol = _lane_pick(d(tril_f, l1) + d(tril_f, l2) + d(tril_f, l3), H_A + h)
        gmat = jnp.broadcast_to(gcol, (c, c))
        diff = gmat - gmat.T
        decay = jnp.where(tril, jnp.exp(jnp.where(tril, diff, 0.0)), 0.0)
        kb = kc * bcol
        lm = jnp.where(strict, _dot_nt(kb, kc) * decay, 0.0)
        eg = jnp.exp(gcol)
        glast = gcol[c - 1:c, :]
        return (lm, vc * bcol, kb * eg, _dot_nt(qc, kc) * decay, qc * eg,
                kc * jnp.exp(glast - gcol), jnp.broadcast_to(jnp.exp(glast), (SUBLANES, LANES)))

    n_chunks = n // c
    group = math.gcd(n_chunks, 16)

    def group_body(gi, carry):
        terms = [chunk_terms(gi * group + j) for j in range(group)]
        t = _unit_lower_inverse(jnp.stack([tm[0] for tm in terms]), masks)
        for j, (_, vb, kbg, qk, qg, kd, gl) in enumerate(terms):
            ci = gi * group + j
            rows = pl.ds(pl.multiple_of(ci * c, c), c)
            u_ref[rows, :] = _dot(t[j], vb)
            w_ref[rows, :] = _dot(t[j], kbg)
            qk_ref[rows, :] = qk
            qg_ref[rows, :] = qg
            kd_ref[rows, :] = kd
            gl_ref[pl.ds(pl.multiple_of(ci * SUBLANES, SUBLANES), SUBLANES), :] = gl
        return carry

    lax.fori_loop(0, n_chunks // group, group_body, 0)

    def state_body(ci, s):
        rows = pl.ds(pl.multiple_of(ci * c, c), c)
        v_new = u_ref[rows, :] - _dot(w_ref[rows, :], s)
        o_ref[rows, :] = _dot(qg_ref[rows, :], s) + _dot(qk_ref[rows, :], v_new)
        g_last = gl_ref[pl.ds(pl.multiple_of(ci * SUBLANES, SUBLANES), SUBLANES), :][0:1, 0:1]
        return s * g_last + _dot(kd_ref[rows, :].T, v_new)

    s_ref[...] = lax.fori_loop(0, n_chunks, state_body, jnp.zeros((DK_A, DV_A), F32))


def _delta_prompt(proj3, ba3, conv_w, alog_row, dtb_row):
    b, n, _ = proj3.shape
    hb = DK_A // LANES

    def col(off):
        return lambda i, h: (i, 0, off // LANES + h * hb)

    def wcol(off):
        return lambda i, h: (0, off // LANES + h * hb)

    seq = lambda off: pl.BlockSpec((None, n, LANES), col(off))
    wsp = lambda off: pl.BlockSpec((CONV_W, LANES), wcol(off))
    row = pl.BlockSpec((1, LANES), lambda i, h: (0, 0))
    return pl.pallas_call(
        _delta_kernel,
        grid=(b, H_A),
        in_specs=[seq(0), seq(QA), seq(2 * QA),
                  pl.BlockSpec((None, n, LANES), lambda i, h: (i, 0, 0)),
                  wsp(0), wsp(QA), wsp(2 * QA), row, row],
        out_specs=(pl.BlockSpec((None, n, DV_A), lambda i, h: (i, 0, h)),
                   pl.BlockSpec((None, None, DK_A, DV_A), lambda i, h: (i, h, 0, 0))),
        out_shape=(jax.ShapeDtypeStruct((b, n, VA), F32),
                   jax.ShapeDtypeStruct((b, H_A, DK_A, DV_A), F32)),
        scratch_shapes=[pltpu.VMEM((n + 8, LANES), F32)] + [pltpu.VMEM((n, LANES), F32)] * 10
                       + [pltpu.VMEM((n // CHUNK * SUBLANES, LANES), F32)],
        compiler_params=_cparams("parallel", "parallel"),
        name="delta_prompt",
    )(proj3, proj3, proj3, ba3, conv_w, conv_w, conv_w, alog_row, dtb_row)


def _ret_kernel(q_ref, k_ref, v_ref, cos_ref, sin_ref, logg_ref, o_ref, s_ref, qs_ref, ks_ref):
    n = q_ref.shape[0]
    c = CHUNK
    cos2, sin2 = cos_ref[...], sin_ref[...]
    qs_ref[...] = _rope(q_ref[...], cos2, sin2)
    ks_ref[...] = _rope(k_ref[...], cos2, sin2) * (DK_B ** -0.5)

    logg = logg_ref[...]
    row = lax.broadcasted_iota(jnp.int32, (c, c), 0)
    col = lax.broadcasted_iota(jnp.int32, (c, c), 1)
    tril = row >= col
    dmat = jnp.where(tril, jnp.exp(jnp.where(tril, (row - col).astype(F32) * logg, 0.0)), 0.0)
    idx = lax.broadcasted_iota(jnp.int32, (c, LANES), 0).astype(F32)
    q_dec = jnp.exp((idx + 1.0) * logg)
    k_dec = jnp.exp((c - 1.0 - idx) * logg)
    g_c = jnp.exp(c * logg[:, 0:1])

    n_chunks = n // c
    group = math.gcd(n_chunks, 4)

    def body(gi, s):
        for j in range(group):
            rows = pl.ds(pl.multiple_of((gi * group + j) * c, c), c)
            qc, kc, vc = qs_ref[rows, :], ks_ref[rows, :], v_ref[rows, :]
            inner = _dot(_dot_nt(qc, kc) * dmat, vc)
            o_ref[rows, :] = inner + _dot(qc * q_dec, s)
            s = s * g_c + _dot((kc * k_dec).T, vc)
        return s

    s_ref[...] = lax.fori_loop(0, n_chunks // group, body, jnp.zeros((DK_B, DV_B), F32))


def _ret_prompt(proj3, cos2, sin2, logg_tab):
    b, n, _ = proj3.shape
    return pl.pallas_call(
        _ret_kernel,
        grid=(b, H_B),
        in_specs=[pl.BlockSpec((None, n, DK_B), lambda i, h: (i, 0, COL_QB // DK_B + h)),
                  pl.BlockSpec((None, n, DK_B), lambda i, h: (i, 0, COL_KB // DK_B + h)),
                  pl.BlockSpec((None, n, DV_B), lambda i, h: (i, 0, COL_VB // DV_B + h)),
                  pl.BlockSpec((n, DK_B), lambda i, h: (0, 0)),
                  pl.BlockSpec((n, DK_B), lambda i, h: (0, 0)),
                  pl.BlockSpec((None, 1, LANES), lambda i, h: (h, 0, 0))],
        out_specs=(pl.BlockSpec((None, n, DV_B), lambda i, h: (i, 0, h)),
                   pl.BlockSpec((None, None, DK_B, DV_B), lambda i, h: (i, h, 0, 0))),
        out_shape=(jax.ShapeDtypeStruct((b, n, VB), F32),
                   jax.ShapeDtypeStruct((b, H_B, DK_B, DV_B), F32)),
        scratch_shapes=[pltpu.VMEM((n, DK_B), F32)] * 2,
        compiler_params=_cparams("parallel", "parallel"),
        name="retention_prompt",
    )(proj3, proj3, proj3, cos2, sin2, logg_tab)


def _sample_prep_kernel(proj_ref, ba_ref, cb_ref, cw_ref, alog_ref, dtb_ref, cos_ref, sin_ref,
                        cnew_ref, va_ref, qat_ref, kat_ref, qbt_ref, kbt_ref, bg_ref):
    x = proj_ref[:, 0:CONV_CH]
    acc = cb_ref[0] * cw_ref[0:1, :]
    for i in range(1, CONV_W - 1):
        acc = acc + cb_ref[i] * cw_ref[i:i + 1, :]
    acc = acc + x * cw_ref[CONV_W - 1:CONV_W, :]
    qkv = _silu(acc)
    for i in range(CONV_W - 2):
        cnew_ref[i] = cb_ref[i + 1]
    cnew_ref[CONV_W - 2] = x

    def l2n(v):
        return v * lax.rsqrt(jnp.sum(v * v, axis=-1, keepdims=True) + EPS)

    cos2, sin2 = cos_ref[0:1, :], sin_ref[0:1, :]
    for h in range(H_A):
        hs = slice(h * DK_A, (h + 1) * DK_A)
        qat_ref[hs, :] = (l2n(qkv[:, h * DK_A:(h + 1) * DK_A]) * (DK_A ** -0.5)).T
        kat_ref[hs, :] = l2n(qkv[:, QA + h * DK_A:QA + (h + 1) * DK_A]).T
    va_ref[...] = qkv[:, 2 * QA:]
    for h in range(H_B):
        hs = slice(h * DK_B, (h + 1) * DK_B)
        qbt_ref[hs, :] = _rope(proj_ref[:, COL_QB + h * DK_B:COL_QB + (h + 1) * DK_B], cos2, sin2).T
        kbt_ref[hs, :] = (_rope(proj_ref[:, COL_KB + h * DK_B:COL_KB + (h + 1) * DK_B], cos2, sin2)
                          * (DK_B ** -0.5)).T
    beta, logd = _decay_terms(ba_ref[...], alog_ref[...], dtb_ref[...])
    lane = lax.broadcasted_iota(jnp.int32, beta.shape, 1)
    bg_ref[...] = jnp.where(lane < H_A, beta, jnp.exp(logd))


def _sample_prep(proj, ba, cb3, conv_w, alog_row, dtb_row, cos2, sin2):
    n = proj.shape[0]
    sd = lambda *s: jax.ShapeDtypeStruct(s, F32)
    return pl.pallas_call(
        _sample_prep_kernel,
        out_shape=(sd(CONV_W - 1, n, CONV_CH), sd(n, VA), sd(QA, n), sd(QA, n), sd(QB, n), sd(QB, n),
                   sd(n, LANES)),
        compiler_params=pltpu.CompilerParams(vmem_limit_bytes=VMEM_LIMIT),
        name="sample_prep",
    )(proj, ba, cb3, conv_w, alog_row, dtb_row, cos2, sin2)


SAMPLE_ROWS = 8


def _sample_state_kernel(sd_ref, sr_ref, va_ref, vb_ref, bg_ref, qat_ref, kat_ref, qbt_ref, kbt_ref,
                         logg_ref, sdn_ref, srn_ref, oa_ref, ob_ref):
    base = pl.program_id(0) * SAMPLE_ROWS
    nseq = qat_ref.shape[1]
    lane = lax.broadcasted_iota(jnp.int32, (DK_A, nseq), 1)

    def column(t_ref, h, seq):
        blk = t_ref[h * DK_A:(h + 1) * DK_A, :]
        return jnp.sum(jnp.where(lane == seq, blk, 0.0), axis=1, keepdims=True)

    for j in range(SAMPLE_ROWS):
        seq = base + j
        bg_row = bg_ref[j:j + 1, :]
        for h in range(H_A):
            kcol, qcol = column(kat_ref, h, seq), column(qat_ref, h, seq)
            beta = _lane_pick(bg_row, h)
            eg = _lane_pick(bg_row, H_A + h)
            s0 = sd_ref[j, h]
            v = va_ref[j:j + 1, h * DV_A:(h + 1) * DV_A]
            ks = jnp.sum(kcol * s0, axis=0, keepdims=True)
            v_new = beta * v - (beta * eg) * ks
            s1 = s0 * eg + kcol * v_new
            sdn_ref[j, h] = s1
            oa_ref[j:j + 1, h * DV_A:(h + 1) * DV_A] = jnp.sum(qcol * s1, axis=0, keepdims=True)
        for h in range(H_B):
            kcol, qcol = column(kbt_ref, h, seq), column(qbt_ref, h, seq)
            gamma = jnp.exp(logg_ref[h][:, 0:1])
            v = vb_ref[j:j + 1, h * DV_B:(h + 1) * DV_B]
            s1 = sr_ref[j, h] * gamma + kcol * v
            srn_ref[j, h] = s1
            ob_ref[j:j + 1, h * DV_B:(h + 1) * DV_B] = jnp.sum(qcol * s1, axis=0, keepdims=True)


def _sample_state(sd, sr, va, proj, bg, qat, kat, qbt, kbt, logg_tab):
    n = sd.shape[0]
    r = SAMPLE_ROWS
    full = lambda a: pl.BlockSpec(a.shape, lambda i: (0,) * a.ndim)
    return pl.pallas_call(
        _sample_state_kernel,
        grid=(n // r,),
        in_specs=[pl.BlockSpec((r, H_A, DK_A, DV_A), lambda i: (i, 0, 0, 0)),
                  pl.BlockSpec((r, H_B, DK_B, DV_B), lambda i: (i, 0, 0, 0)),
                  pl.BlockSpec((r, VA), lambda i: (i, 0)),
                  pl.BlockSpec((r, VB), lambda i: (i, COL_VB // VB)),
                  pl.BlockSpec((r, LANES), lambda i: (i, 0)),
                  full(qat), full(kat), full(qbt), full(kbt), full(logg_tab)],
        out_specs=(pl.BlockSpec((r, H_A, DK_A, DV_A), lambda i: (i, 0, 0, 0)),
                   pl.BlockSpec((r, H_B, DK_B, DV_B), lambda i: (i, 0, 0, 0)),
                   pl.BlockSpec((r, VA), lambda i: (i, 0)),
                   pl.BlockSpec((r, VB), lambda i: (i, 0))),
        out_shape=(jax.ShapeDtypeStruct(sd.shape, F32), jax.ShapeDtypeStruct(sr.shape, F32),
                   jax.ShapeDtypeStruct((n, VA), F32), jax.ShapeDtypeStruct((n, VB), F32)),
        compiler_params=_cparams("parallel"),
        name="sample_state",
    )(sd, sr, va, proj, bg, qat, kat, qbt, kbt, logg_tab)


def _postmix_kernel(oa_ref, ob_ref, ga_ref, gb_ref, sa_ref, sb_ref, x_ref, gna_ref, gnb_ref,
                    wa_ref, wb_ref, wo_ref, nf_ref, wq_ref, keys_ref, h_ref, hnt_ref, st_ref):
    def gated(o_ref, g_ref, gn_ref, heads, dv):
        parts = []
        for h in range(heads):
            o = o_ref[:, h * dv:(h + 1) * dv]
            y = o * lax.rsqrt(jnp.mean(o * o, axis=-1, keepdims=True) + EPS) * gn_ref[...]
            parts.append(y * _silu(g_ref[:, h * dv:(h + 1) * dv]))
        return jnp.concatenate(parts, axis=1)

    br_a = _dot(gated(oa_ref, ga_ref, gna_ref, H_A, DV_A), wa_ref[...])
    br_b = _dot(gated(ob_ref, gb_ref, gnb_ref, H_B, DV_B), wb_ref[...])
    merged = _sigmoid(sa_ref[...]) * br_a + _sigmoid(sb_ref[...]) * br_b
    hres = x_ref[...] + _dot(merged, wo_ref[...])
    h_ref[...] = hres
    hn = hres * lax.rsqrt(jnp.mean(hres * hres, axis=-1, keepdims=True) + EPS) * nf_ref[...]
    hnt_ref[...] = hn.T.astype(BF16)
    q = _dot(hn, wq_ref[...])
    for hp in range(2 * PEER_HEADS):
        st_ref[hp] = _dot_nt(keys_ref[hp], q[:, hp * N_KEYS:(hp + 1) * N_KEYS])


def _postmix(oa, ob, proj, x2, gn_a, gn_b, w_a, w_b, w_o, norm_ffn, w_q, keys):
    t = x2.shape[0]
    tm = min(t, 256)
    full = lambda a: pl.BlockSpec(a.shape, lambda i: (0,) * a.ndim)
    pcol = lambda width, off: pl.BlockSpec((tm, width), lambda i: (i, off // width))
    return pl.pallas_call(
        _postmix_kernel,
        grid=(t // tm,),
        in_specs=[pl.BlockSpec((tm, VA), lambda i: (i, 0)), pl.BlockSpec((tm, VB), lambda i: (i, 0)),
                  pcol(VA, COL_GATE_A), pcol(VB, COL_GATE_B), pcol(D_MODEL, COL_SEL_A),
                  pcol(D_MODEL, COL_SEL_B), pl.BlockSpec((tm, D_MODEL), lambda i: (i, 0)),
                  full(gn_a), full(gn_b), full(w_a), full(w_b), full(w_o), full(norm_ffn), full(w_q),
                  full(keys)],
        out_specs=(pl.BlockSpec((tm, D_MODEL), lambda i: (i, 0)),
                   pl.BlockSpec((D_MODEL, tm), lambda i: (0, i)),
                   pl.BlockSpec((2 * PEER_HEADS, N_KEYS, tm), lambda i: (0, 0, i))),
        out_shape=(jax.ShapeDtypeStruct((t, D_MODEL), F32), jax.ShapeDtypeStruct((D_MODEL, t), BF16),
                   jax.ShapeDtypeStruct((2 * PEER_HEADS, N_KEYS, t), F32)),
        compiler_params=_cparams("parallel"),
        name="postmix",
    )(oa, ob, proj, proj, proj, proj, x2, gn_a, gn_b, w_a, w_b, w_o, norm_ffn, w_q, keys)


A_PER_CHUNK = 2 * SUBLANES
E_CHUNK = A_PER_CHUNK * N_KEYS
PEER_TOKENS = 256

def _batcher_pairs(n):
    pairs = []
    p = 1
    while p < n:
        k = p
        while k >= 1:
            for j in range(k % p, n - k, 2 * k):
                for i in range(min(k, n - j - k)):
                    if (i + j) // (2 * p) == (i + j + k) // (2 * p):
                        pairs.append((i + j, i + j + k))
            k //= 2
        p *= 2
    return pairs


_SORT_TOPK = _batcher_pairs(PEER_TOPK)


def _vmax(x, y):
    if x is None:
        return y
    return x if y is None else jnp.maximum(x, y)


def _exchange(x, y):
    if x is None or y is None:
        return _vmax(x, y), None
    return jnp.maximum(x, y), jnp.minimum(x, y)


def _sort_desc(v):
    v = list(v)
    for i, j in _SORT_TOPK:
        v[i], v[j] = _exchange(v[i], v[j])
    return v


def _merge_top(v, w):
    n = len(v)
    c = [_vmax(v[i], w[n - 1 - i]) for i in range(n)]
    d = n // 2
    while d >= 1:
        for i in range(n):
            if (i & d) == 0:
                c[i], c[i + d] = _exchange(c[i], c[i + d])
        d //= 2
    return c


def _top_scores(x):
    v = _sort_desc([x[i * SUBLANES:(i + 1) * SUBLANES, :] for i in range(N_KEYS // SUBLANES)])
    shift = SUBLANES // 2
    while shift >= 1:
        v = _merge_top(v, [pltpu.roll(t, shift, axis=0) for t in v])
        shift //= 2
    return v


def _peer_select(st_ref, p1_ref, p2_ref, th_ref, tb):
    sub = lax.broadcasted_iota(jnp.int32, (SUBLANES, LANES), 0)
    for tg in range(tb // LANES):
        lanes = slice(tg * LANES, (tg + 1) * LANES)
        top = [[None] * PEER_TOPK for _ in range(2)]
        for h in range(PEER_HEADS):
            for p in range(2):
                v = _top_scores(st_ref[2 * h + p, :, lanes])
                for r in range(PEER_TOPK):
                    top[p][r] = v[r] if h == 0 else jnp.where(sub == h, v[r], top[p][r])
        sums = [[top[0][r1] + top[1][r2] if (r1 + 1) * (r2 + 1) <= PEER_TOPK else None
                 for r2 in range(PEER_TOPK)] for r1 in range(PEER_TOPK)]
        best = sums[0]
        for r1 in range(1, PEER_TOPK):
            best = _merge_top(best, sums[r1])
        z = jnp.ones((SUBLANES, LANES), F32)
        for r in range(1, PEER_TOPK):
            z = z + jnp.exp(best[r] - best[0])
        tau = best[PEER_TOPK - 1]
        theta = []
        for r1 in range(PEER_TOPK):
            th = None
            for r2 in range(PEER_TOPK):
                if sums[r1][r2] is not None:
                    cand = jnp.where(sums[r1][r2] >= tau, top[1][r2], jnp.inf)
                    th = cand if th is None else jnp.minimum(th, cand)
            theta.append(th)
        for h in range(PEER_HEADS):
            row = slice(h, h + 1)
            s1 = st_ref[2 * h, :, lanes]
            th_a = jnp.full(s1.shape, jnp.inf, F32)
            for r1 in range(PEER_TOPK):
                th_a = jnp.where(s1 == top[0][r1][row, :], theta[r1][row, :], th_a)
            th_ref[h, :, lanes] = th_a
            p1_ref[h, :, lanes] = jnp.exp(s1 - top[0][0][row, :]) / z[row, :] * 0.5
            p2_ref[h, :, lanes] = jnp.exp(st_ref[2 * h + 1, :, lanes] - top[1][0][row, :])


def _peer_coefficients(chunk, st_ref, p1_ref, p2_ref, th_ref, act_ref, coef_ref, tb):
    a0 = pl.multiple_of(chunk * A_PER_CHUNK, A_PER_CHUNK)
    for tg in range(tb // LANES):
        lanes = slice(tg * LANES, (tg + 1) * LANES)
        tht = [th_ref[h, pl.ds(a0, A_PER_CHUNK), lanes] for h in range(PEER_HEADS)]
        p1t = [p1_ref[h, pl.ds(a0, A_PER_CHUNK), lanes] for h in range(PEER_HEADS)]
        for r in range(A_PER_CHUNK):
            rows = slice(r * N_KEYS, (r + 1) * N_KEYS)
            gate = jnp.zeros((N_KEYS, LANES), F32)
            for h in range(PEER_HEADS):
                keep = st_ref[2 * h + 1, :, lanes] >= tht[h][r:r + 1, :]
                gate = gate + jnp.where(keep, p2_ref[h, :, lanes], 0.0) * p1t[h][r:r + 1, :]
            coef_ref[rows, lanes] = (gate * _gelu_times2(act_ref[rows, lanes])).astype(BF16)


def _peer_kernel(hnt_ref, hntn_ref, st_ref, u0_ref, uo_ref, ue_ref, vte_ref, vto_ref, ft_ref,
                 p1_ref, p2_ref, th_ref, acta_ref, actb_ref, coefa_ref, coefb_ref):
    i = pl.program_id(0)
    s = pl.program_id(1)
    last = pl.num_programs(1) - 1
    tb = hnt_ref.shape[1]
    dot = functools.partial(jnp.dot, preferred_element_type=F32)

    @pl.when(s == 0)
    def _():
        _peer_select(st_ref, p1_ref, p2_ref, th_ref, tb)
        ft_ref[...] = jnp.zeros_like(ft_ref)

    @pl.when((s == 0) & (i == 0))
    def _():
        acta_ref[...] = dot(u0_ref[...], hnt_ref[...])

    actb_ref[...] = dot(uo_ref[...], hnt_ref[...])
    _peer_coefficients(2 * s, st_ref, p1_ref, p2_ref, th_ref, acta_ref, coefa_ref, tb)
    ft_ref[...] += dot(vte_ref[...], coefa_ref[...])
    acta_ref[...] = dot(ue_ref[...], jnp.where(s == last, hntn_ref[...], hnt_ref[...]))
    _peer_coefficients(2 * s + 1, st_ref, p1_ref, p2_ref, th_ref, actb_ref, coefb_ref, tb)
    ft_ref[...] += dot(vto_ref[...], coefb_ref[...])


def _peer(hnt, st, u_bf, vt_bf):
    t = hnt.shape[1]
    tb = min(t, PEER_TOKENS)
    n_chunks = u_bf.shape[0] // E_CHUNK
    n_blocks = t // tb
    steps = n_chunks // 2
    ublk = lambda fn: pl.BlockSpec((E_CHUNK, D_MODEL), fn)
    vblk = lambda fn: pl.BlockSpec((None, D_MODEL, E_CHUNK), fn)
    return pl.pallas_call(
        _peer_kernel,
        grid=(n_blocks, steps),
        in_specs=[pl.BlockSpec((D_MODEL, tb), lambda i, s: (0, i)),
                  pl.BlockSpec((D_MODEL, tb), lambda i, s: (0, jnp.minimum(i + 1, n_blocks - 1))),
                  pl.BlockSpec((2 * PEER_HEADS, N_KEYS, tb), lambda i, s: (0, 0, i)),
                  ublk(lambda i, s: (0, 0)),
                  ublk(lambda i, s: (2 * s + 1, 0)),
                  ublk(lambda i, s: ((2 * s + 2) % n_chunks, 0)),
                  vblk(lambda i, s: (2 * s, 0, 0)),
                  vblk(lambda i, s: (2 * s + 1, 0, 0))],
        out_specs=pl.BlockSpec((D_MODEL, tb), lambda i, s: (0, i)),
        out_shape=jax.ShapeDtypeStruct((D_MODEL, t), F32),
        scratch_shapes=[pltpu.VMEM((PEER_HEADS, N_KEYS, tb), F32),
                        pltpu.VMEM((PEER_HEADS, N_KEYS, tb), F32),
                        pltpu.VMEM((PEER_HEADS, N_KEYS, tb), F32),
                        pltpu.VMEM((E_CHUNK, tb), F32),
                        pltpu.VMEM((E_CHUNK, tb), F32),
                        pltpu.VMEM((E_CHUNK, tb), BF16),
                        pltpu.VMEM((E_CHUNK, tb), BF16)],
        compiler_params=_cparams("arbitrary", "arbitrary"),
        name="peer_dense",
    )(hnt, hnt, st, u_bf, u_bf, u_bf, vt_bf, vt_bf)


def _final_kernel(h_ref, ft_ref, g_ref, y_ref):
    hres = h_ref[...] + ft_ref[...].T
    y_ref[...] = hres * lax.rsqrt(jnp.mean(hres * hres, axis=-1, keepdims=True) + EPS) * g_ref[...]


def _final(h, ft, g):
    t = h.shape[0]
    tm = min(t, 512)
    return pl.pallas_call(
        _final_kernel,
        grid=(t // tm,),
        in_specs=[pl.BlockSpec((tm, D_MODEL), lambda i: (i, 0)),
                  pl.BlockSpec((D_MODEL, tm), lambda i: (0, i)),
                  pl.BlockSpec((1, D_MODEL), lambda i: (0, 0))],
        out_specs=pl.BlockSpec((tm, D_MODEL), lambda i: (i, 0)),
        out_shape=jax.ShapeDtypeStruct((t, D_MODEL), F32),
        compiler_params=_cparams("parallel"),
        name="final_norm",
    )(h, ft, g)


def _pack_weights(norm_mix, w_in, conv_w, a_log, dt_bias, gn_a, gn_b, w_br_a, w_br_b, w_out,
                  norm_ffn, peer_wq, peer_keys, peer_u, peer_v, norm_final):
    w = w_in[0]
    n_small = 2 * H_A
    c0 = CONV_CH + VA
    w_main = jnp.concatenate([w[:, :c0], w[:, c0 + n_small:]], axis=1).astype(BF16)
    w_ba = jnp.pad(w[:, c0:c0 + n_small], ((0, 0), (0, LANES - n_small)))
    wb_hi = w_ba.astype(BF16)
    wb_lo = (w_ba - wb_hi.astype(F32)).astype(BF16)
    pad_row = lambda v: jnp.pad(v.reshape(1, H_A), ((0, 0), (H_A, LANES - 2 * H_A)))
    logg = jnp.log1p(-(2.0 ** (-5.0 - jnp.arange(H_B, dtype=F32))))
    return dict(
        norm_mix=norm_mix[0].reshape(1, D_MODEL), w_main=w_main, wb_hi=wb_hi, wb_lo=wb_lo,
        conv_w=conv_w[0], alog_row=pad_row(a_log[0]), dtb_row=pad_row(dt_bias[0]),
        logg_tab=jnp.broadcast_to(logg[:, None, None], (H_B, 1, LANES)),
        gn_a=gn_a[0].reshape(1, DV_A), gn_b=gn_b[0].reshape(1, DV_B),
        w_a=w_br_a[0].astype(BF16), w_b=w_br_b[0].astype(BF16), w_o=w_out[0].astype(BF16),
        norm_ffn=norm_ffn[0].reshape(1, D_MODEL), w_q=peer_wq[0].astype(BF16),
        keys=peer_keys[0].reshape(2 * PEER_HEADS, N_KEYS, N_KEYS).astype(BF16),
        u_bf=peer_u[0].astype(BF16),
        vt_bf=jnp.transpose(peer_v[0].astype(BF16).reshape(-1, E_CHUNK, D_MODEL), (0, 2, 1)),
        norm_final=norm_final.reshape(1, D_MODEL))


def _channel_mix(oa, ob, proj, x2, p):
    h, hnt, st = _postmix(oa, ob, proj, x2, p["gn_a"], p["gn_b"], p["w_a"], p["w_b"], p["w_o"],
                          p["norm_ffn"], p["w_q"], p["keys"])
    ft = _peer(hnt, st, p["u_bf"], p["vt_bf"])
    return _final(h, ft, p["norm_final"])


def _prompt_group(x, p):
    b, n, _ = x.shape
    x2 = x.reshape(b * n, D_MODEL)
    proj, ba = _inproj(x2, p["norm_mix"], p["w_main"], p["wb_hi"], p["wb_lo"])
    proj3 = proj.reshape(b, n, N_MAIN)
    oa, s_delta = _delta_prompt(proj3, ba.reshape(b, n, LANES), p["conv_w"], p["alog_row"], p["dtb_row"])
    cos2, sin2 = _rope_tables(n, 0)
    ob, s_ret = _ret_prompt(proj3, cos2, sin2, p["logg_tab"])
    y = _channel_mix(oa.reshape(b * n, VA), ob.reshape(b * n, VB), proj, x2, p)
    conv_new = proj3[:, n - (CONV_W - 1):, :CONV_CH]
    return y.reshape(b, n, D_MODEL), conv_new[None], s_delta[None], s_ret[None]


def _sample_group(x, conv_buf, s_delta, s_ret, p):
    n = x.shape[0]
    x2 = x.reshape(n, D_MODEL)
    proj, ba = _inproj(x2, p["norm_mix"], p["w_main"], p["wb_hi"], p["wb_lo"])
    cos2, sin2 = _rope_tables(8, PAST_LEN)
    cb3 = jnp.transpose(conv_buf, (1, 0, 2))
    cnew, va, qat, kat, qbt, kbt, bg = _sample_prep(proj, ba, cb3, p["conv_w"], p["alog_row"],
                                                    p["dtb_row"], cos2, sin2)
    sd_new, sr_new, oa, ob = _sample_state(s_delta, s_ret, va, proj, bg, qat, kat, qbt, kbt,
                                           p["logg_tab"])
    y = _channel_mix(oa, ob, proj, x2, p)
    return (y.reshape(n, 1, D_MODEL), jnp.transpose(cnew, (1, 0, 2))[None], sd_new[None], sr_new[None])


def kernel(x_prompt, x_sample, state_conv_a, state_delta, state_ret, norm_mix, w_in, conv_w, a_log, dt_bias, gn_a, gn_b, w_br_a, w_br_b, w_out, norm_ffn, peer_wq, peer_keys, peer_u, peer_v, norm_final):
    assert w_in.shape[0] == 1 and x_sample.shape[1] == 1
    p = _pack_weights(norm_mix, w_in, conv_w, a_log, dt_bias, gn_a, gn_b, w_br_a, w_br_b, w_out,
                      norm_ffn, peer_wq, peer_keys, peer_u, peer_v, norm_final)
    y_p, conv_p, delta_p, ret_p = _prompt_group(x_prompt, p)
    y_s, conv_s, delta_s, ret_s = _sample_group(x_sample, state_conv_a[0], state_delta[0],
                                                state_ret[0], p)
    return (y_p, y_s, conv_p, delta_p, ret_p, conv_s, delta_s, ret_s)
```

```python
import functools
import math

import jax
import jax.numpy as jnp
from jax import lax
from jax.experimental import pallas as pl
from jax.experimental.pallas import tpu as pltpu

F32 = jnp.float32
BF16 = jnp.bfloat16

EPS = 1e-6
D_MODEL = 1024
H_A, DK_A, DV_A, CONV_W = 4, 128, 128, 4
H_B, DK_B, DV_B = 4, 128, 256
ROPE_BASE = 10000.0
PAST_LEN = 16384
N_KEYS = 128
PEER_HEADS = 8
PEER_TOPK = 16
QA, VA, QB, VB = H_A * DK_A, H_A * DV_A, H_B * DK_B, H_B * DV_B
CONV_CH = 2 * QA + VA

COL_GATE_A, COL_QB, COL_KB, COL_VB = 1536, 2048, 2560, 3072
COL_GATE_B, COL_SEL_A, COL_SEL_B, N_MAIN = 4096, 5120, 6144, 7168
LANES = 128
SUBLANES = 8
CHUNK = 128
VMEM_LIMIT = 56 * 1024 * 1024


def _cparams(*sem):
    return pltpu.CompilerParams(dimension_semantics=sem, vmem_limit_bytes=VMEM_LIMIT)


def _dot(a, b):
    return jnp.dot(a.astype(BF16), b.astype(BF16), preferred_element_type=F32)


def _dot_nt(a, b):
    return lax.dot_general(a.astype(BF16), b.astype(BF16), (((1,), (1,)), ((), ())),
                           preferred_element_type=F32)


def _split2(a):
    hi = a.astype(BF16)
    return hi, (a - hi.astype(F32)).astype(BF16)


def _dot3(a, b):
    ah, al = _split2(a)
    bh, bl = _split2(b)
    if a.ndim == 3:
        dims = (((2,), (1,)), ((0,), (0,)))
        d = lambda x, y: lax.dot_general(x, y, dims, preferred_element_type=F32)
    else:
        d = functools.partial(jnp.dot, preferred_element_type=F32)
    return d(ah, bh) + d(al, bh) + d(ah, bl)


def _sigmoid(x):
    return 1.0 / (1.0 + jnp.exp(-x))


def _silu(x):
    return x * _sigmoid(x)


def _softplus(x):
    return jnp.maximum(x, 0.0) + jnp.log1p(jnp.exp(-jnp.abs(x)))


def _gelu_times2(x):
    return x * (1.0 + lax.erf(x * (1.0 / math.sqrt(2.0))))


def _rope_kernel(cos_ref, sin_ref, *, pos0):
    shape = cos_ref.shape
    half = shape[1] // 2
    lane = lax.broadcasted_iota(jnp.int32, shape, 1)
    row = lax.broadcasted_iota(jnp.int32, shape, 0)
    j = jnp.where(lane >= half, lane - half, lane).astype(F32)
    inv = jnp.exp(j * (-math.log(ROPE_BASE) / half))
    ang = (row + pos0).astype(F32) * inv
    s = jnp.sin(ang)
    cos_ref[...] = jnp.cos(ang)
    sin_ref[...] = jnp.where(lane >= half, s, -s)


def _rope_tables(rows, pos0):
    sds = jax.ShapeDtypeStruct((rows, DK_B), F32)
    return pl.pallas_call(functools.partial(_rope_kernel, pos0=pos0), out_shape=(sds, sds),
                          name="rope_tables")()


def _rope(x, cos2, sin2):
    return x * cos2 + pltpu.roll(x, DK_B // 2, axis=1) * sin2


def _inproj_kernel(x_ref, g_ref, w_ref, wbh_ref, wbl_ref, proj_ref, ba_ref, xn_ref):
    @pl.when(pl.program_id(1) == 0)
    def _():
        x = x_ref[...]
        y = x * lax.rsqrt(jnp.mean(x * x, axis=-1, keepdims=True) + EPS) * g_ref[...]
        yh, yl = _split2(y)
        xn_ref[...] = yh
        d = functools.partial(jnp.dot, preferred_element_type=F32)
        ba_ref[...] = d(yh, wbh_ref[...]) + d(yl, wbh_ref[...]) + d(yh, wbl_ref[...])

    proj_ref[...] = jnp.dot(xn_ref[...], w_ref[...], preferred_element_type=F32)


def _inproj(x2, g, w_main, wb_hi, wb_lo):
    t = x2.shape[0]
    tm = min(t, 1024)
    tn = 1024
    return pl.pallas_call(
        _inproj_kernel,
        grid=(t // tm, N_MAIN // tn),
        in_specs=[pl.BlockSpec((tm, D_MODEL), lambda i, j: (i, 0)),
                  pl.BlockSpec((1, D_MODEL), lambda i, j: (0, 0)),
                  pl.BlockSpec((D_MODEL, tn), lambda i, j: (0, j)),
                  pl.BlockSpec((D_MODEL, LANES), lambda i, j: (0, 0)),
                  pl.BlockSpec((D_MODEL, LANES), lambda i, j: (0, 0))],
        out_specs=(pl.BlockSpec((tm, tn), lambda i, j: (i, j)),
                   pl.BlockSpec((tm, LANES), lambda i, j: (i, 0))),
        out_shape=(jax.ShapeDtypeStruct((t, N_MAIN), F32), jax.ShapeDtypeStruct((t, LANES), F32)),
        scratch_shapes=[pltpu.VMEM((tm, D_MODEL), BF16)],
        compiler_params=_cparams("parallel", "arbitrary"),
        name="rms_inproj",
    )(x2, g, w_main, wb_hi, wb_lo)


def _lane_pick(x, idx):
    lane = lax.broadcasted_iota(jnp.int32, x.shape, 1)
    return jnp.sum(jnp.where(lane == idx, x, 0.0), axis=1, keepdims=True)


def _decay_terms(ba, alog_row, dtb_row):
    beta = _sigmoid(ba)
    logd = -jnp.exp(alog_row) * _softplus(ba + dtb_row)
    return beta, logd


def _unit_lower_inverse(lm, masks):
    n = lm.shape[-1]
    row = lax.broadcasted_iota(jnp.int32, (n, n), 0)
    col = lax.broadcasted_iota(jnp.int32, (n, n), 1)
    t = jnp.where(row == col, 1.0, 0.0) - lm * masks[0]
    for m in masks[1:]:
        t = t - _dot3(_dot3(t, lm * m), t)
    return t


def _doubling_masks(n):
    row = lax.broadcasted_iota(jnp.int32, (n, n), 0)
    col = lax.broadcasted_iota(jnp.int32, (n, n), 1)
    masks = []
    lvl = 0
    while (1 << lvl) < n:
        same = (row >> (lvl + 1)) == (col >> (lvl + 1))
        lower = ((row >> lvl) & 1) == 1
        left = ((col >> lvl) & 1) == 0
        masks.append(jnp.where(same & lower & left, 1.0, 0.0))
        lvl += 1
    return masks


def _delta_kernel(q_ref, k_ref, v_ref, ba_ref, wq_ref, wk_ref, wv_ref, alog_ref, dtb_ref,
                  o_ref, s_ref, pad_ref, qs_ref, ks_ref, vs_ref, bt_ref, ld_ref,
                  u_ref, w_ref, qk_ref, qg_ref, kd_ref, gl_ref):
    h = pl.program_id(1)
    n = q_ref.shape[0]
    c = CHUNK

    def conv_silu(x_ref, w_ref):
        pad_ref[0:8, :] = jnp.zeros((8, LANES), F32)
        pad_ref[8:8 + n, :] = x_ref[...]
        base = 8 - (CONV_W - 1)
        acc = pad_ref[base:base + n, :] * w_ref[0:1, :]
        for i in range(1, CONV_W):
            acc = acc + pad_ref[base + i:base + i + n, :] * w_ref[i:i + 1, :]
        return _silu(acc)

    def l2n(x):
        return x * lax.rsqrt(jnp.sum(x * x, axis=-1, keepdims=True) + EPS)

    qs_ref[...] = l2n(conv_silu(q_ref, wq_ref)) * (DK_A ** -0.5)
    ks_ref[...] = l2n(conv_silu(k_ref, wk_ref))
    vs_ref[...] = conv_silu(v_ref, wv_ref)
    beta, logd = _decay_terms(ba_ref[...], alog_ref[...], dtb_ref[...])
    bt_ref[...] = beta
    ld_ref[...] = logd

    row = lax.broadcasted_iota(jnp.int32, (c, c), 0)
    col = lax.broadcasted_iota(jnp.int32, (c, c), 1)
    tril = row >= col
    strict = row > col
    tril_f = jnp.where(tril, 1.0, 0.0).astype(BF16)
    masks = _doubling_masks(c)

    def chunk_terms(ci):
        rows = pl.ds(pl.multiple_of(ci * c, c), c)
        qc, kc, vc = qs_ref[rows, :], ks_ref[rows, :], vs_ref[rows, :]
        bcol = _lane_pick(bt_ref[rows, :], h)
        ldc = ld_ref[rows, :]
        l1 = ldc.astype(BF16)
        r1 = ldc - l1.astype(F32)
        l2 = r1.astype(BF16)
        l3 = (r1 - l2.astype(F32)).astype(BF16)
        d = functools.partial(jnp.dot, preferred_element_type=F32)
        gcol = _lane_pick(d(tril_f, l1) + d(tril_f, l2) + d(tril_f, l3), H_A + h)
        gmat = jnp.broadcast_to(gcol, (c, c))
        diff = gmat - gmat.T
        decay = jnp.where(tril, jnp.exp(jnp.where(tril, diff, 0.0)), 0.0)
        kb = kc * bcol
        lm = jnp.where(strict, _dot_nt(kb, kc) * decay, 0.0)
        eg = jnp.exp(gcol)
        glast = gcol[c - 1:c, :]
        return (lm, vc * bcol, kb * eg, _dot_nt(qc, kc) * decay, qc * eg,
                kc * jnp.exp(glast - gcol), jnp.broadcast_to(jnp.exp(glast), (SUBLANES, LANES)))

    n_chunks = n // c
    group = math.gcd(n_chunks, 16)

    def group_body(gi, carry):
        terms = [chunk_terms(gi * group + j) for j in range(group)]
        t = _unit_lower_inverse(jnp.stack([tm[0] for tm in terms]), masks)
        for j, (_, vb, kbg, qk, qg, kd, gl) in enumerate(terms):
            ci = gi * group + j
            rows = pl.ds(pl.multiple_of(ci * c, c), c)
            u_ref[rows, :] = _dot(t[j], vb)
            w_ref[rows, :] = _dot(t[j], kbg)
            qk_ref[rows, :] = qk
            qg_ref[rows, :] = qg
            kd_ref[rows, :] = kd
            gl_ref[pl.ds(pl.multiple_of(ci * SUBLANES, SUBLANES), SUBLANES), :] = gl
        return carry

    lax.fori_loop(0, n_chunks // group, group_body, 0)

    def state_body(ci, s):
        rows = pl.ds(pl.multiple_of(ci * c, c), c)
        v_new = u_ref[rows, :] - _dot(w_ref[rows, :], s)
        o_ref[rows, :] = _dot(qg_ref[rows, :], s) + _dot(qk_ref[rows, :], v_new)
        g_last = gl_ref[pl.ds(pl.multiple_of(ci * SUBLANES, SUBLANES), SUBLANES), :][0:1, 0:1]
        return s * g_last + _dot(kd_ref[rows, :].T, v_new)

    s_ref[...] = lax.fori_loop(0, n_chunks, state_body, jnp.zeros((DK_A, DV_A), F32))


def _delta_prompt(proj3, ba3, conv_w, alog_row, dtb_row):
    b, n, _ = proj3.shape
    hb = DK_A // LANES

    def col(off):
        return lambda i, h: (i, 0, off // LANES + h * hb)

    def wcol(off):
        return lambda i, h: (0, off // LANES + h * hb)

    seq = lambda off: pl.BlockSpec((None, n, LANES), col(off))
    wsp = lambda off: pl.BlockSpec((CONV_W, LANES), wcol(off))
    row = pl.BlockSpec((1, LANES), lambda i, h: (0, 0))
    return pl.pallas_call(
        _delta_kernel,
        grid=(b, H_A),
        in_specs=[seq(0), seq(QA), seq(2 * QA),
                  pl.BlockSpec((None, n, LANES), lambda i, h: (i, 0, 0)),
                  wsp(0), wsp(QA), wsp(2 * QA), row, row],
        out_specs=(pl.BlockSpec((None, n, DV_A), lambda i, h: (i, 0, h)),
                   pl.BlockSpec((None, None, DK_A, DV_A), lambda i, h: (i, h, 0, 0))),
        out_shape=(jax.ShapeDtypeStruct((b, n, VA), F32),
                   jax.ShapeDtypeStruct((b, H_A, DK_A, DV_A), F32)),
        scratch_shapes=[pltpu.VMEM((n + 8, LANES), F32)] + [pltpu.VMEM((n, LANES), F32)] * 10
                       + [pltpu.VMEM((n // CHUNK * SUBLANES, LANES), F32)],
        compiler_params=_cparams("parallel", "parallel"),
        name="delta_prompt",
    )(proj3, proj3, proj3, ba3, conv_w, conv_w, conv_w, alog_row, dtb_row)


def _ret_kernel(q_ref, k_ref, v_ref, cos_ref, sin_ref, logg_ref, o_ref, s_ref, qs_ref, ks_ref):
    n = q_ref.shape[0]
    c = CHUNK
    cos2, sin2 = cos_ref[...], sin_ref[...]
    qs_ref[...] = _rope(q_ref[...], cos2, sin2)
    ks_ref[...] = _rope(k_ref[...], cos2, sin2) * (DK_B ** -0.5)

    logg = logg_ref[...]
    row = lax.broadcasted_iota(jnp.int32, (c, c), 0)
    col = lax.broadcasted_iota(jnp.int32, (c, c), 1)
    tril = row >= col
    dmat = jnp.where(tril, jnp.exp(jnp.where(tril, (row - col).astype(F32) * logg, 0.0)), 0.0)
    idx = lax.broadcasted_iota(jnp.int32, (c, LANES), 0).astype(F32)
    q_dec = jnp.exp((idx + 1.0) * logg)
    k_dec = jnp.exp((c - 1.0 - idx) * logg)
    g_c = jnp.exp(c * logg[:, 0:1])

    n_chunks = n // c
    group = math.gcd(n_chunks, 4)

    def body(gi, s):
        for j in range(group):
            rows = pl.ds(pl.multiple_of((gi * group + j) * c, c), c)
            qc, kc, vc = qs_ref[rows, :], ks_ref[rows, :], v_ref[rows, :]
            inner = _dot(_dot_nt(qc, kc) * dmat, vc)
            o_ref[rows, :] = inner + _dot(qc * q_dec, s)
            s = s * g_c + _dot((kc * k_dec).T, vc)
        return s

    s_ref[...] = lax.fori_loop(0, n_chunks // group, body, jnp.zeros((DK_B, DV_B), F32))


def _ret_prompt(proj3, cos2, sin2, logg_tab):
    b, n, _ = proj3.shape
    return pl.pallas_call(
        _ret_kernel,
        grid=(b, H_B),
        in_specs=[pl.BlockSpec((None, n, DK_B), lambda i, h: (i, 0, COL_QB // DK_B + h)),
                  pl.BlockSpec((None, n, DK_B), lambda i, h: (i, 0, COL_KB // DK_B + h)),
                  pl.BlockSpec((None, n, DV_B), lambda i, h: (i, 0, COL_VB // DV_B + h)),
                  pl.BlockSpec((n, DK_B), lambda i, h: (0, 0)),
                  pl.BlockSpec((n, DK_B), lambda i, h: (0, 0)),
                  pl.BlockSpec((None, 1, LANES), lambda i, h: (h, 0, 0))],
        out_specs=(pl.BlockSpec((None, n, DV_B), lambda i, h: (i, 0, h)),
                   pl.BlockSpec((None, None, DK_B, DV_B), lambda i, h: (i, h, 0, 0))),
        out_shape=(jax.ShapeDtypeStruct((b, n, VB), F32),
                   jax.ShapeDtypeStruct((b, H_B, DK_B, DV_B), F32)),
        scratch_shapes=[pltpu.VMEM((n, DK_B), F32)] * 2,
        compiler_params=_cparams("parallel", "parallel"),
        name="retention_prompt",
    )(proj3, proj3, proj3, cos2, sin2, logg_tab)


def _sample_prep_kernel(proj_ref, ba_ref, cb_ref, cw_ref, alog_ref, dtb_ref, cos_ref, sin_ref,
                        cnew_ref, va_ref, qat_ref, kat_ref, qbt_ref, kbt_ref, bg_ref):
    x = proj_ref[:, 0:CONV_CH]
    acc = cb_ref[0] * cw_ref[0:1, :]
    for i in range(1, CONV_W - 1):
        acc = acc + cb_ref[i] * cw_ref[i:i + 1, :]
    acc = acc + x * cw_ref[CONV_W - 1:CONV_W, :]
    qkv = _silu(acc)
    for i in range(CONV_W - 2):
        cnew_ref[i] = cb_ref[i + 1]
    cnew_ref[CONV_W - 2] = x

    def l2n(v):
        return v * lax.rsqrt(jnp.sum(v * v, axis=-1, keepdims=True) + EPS)

    cos2, sin2 = cos_ref[0:1, :], sin_ref[0:1, :]
    for h in range(H_A):
        hs = slice(h * DK_A, (h + 1) * DK_A)
        qat_ref[hs, :] = (l2n(qkv[:, h * DK_A:(h + 1) * DK_A]) * (DK_A ** -0.5)).T
        kat_ref[hs, :] = l2n(qkv[:, QA + h * DK_A:QA + (h + 1) * DK_A]).T
    va_ref[...] = qkv[:, 2 * QA:]
    for h in range(H_B):
        hs = slice(h * DK_B, (h + 1) * DK_B)
        qbt_ref[hs, :] = _rope(proj_ref[:, COL_QB + h * DK_B:COL_QB + (h + 1) * DK_B], cos2, sin2).T
        kbt_ref[hs, :] = (_rope(proj_ref[:, COL_KB + h * DK_B:COL_KB + (h + 1) * DK_B], cos2, sin2)
                          * (DK_B ** -0.5)).T
    beta, logd = _decay_terms(ba_ref[...], alog_ref[...], dtb_ref[...])
    lane = lax.broadcasted_iota(jnp.int32, beta.shape, 1)
    bg_ref[...] = jnp.where(lane < H_A, beta, jnp.exp(logd))


def _sample_prep(proj, ba, cb3, conv_w, alog_row, dtb_row, cos2, sin2):
    n = proj.shape[0]
    sd = lambda *s: jax.ShapeDtypeStruct(s, F32)
    return pl.pallas_call(
        _sample_prep_kernel,
        out_shape=(sd(CONV_W - 1, n, CONV_CH), sd(n, VA), sd(QA, n), sd(QA, n), sd(QB, n), sd(QB, n),
                   sd(n, LANES)),
        compiler_params=pltpu.CompilerParams(vmem_limit_bytes=VMEM_LIMIT),
        name="sample_prep",
    )(proj, ba, cb3, conv_w, alog_row, dtb_row, cos2, sin2)


SAMPLE_ROWS = 8


def _sample_state_kernel(sd_ref, sr_ref, va_ref, vb_ref, bg_ref, qat_ref, kat_ref, qbt_ref, kbt_ref,
                         logg_ref, sdn_ref, srn_ref, oa_ref, ob_ref):
    base = pl.program_id(0) * SAMPLE_ROWS
    nseq = qat_ref.shape[1]
    lane = lax.broadcasted_iota(jnp.int32, (DK_A, nseq), 1)

    def column(t_ref, h, seq):
        blk = t_ref[h * DK_A:(h + 1) * DK_A, :]
        return jnp.sum(jnp.where(lane == seq, blk, 0.0), axis=1, keepdims=True)

    for j in range(SAMPLE_ROWS):
        seq = base + j
        bg_row = bg_ref[j:j + 1, :]
        for h in range(H_A):
            kcol, qcol = column(kat_ref, h, seq), column(qat_ref, h, seq)
            beta = _lane_pick(bg_row, h)
            eg = _lane_pick(bg_row, H_A + h)
            s0 = sd_ref[j, h]
            v = va_ref[j:j + 1, h * DV_A:(h + 1) * DV_A]
            ks = jnp.sum(kcol * s0, axis=0, keepdims=True)
            v_new = beta * v - (beta * eg) * ks
            s1 = s0 * eg + kcol * v_new
            sdn_ref[j, h] = s1
            oa_ref[j:j + 1, h * DV_A:(h + 1) * DV_A] = jnp.sum(qcol * s1, axis=0, keepdims=True)
        for h in range(H_B):
            kcol, qcol = column(kbt_ref, h, seq), column(qbt_ref, h, seq)
            gamma = jnp.exp(logg_ref[h][:, 0:1])
            v = vb_ref[j:j + 1, h * DV_B:(h + 1) * DV_B]
            s1 = sr_ref[j, h] * gamma + kcol * v
            srn_ref[j, h] = s1
            ob_ref[j:j + 1, h * DV_B:(h + 1) * DV_B] = jnp.sum(qcol * s1, axis=0, keepdims=True)


def _sample_state(sd, sr, va, proj, bg, qat, kat, qbt, kbt, logg_tab):
    n = sd.shape[0]
    r = SAMPLE_ROWS
    full = lambda a: pl.BlockSpec(a.shape, lambda i: (0,) * a.ndim)
    return pl.pallas_call(
        _sample_state_kernel,
        grid=(n // r,),
        in_specs=[pl.BlockSpec((r, H_A, DK_A, DV_A), lambda i: (i, 0, 0, 0)),
                  pl.BlockSpec((r, H_B, DK_B, DV_B), lambda i: (i, 0, 0, 0)),
                  pl.BlockSpec((r, VA), lambda i: (i, 0)),
                  pl.BlockSpec((r, VB), lambda i: (i, COL_VB // VB)),
                  pl.BlockSpec((r, LANES), lambda i: (i, 0)),
                  full(qat), full(kat), full(qbt), full(kbt), full(logg_tab)],
        out_specs=(pl.BlockSpec((r, H_A, DK_A, DV_A), lambda i: (i, 0, 0, 0)),
                   pl.BlockSpec((r, H_B, DK_B, DV_B), lambda i: (i, 0, 0, 0)),
                   pl.BlockSpec((r, VA), lambda i: (i, 0)),
                   pl.BlockSpec((r, VB), lambda i: (i, 0))),
        out_shape=(jax.ShapeDtypeStruct(sd.shape, F32), jax.ShapeDtypeStruct(sr.shape, F32),
                   jax.ShapeDtypeStruct((n, VA), F32), jax.ShapeDtypeStruct((n, VB), F32)),
        compiler_params=_cparams("parallel"),
        name="sample_state",
    )(sd, sr, va, proj, bg, qat, kat, qbt, kbt, logg_tab)


def _postmix_kernel(oa_ref, ob_ref, ga_ref, gb_ref, sa_ref, sb_ref, x_ref, gna_ref, gnb_ref,
                    wa_ref, wb_ref, wo_ref, nf_ref, wq_ref, keys_ref, h_ref, hnt_ref, st_ref):
    def gated(o_ref, g_ref, gn_ref, heads, dv):
        parts = []
        for h in range(heads):
            o = o_ref[:, h * dv:(h + 1) * dv]
            y = o * lax.rsqrt(jnp.mean(o * o, axis=-1, keepdims=True) + EPS) * gn_ref[...]
            parts.append(y * _silu(g_ref[:, h * dv:(h + 1) * dv]))
        return jnp.concatenate(parts, axis=1)

    br_a = _dot(gated(oa_ref, ga_ref, gna_ref, H_A, DV_A), wa_ref[...])
    br_b = _dot(gated(ob_ref, gb_ref, gnb_ref, H_B, DV_B), wb_ref[...])
    merged = _sigmoid(sa_ref[...]) * br_a + _sigmoid(sb_ref[...]) * br_b
    hres = x_ref[...] + _dot(merged, wo_ref[...])
    h_ref[...] = hres
    hn = hres * lax.rsqrt(jnp.mean(hres * hres, axis=-1, keepdims=True) + EPS) * nf_ref[...]
    hnt_ref[...] = hn.T.astype(BF16)
    q = _dot(hn, wq_ref[...])
    for hp in range(2 * PEER_HEADS):
        st_ref[hp] = _dot_nt(keys_ref[hp], q[:, hp * N_KEYS:(hp + 1) * N_KEYS])


def _postmix(oa, ob, proj, x2, gn_a, gn_b, w_a, w_b, w_o, norm_ffn, w_q, keys):
    t = x2.shape[0]
    tm = min(t, 512)
    full = lambda a: pl.BlockSpec(a.shape, lambda i: (0,) * a.ndim, pipeline_mode=pl.Buffered(1))
    pcol = lambda width, off: pl.BlockSpec((tm, width), lambda i: (i, off // width))
    return pl.pallas_call(
        _postmix_kernel,
        grid=(t // tm,),
        in_specs=[pl.BlockSpec((tm, VA), lambda i: (i, 0)), pl.BlockSpec((tm, VB), lambda i: (i, 0)),
                  pcol(VA, COL_GATE_A), pcol(VB, COL_GATE_B), pcol(D_MODEL, COL_SEL_A),
                  pcol(D_MODEL, COL_SEL_B), pl.BlockSpec((tm, D_MODEL), lambda i: (i, 0)),
                  full(gn_a), full(gn_b), full(w_a), full(w_b), full(w_o), full(norm_ffn), full(w_q),
                  full(keys)],
        out_specs=(pl.BlockSpec((tm, D_MODEL), lambda i: (i, 0)),
                   pl.BlockSpec((D_MODEL, tm), lambda i: (0, i)),
                   pl.BlockSpec((2 * PEER_HEADS, N_KEYS, tm), lambda i: (0, 0, i))),
        out_shape=(jax.ShapeDtypeStruct((t, D_MODEL), F32), jax.ShapeDtypeStruct((D_MODEL, t), BF16),
                   jax.ShapeDtypeStruct((2 * PEER_HEADS, N_KEYS, t), F32)),
        compiler_params=_cparams("parallel"),
        name="postmix",
    )(oa, ob, proj, proj, proj, proj, x2, gn_a, gn_b, w_a, w_b, w_o, norm_ffn, w_q, keys)


A_PER_CHUNK = 2 * SUBLANES
E_CHUNK = A_PER_CHUNK * N_KEYS
PEER_TOKENS = 256

def _batcher_pairs(n):
    pairs = []
    p = 1
    while p < n:
        k = p
        while k >= 1:
            for j in range(k % p, n - k, 2 * k):
                for i in range(min(k, n - j - k)):
                    if (i + j) // (2 * p) == (i + j + k) // (2 * p):
                        pairs.append((i + j, i + j + k))
            k //= 2
        p *= 2
    return pairs


_SORT_TOPK = _batcher_pairs(PEER_TOPK)


def _vmax(x, y):
    if x is None:
        return y
    return x if y is None else jnp.maximum(x, y)


def _exchange(x, y):
    if x is None or y is None:
        return _vmax(x, y), None
    return jnp.maximum(x, y), jnp.minimum(x, y)


def _sort_desc(v):
    v = list(v)
    for i, j in _SORT_TOPK:
        v[i], v[j] = _exchange(v[i], v[j])
    return v


def _merge_top(v, w):
    n = len(v)
    c = [_vmax(v[i], w[n - 1 - i]) for i in range(n)]
    d = n // 2
    while d >= 1:
        for i in range(n):
            if (i & d) == 0:
                c[i], c[i + d] = _exchange(c[i], c[i + d])
        d //= 2
    return c


def _top_scores(x):
    v = _sort_desc([x[i * SUBLANES:(i + 1) * SUBLANES, :] for i in range(N_KEYS // SUBLANES)])
    shift = SUBLANES // 2
    while shift >= 1:
        v = _merge_top(v, [pltpu.roll(t, shift, axis=0) for t in v])
        shift //= 2
    return v


def _peer_select(st_ref, p1_ref, p2_ref, th_ref, tb):
    sub = lax.broadcasted_iota(jnp.int32, (SUBLANES, LANES), 0)
    for tg in range(tb // LANES):
        lanes = slice(tg * LANES, (tg + 1) * LANES)
        top = [[None] * PEER_TOPK for _ in range(2)]
        for h in range(PEER_HEADS):
            for p in range(2):
                v = _top_scores(st_ref[2 * h + p, :, lanes])
                for r in range(PEER_TOPK):
                    top[p][r] = v[r] if h == 0 else jnp.where(sub == h, v[r], top[p][r])
        sums = [[top[0][r1] + top[1][r2] if (r1 + 1) * (r2 + 1) <= PEER_TOPK else None
                 for r2 in range(PEER_TOPK)] for r1 in range(PEER_TOPK)]
        best = sums[0]
        for r1 in range(1, PEER_TOPK):
            best = _merge_top(best, sums[r1])
        z = jnp.ones((SUBLANES, LANES), F32)
        for r in range(1, PEER_TOPK):
            z = z + jnp.exp(best[r] - best[0])
        tau = best[PEER_TOPK - 1]
        theta = []
        for r1 in range(PEER_TOPK):
            th = None
            for r2 in range(PEER_TOPK):
                if sums[r1][r2] is not None:
                    cand = jnp.where(sums[r1][r2] >= tau, top[1][r2], jnp.inf)
                    th = cand if th is None else jnp.minimum(th, cand)
            theta.append(th)
        for h in range(PEER_HEADS):
            row = slice(h, h + 1)
            s1 = st_ref[2 * h, :, lanes]
            th_a = jnp.full(s1.shape, jnp.inf, F32)
            for r1 in range(PEER_TOPK):
                th_a = jnp.where(s1 == top[0][r1][row, :], theta[r1][row, :], th_a)
            th_ref[h, :, lanes] = th_a
            p1_ref[h, :, lanes] = jnp.exp(s1 - top[0][0][row, :]) / z[row, :] * 0.5
            p2_ref[h, :, lanes] = jnp.exp(st_ref[2 * h + 1, :, lanes] - top[1][0][row, :])


def _peer_coefficients(chunk, st_ref, p1_ref, p2_ref, th_ref, act_ref, coef_ref, tb):
    a0 = pl.multiple_of(chunk * A_PER_CHUNK, A_PER_CHUNK)
    for tg in range(tb // LANES):
        lanes = slice(tg * LANES, (tg + 1) * LANES)
        tht = [th_ref[h, pl.ds(a0, A_PER_CHUNK), lanes] for h in range(PEER_HEADS)]
        p1t = [p1_ref[h, pl.ds(a0, A_PER_CHUNK), lanes] for h in range(PEER_HEADS)]
        for r in range(A_PER_CHUNK):
            rows = slice(r * N_KEYS, (r + 1) * N_KEYS)
            gate = jnp.zeros((N_KEYS, LANES), F32)
            for h in range(PEER_HEADS):
                keep = st_ref[2 * h + 1, :, lanes] >= tht[h][r:r + 1, :]
                gate = gate + jnp.where(keep, p2_ref[h, :, lanes], 0.0) * p1t[h][r:r + 1, :]
            coef_ref[rows, lanes] = (gate * _gelu_times2(act_ref[rows, lanes])).astype(BF16)


def _peer_kernel(hnt_ref, hntn_ref, st_ref, u0_ref, uo_ref, ue_ref, vte_ref, vto_ref, ft_ref,
                 p1_ref, p2_ref, th_ref, acta_ref, actb_ref, coefa_ref, coefb_ref):
    i = pl.program_id(0)
    s = pl.program_id(1)
    last = pl.num_programs(1) - 1
    tb = hnt_ref.shape[1]
    dot = functools.partial(jnp.dot, preferred_element_type=F32)

    @pl.when(s == 0)
    def _():
        _peer_select(st_ref, p1_ref, p2_ref, th_ref, tb)
        ft_ref[...] = jnp.zeros_like(ft_ref)

    @pl.when((s == 0) & (i == 0))
    def _():
        acta_ref[...] = dot(u0_ref[...], hnt_ref[...])

    actb_ref[...] = dot(uo_ref[...], hnt_ref[...])
    _peer_coefficients(2 * s, st_ref, p1_ref, p2_ref, th_ref, acta_ref, coefa_ref, tb)
    ft_ref[...] += dot(vte_ref[...], coefa_ref[...])
    acta_ref[...] = dot(ue_ref[...], jnp.where(s == last, hntn_ref[...], hnt_ref[...]))
    _peer_coefficients(2 * s + 1, st_ref, p1_ref, p2_ref, th_ref, actb_ref, coefb_ref, tb)
    ft_ref[...] += dot(vto_ref[...], coefb_ref[...])


def _peer(hnt, st, u_bf, vt_bf):
    t = hnt.shape[1]
    tb = min(t, PEER_TOKENS)
    n_chunks = u_bf.shape[0] // E_CHUNK
    n_blocks = t // tb
    steps = n_chunks // 2
    ublk = lambda fn: pl.BlockSpec((E_CHUNK, D_MODEL), fn)
    vblk = lambda fn: pl.BlockSpec((None, D_MODEL, E_CHUNK), fn)
    return pl.pallas_call(
        _peer_kernel,
        grid=(n_blocks, steps),
        in_specs=[pl.BlockSpec((D_MODEL, tb), lambda i, s: (0, i)),
                  pl.BlockSpec((D_MODEL, tb), lambda i, s: (0, jnp.minimum(i + 1, n_blocks - 1))),
                  pl.BlockSpec((2 * PEER_HEADS, N_KEYS, tb), lambda i, s: (0, 0, i)),
                  ublk(lambda i, s: (0, 0)),
                  ublk(lambda i, s: (2 * s + 1, 0)),
                  ublk(lambda i, s: ((2 * s + 2) % n_chunks, 0)),
                  vblk(lambda i, s: (2 * s, 0, 0)),
                  vblk(lambda i, s: (2 * s + 1, 0, 0))],
        out_specs=pl.BlockSpec((D_MODEL, tb), lambda i, s: (0, i)),
        out_shape=jax.ShapeDtypeStruct((D_MODEL, t), F32),
        scratch_shapes=[pltpu.VMEM((PEER_HEADS, N_KEYS, tb), F32),
                        pltpu.VMEM((PEER_HEADS, N_KEYS, tb), F32),
                        pltpu.VMEM((PEER_HEADS, N_KEYS, tb), F32),
                        pltpu.VMEM((E_CHUNK, tb), F32),
                        pltpu.VMEM((E_CHUNK, tb), F32),
                        pltpu.VMEM((E_CHUNK, tb), BF16),
                        pltpu.VMEM((E_CHUNK, tb), BF16)],
        compiler_params=_cparams("arbitrary", "arbitrary"),
        name="peer_dense",
    )(hnt, hnt, st, u_bf, u_bf, u_bf, vt_bf, vt_bf)


def _final_kernel(h_ref, ft_ref, g_ref, y_ref):
    hres = h_ref[...] + ft_ref[...].T
    y_ref[...] = hres * lax.rsqrt(jnp.mean(hres * hres, axis=-1, keepdims=True) + EPS) * g_ref[...]


def _final(h, ft, g):
    t = h.shape[0]
    tm = min(t, 512)
    return pl.pallas_call(
        _final_kernel,
        grid=(t // tm,),
        in_specs=[pl.BlockSpec((tm, D_MODEL), lambda i: (i, 0)),
                  pl.BlockSpec((D_MODEL, tm), lambda i: (0, i)),
                  pl.BlockSpec((1, D_MODEL), lambda i: (0, 0))],
        out_specs=pl.BlockSpec((tm, D_MODEL), lambda i: (i, 0)),
        out_shape=jax.ShapeDtypeStruct((t, D_MODEL), F32),
        compiler_params=_cparams("parallel"),
        name="final_norm",
    )(h, ft, g)


def _pack_weights(norm_mix, w_in, conv_w, a_log, dt_bias, gn_a, gn_b, w_br_a, w_br_b, w_out,
                  norm_ffn, peer_wq, peer_keys, peer_u, peer_v, norm_final):
    w = w_in[0]
    n_small = 2 * H_A
    c0 = CONV_CH + VA
    w_main = jnp.concatenate([w[:, :c0], w[:, c0 + n_small:]], axis=1).astype(BF16)
    w_ba = jnp.pad(w[:, c0:c0 + n_small], ((0, 0), (0, LANES - n_small)))
    wb_hi = w_ba.astype(BF16)
    wb_lo = (w_ba - wb_hi.astype(F32)).astype(BF16)
    pad_row = lambda v: jnp.pad(v.reshape(1, H_A), ((0, 0), (H_A, LANES - 2 * H_A)))
    logg = jnp.log1p(-(2.0 ** (-5.0 - jnp.arange(H_B, dtype=F32))))
    return dict(
        norm_mix=norm_mix[0].reshape(1, D_MODEL), w_main=w_main, wb_hi=wb_hi, wb_lo=wb_lo,
        conv_w=conv_w[0], alog_row=pad_row(a_log[0]), dtb_row=pad_row(dt_bias[0]),
        logg_tab=jnp.broadcast_to(logg[:, None, None], (H_B, 1, LANES)),
        gn_a=gn_a[0].reshape(1, DV_A), gn_b=gn_b[0].reshape(1, DV_B),
        w_a=w_br_a[0].astype(BF16), w_b=w_br_b[0].astype(BF16), w_o=w_out[0].astype(BF16),
        norm_ffn=norm_ffn[0].reshape(1, D_MODEL), w_q=peer_wq[0].astype(BF16),
        keys=peer_keys[0].reshape(2 * PEER_HEADS, N_KEYS, N_KEYS).astype(BF16),
        u_bf=peer_u[0].astype(BF16),
        vt_bf=jnp.transpose(peer_v[0].astype(BF16).reshape(-1, E_CHUNK, D_MODEL), (0, 2, 1)),
        norm_final=norm_final.reshape(1, D_MODEL))


def _channel_mix(oa, ob, proj, x2, p):
    h, hnt, st = _postmix(oa, ob, proj, x2, p["gn_a"], p["gn_b"], p["w_a"], p["w_b"], p["w_o"],
                          p["norm_ffn"], p["w_q"], p["keys"])
    ft = _peer(hnt, st, p["u_bf"], p["vt_bf"])
    return _final(h, ft, p["norm_final"])


def _prompt_group(x, p):
    b, n, _ = x.shape
    x2 = x.reshape(b * n, D_MODEL)
    proj, ba = _inproj(x2, p["norm_mix"], p["w_main"], p["wb_hi"], p["wb_lo"])
    proj3 = proj.reshape(b, n, N_MAIN)
    oa, s_delta = _delta_prompt(proj3, ba.reshape(b, n, LANES), p["conv_w"], p["alog_row"], p["dtb_row"])
    cos2, sin2 = _rope_tables(n, 0)
    ob, s_ret = _ret_prompt(proj3, cos2, sin2, p["logg_tab"])
    y = _channel_mix(oa.reshape(b * n, VA), ob.reshape(b * n, VB), proj, x2, p)
    conv_new = proj3[:, n - (CONV_W - 1):, :CONV_CH]
    return y.reshape(b, n, D_MODEL), conv_new[None], s_delta[None], s_ret[None]


def _sample_group(x, conv_buf, s_delta, s_ret, p):
    n = x.shape[0]
    x2 = x.reshape(n, D_MODEL)
    proj, ba = _inproj(x2, p["norm_mix"], p["w_main"], p["wb_hi"], p["wb_lo"])
    cos2, sin2 = _rope_tables(8, PAST_LEN)
    cb3 = jnp.transpose(conv_buf, (1, 0, 2))
    cnew, va, qat, kat, qbt, kbt, bg = _sample_prep(proj, ba, cb3, p["conv_w"], p["alog_row"],
                                                    p["dtb_row"], cos2, sin2)
    sd_new, sr_new, oa, ob = _sample_state(s_delta, s_ret, va, proj, bg, qat, kat, qbt, kbt,
                                           p["logg_tab"])
    y = _channel_mix(oa, ob, proj, x2, p)
    return (y.reshape(n, 1, D_MODEL), jnp.transpose(cnew, (1, 0, 2))[None], sd_new[None], sr_new[None])


def kernel(x_prompt, x_sample, state_conv_a, state_delta, state_ret, norm_mix, w_in, conv_w, a_log, dt_bias, gn_a, gn_b, w_br_a, w_br_b, w_out, norm_ffn, peer_wq, peer_keys, peer_u, peer_v, norm_final):
    assert w_in.shape[0] == 1 and x_sample.shape[1] == 1
    p = _pack_weights(norm_mix, w_in, conv_w, a_log, dt_bias, gn_a, gn_b, w_br_a, w_br_b, w_out,
                      norm_ffn, peer_wq, peer_keys, peer_u, peer_v, norm_final)
    y_p, conv_p, delta_p, ret_p = _prompt_group(x_prompt, p)
    y_s, conv_s, delta_s, ret_s = _sample_group(x_sample, state_conv_a[0], state_delta[0],
                                                state_ret[0], p)
    return (y_p, y_s, conv_p, delta_p, ret_p, conv_s, delta_s, ret_s)
```
